```python
import math
import jax, jax.numpy as jnp
from jax import lax
import numpy as np

D_MODEL = 1024
BATCH = 16
SEQ = 2048
DEPTH = 4

N_MEM = 256
EPS = 1e-6
D_FF = 2816
SSD_HEADS = 16
SSD_HEAD_DIM = 64
SSD_INNER = SSD_HEADS * SSD_HEAD_DIM
SSD_GROUPS = 2
SSD_HPG = SSD_HEADS // SSD_GROUPS
SSD_STATE = 128
SSD_CONV = 5
SSD_CHUNK = 128
SSD_CONV_CH = SSD_INNER + 2 * SSD_GROUPS * SSD_STATE
MLA_HEADS = 8
MLA_Q_RANK = 512
MLA_KV_RANK = 256
MLA_NOPE = 64
MLA_ROPE = 32
MLA_V = 64
MLA_Q_BLOCK = 128
ROPE_THETA = 10000.0
IN_SPLITS = (
    SSD_INNER,
    SSD_INNER + SSD_CONV_CH,
    SSD_INNER + SSD_CONV_CH + 2 * SSD_HEADS,
    SSD_INNER + SSD_CONV_CH + 2 * SSD_HEADS + MLA_Q_RANK,
    SSD_INNER + SSD_CONV_CH + 2 * SSD_HEADS + MLA_Q_RANK + MLA_KV_RANK,
)
IN_COLS = IN_SPLITS[-1] + MLA_ROPE
MIX_WIDTH = SSD_INNER + MLA_HEADS * MLA_V
FNET_GROUPS = 4
FNET_GROUP_CH = D_MODEL // FNET_GROUPS
XA_HEADS = 4
XA_HEAD_DIM = D_MODEL // XA_HEADS
N_EVEN = (DEPTH + 1) // 2
N_ODD = DEPTH // 2

kernel_name = "hybrid_ssd_mla_fnet_macaron_encoder"


def rmsnorm(x, w):
    xf = x.astype(jnp.float32)
    y = xf * lax.rsqrt(jnp.mean(xf * xf, axis=-1, keepdims=True) + EPS)
    return (y * w.astype(jnp.float32)).astype(x.dtype)


def swiglu_ffn(x, w_gu, w_down):
    g, u = jnp.split(x @ w_gu, 2, axis=-1)
    return (jax.nn.silu(g) * u) @ w_down


def rope_tables(positions):
    inv = 1.0 / (ROPE_THETA ** (jnp.arange(0, MLA_ROPE, 2, dtype=jnp.float32) / MLA_ROPE))
    ang = positions.astype(jnp.float32)[..., None] * inv
    return jnp.cos(ang), jnp.sin(ang)


def apply_rope(x, cos, sin):
    x1, x2 = jnp.split(x.astype(jnp.float32), 2, axis=-1)
    return jnp.concatenate([x1 * cos - x2 * sin, x1 * sin + x2 * cos], axis=-1).astype(x.dtype)


def centred_depthwise_conv(x, w, bias):
    ch = x.shape[-1]
    y = lax.conv_general_dilated(
        x, w[:, None, :].astype(x.dtype), window_strides=(1,),
        padding=[(SSD_CONV // 2, SSD_CONV // 2)],
        dimension_numbers=("NWC", "WIO", "NWC"), feature_group_count=ch)
    return y + bias.astype(x.dtype)


def ssd_chunked(xdt, la, bm, cm):
    b, l, g, r, p = xdt.shape
    n = bm.shape[-1]
    q = SSD_CHUNK
    nc = l // q
    xdt = xdt.reshape(b, nc, q, g, r, p)
    la = la.reshape(b, nc, q, g, r)
    bm = bm.reshape(b, nc, q, g, n)
    cm = cm.reshape(b, nc, q, g, n)
    cum = jnp.cumsum(la, axis=2)
    lower = jnp.tril(jnp.ones((q, q), dtype=bool))[:, :, None, None]
    seg = cum[:, :, :, None] - cum[:, :, None, :]
    decay = jnp.exp(jnp.where(lower, seg, -jnp.inf))
    cb = jnp.einsum("bzlgn,bzsgn->bzlsg", cm, bm)
    y_diag = jnp.einsum("bzlsgr,bzsgrp->bzlgrp", cb[..., None] * decay, xdt)
    decay_end = jnp.exp(cum[:, :, -1:] - cum)
    states = jnp.einsum("bzsgn,bzsgrp->bzgrpn", bm, xdt * decay_end[..., None])
    chunk_decay = jnp.exp(cum[:, :, -1])

    def carry_step(h, inp):
        s_z, d_z = inp
        return h * d_z[..., None, None] + s_z, h

    h0 = jnp.zeros((b, g, r, p, n), xdt.dtype)
    _, h_in = lax.scan(carry_step, h0, (jnp.moveaxis(states, 1, 0), jnp.moveaxis(chunk_decay, 1, 0)))
    h_in = jnp.moveaxis(h_in, 0, 1)
    y_off = jnp.einsum("bzlgn,bzgrpn->bzlgrp", cm, h_in) * jnp.exp(cum)[..., None]
    return (y_diag + y_off).reshape(b, l, g, r, p)


def block_attention(q, k, v, scale):
    b, s, h, dk = q.shape
    nb = s // MLA_Q_BLOCK
    qb = jnp.moveaxis(q.reshape(b, nb, MLA_Q_BLOCK, h, dk), 1, 0)

    def one_block(qi):
        sc = jnp.einsum("bqhd,bkhd->bhqk", qi, k).astype(jnp.float32) * scale
        pr = jax.nn.softmax(sc, axis=-1).astype(v.dtype)
        return jnp.einsum("bhqk,bkhd->bqhd", pr, v)

    o = lax.map(one_block, qb)
    return jnp.moveaxis(o, 0, 1).reshape(b, s, h, v.shape[-1])


def ssd_mla_mixer(u, cos, sin, w_in, conv_w, conv_b, dt_bias, a_log, ssd_d, ssd_norm,
                  q_norm, w_uq, kv_norm, w_ukv, w_out):
    b, s, _ = u.shape
    z, xbc, dt_raw, c_q, c_kv, k_rope = jnp.split(u @ w_in, IN_SPLITS, axis=-1)
    xbc = jax.nn.silu(centred_depthwise_conv(xbc, conv_w, conv_b))
    xs, bm, cm = jnp.split(xbc, [SSD_INNER, SSD_INNER + SSD_GROUPS * SSD_STATE], axis=-1)
    xs = xs.astype(jnp.float32).reshape(b, s, SSD_GROUPS, SSD_HPG, SSD_HEAD_DIM)
    bm = bm.astype(jnp.float32).reshape(b, s, SSD_GROUPS, SSD_STATE)
    cm = cm.astype(jnp.float32).reshape(b, s, SSD_GROUPS, SSD_STATE)
    dt = jax.nn.softplus(dt_raw.astype(jnp.float32).reshape(b, s, 2, SSD_GROUPS, SSD_HPG)
                         + dt_bias.astype(jnp.float32).reshape(2, SSD_GROUPS, SSD_HPG))
    a = -jnp.exp(a_log.astype(jnp.float32)).reshape(2, SSD_GROUPS, SSD_HPG)
    y_fwd = ssd_chunked(xs * dt[:, :, 0, ..., None], dt[:, :, 0] * a[0], bm, cm)
    flip = lambda t: jnp.flip(t, axis=1)
    y_bwd = flip(ssd_chunked(flip(xs * dt[:, :, 1, ..., None]), flip(dt[:, :, 1] * a[1]), flip(bm), flip(cm)))
    y = y_fwd + y_bwd + xs * ssd_d.astype(jnp.float32).reshape(SSD_GROUPS, SSD_HPG, 1)
    y = y.reshape(b, s, SSD_INNER).astype(u.dtype)
    y_ssd = rmsnorm(y * jax.nn.silu(z), ssd_norm)
    q = (rmsnorm(c_q, q_norm) @ w_uq).reshape(b, s, MLA_HEADS, MLA_NOPE + MLA_ROPE)
    q_nope, q_pe = jnp.split(q, [MLA_NOPE], axis=-1)
    q_pe = apply_rope(q_pe, cos[:, :, None], sin[:, :, None])
    kv = (rmsnorm(c_kv, kv_norm) @ w_ukv).reshape(b, s, MLA_HEADS, MLA_NOPE + MLA_V)
    k_nope, v = jnp.split(kv, [MLA_NOPE], axis=-1)
    k_pe = apply_rope(k_rope, cos, sin)
    k = jnp.concatenate([k_nope, jnp.broadcast_to(k_pe[:, :, None], (b, s, MLA_HEADS, MLA_ROPE))], axis=-1)
    qf = jnp.concatenate([q_nope, q_pe], axis=-1)
    o_mla = block_attention(qf, k, v, (MLA_NOPE + MLA_ROPE) ** -0.5).reshape(b, s, MLA_HEADS * MLA_V)
    return jnp.concatenate([y_ssd, o_mla], axis=-1) @ w_out


def fourier_mixer(u, w_out):
    b, s, d = u.shape
    ug = jnp.moveaxis(u.astype(jnp.float32).reshape(b, s, FNET_GROUPS, FNET_GROUP_CH), 2, 1)
    f = jnp.fft.fft2(ug, norm="ortho").real
    return jnp.moveaxis(f, 1, 2).reshape(b, s, d).astype(u.dtype) @ w_out


def memory_cross_attention(hn, mem_n, wq, wkv, wo):
    b, s, d = hn.shape
    t = mem_n.shape[1]
    q = (hn @ wq).reshape(b, s, XA_HEADS, XA_HEAD_DIM)
    k, v = jnp.split(mem_n @ wkv, 2, axis=-1)
    k = k.reshape(b, t, XA_HEADS, XA_HEAD_DIM)
    v = v.reshape(b, t, XA_HEADS, XA_HEAD_DIM)
    sc = jnp.einsum("bqhd,bkhd->bhqk", q, k).astype(jnp.float32) * (XA_HEAD_DIM ** -0.5)
    pr = jax.nn.softmax(sc, axis=-1).astype(v.dtype)
    o = jnp.einsum("bhqk,bkhd->bqhd", pr, v).reshape(b, s, d)
    return o @ wo


def setup_inputs(seed: int = 0) -> dict:
    key = jax.random.key(seed)
    ks = iter(jax.random.split(key, 32))
    f32 = jnp.float32

    def nrm(shape, fan_in):
        return jax.random.normal(next(ks), shape, f32) * (fan_in ** -0.5)

    def gain(shape):
        return 1.0 + 0.02 * jax.random.normal(next(ks), shape, f32)

    L, E, O = DEPTH, N_EVEN, N_ODD
    x = jax.random.normal(next(ks), (BATCH, SEQ, D_MODEL), f32)
    mem = jax.random.normal(next(ks), (BATCH, N_MEM, D_MODEL), f32)
    positions = jnp.broadcast_to(jnp.arange(SEQ, dtype=jnp.int32)[None], (BATCH, SEQ))
    mem_norm = gain((D_MODEL,))
    final_norm = gain((D_MODEL,))
    ffn1_norm = gain((L, D_MODEL))
    ffn1_w_gu = nrm((L, D_MODEL, 2 * D_FF), D_MODEL)
    ffn1_w_down = nrm((L, D_FF, D_MODEL), D_FF)
    mix_norm = gain((L, D_MODEL))
    xa_norm = gain((L, D_MODEL))
    xa_wq = nrm((L, D_MODEL, D_MODEL), D_MODEL)
    xa_wkv = nrm((L, D_MODEL, 2 * D_MODEL), D_MODEL)
    xa_wo = nrm((L, D_MODEL, D_MODEL), D_MODEL)
    ffn2_norm = gain((L, D_MODEL))
    ffn2_w_gu = nrm((L, D_MODEL, 2 * D_FF), D_MODEL)
    ffn2_w_down = nrm((L, D_FF, D_MODEL), D_FF)
    w_in = nrm((E, D_MODEL, IN_COLS), D_MODEL)
    conv_w = nrm((E, SSD_CONV, SSD_CONV_CH), SSD_CONV)
    conv_b = 0.02 * jax.random.normal(next(ks), (E, SSD_CONV_CH), f32)
    dt0 = jnp.exp(jax.random.uniform(next(ks), (E, 2, SSD_HEADS), f32,
                                     minval=math.log(1e-3), maxval=math.log(1e-1)))
    dt_bias = dt0 + jnp.log(-jnp.expm1(-dt0))
    a_log = jnp.log(jax.random.uniform(next(ks), (E, 2, SSD_HEADS), f32, minval=1.0, maxval=16.0))
    ssd_d = 1.0 + 0.1 * jax.random.normal(next(ks), (E, SSD_HEADS), f32)
    ssd_norm = gain((E, SSD_INNER))
    q_norm = gain((E, MLA_Q_RANK))
    w_uq = nrm((E, MLA_Q_RANK, MLA_HEADS * (MLA_NOPE + MLA_ROPE)), MLA_Q_RANK)
    kv_norm = gain((E, MLA_KV_RANK))
    w_ukv = nrm((E, MLA_KV_RANK, MLA_HEADS * (MLA_NOPE + MLA_V)), MLA_KV_RANK)
    w_out = nrm((E, MIX_WIDTH, D_MODEL), MIX_WIDTH)
    fnet_w_out = nrm((O, D_MODEL, D_MODEL), D_MODEL)
    return {
        "x": x, "mem": mem, "positions": positions, "mem_norm": mem_norm, "final_norm": final_norm,
        "ffn1_norm": ffn1_norm, "ffn1_w_gu": ffn1_w_gu, "ffn1_w_down": ffn1_w_down,
        "mix_norm": mix_norm, "xa_norm": xa_norm, "xa_wq": xa_wq, "xa_wkv": xa_wkv, "xa_wo": xa_wo,
        "ffn2_norm": ffn2_norm, "ffn2_w_gu": ffn2_w_gu, "ffn2_w_down": ffn2_w_down,
        "w_in": w_in, "conv_w": conv_w, "conv_b": conv_b, "dt_bias": dt_bias, "a_log": a_log,
        "ssd_d": ssd_d, "ssd_norm": ssd_norm, "q_norm": q_norm, "w_uq": w_uq, "kv_norm": kv_norm,
        "w_ukv": w_ukv, "w_out": w_out, "fnet_w_out": fnet_w_out,
    }


def reference(x, mem, positions, mem_norm, final_norm,
              ffn1_norm, ffn1_w_gu, ffn1_w_down,
              mix_norm, xa_norm, xa_wq, xa_wkv, xa_wo,
              ffn2_norm, ffn2_w_gu, ffn2_w_down,
              w_in, conv_w, conv_b, dt_bias, a_log, ssd_d, ssd_norm,
              q_norm, w_uq, kv_norm, w_ukv, w_out, fnet_w_out):
    cos, sin = rope_tables(positions)
    mem_n = rmsnorm(mem, mem_norm)
    h = x
    for layer in range(DEPTH):
        h = h + 0.5 * swiglu_ffn(rmsnorm(h, ffn1_norm[layer]), ffn1_w_gu[layer], ffn1_w_down[layer])
        u = rmsnorm(h, mix_norm[layer])
        if layer % 2 == 0:
            e = layer // 2
            h = h + ssd_mla_mixer(u, cos, sin, w_in[e], conv_w[e], conv_b[e], dt_bias[e], a_log[e],
                                  ssd_d[e], ssd_norm[e], q_norm[e], w_uq[e], kv_norm[e], w_ukv[e], w_out[e])
        else:
            h = h + fourier_mixer(u, fnet_w_out[layer // 2])
        h = h + memory_cross_attention(rmsnorm(h, xa_norm[layer]), mem_n, xa_wq[layer], xa_wkv[layer], xa_wo[layer])
        h = h + 0.5 * swiglu_ffn(rmsnorm(h, ffn2_norm[layer]), ffn2_w_gu[layer], ffn2_w_down[layer])
    return rmsnorm(h, final_norm)
```

```python
import functools
import math

import jax
import jax.numpy as jnp
from jax import lax
from jax.experimental import pallas as pl
from jax.experimental.pallas import tpu as pltpu

EPS = 1e-6
BF = jnp.bfloat16
F32 = jnp.float32

V7X_VMEM_BYTES = 64 * 1024 * 1024
VMEM_LIMIT = V7X_VMEM_BYTES - 8 * 1024 * 1024
LANES = 128

SSD_HEADS = 16
SSD_HEAD_DIM = 64
SSD_GROUPS = 2
SSD_STATE = 128
SSD_CONV = 5
SSD_CHUNK = 128
MLA_HEADS = 8
MLA_Q_RANK = 512
MLA_KV_RANK = 256
MLA_NOPE = 64
MLA_ROPE = 32
MLA_V = 64
ROPE_THETA = 10000.0
FNET_GROUPS = 4
XA_HEADS = 4

NT_DIMS = (((1,), (1,)), ((), ()))
TN_DIMS = (((0,), (0,)), ((), ()))


def _params(*sem):
    return pltpu.CompilerParams(dimension_semantics=sem, vmem_limit_bytes=VMEM_LIMIT)


def _resident(shape, index_map):
    return pl.BlockSpec(shape, index_map, pipeline_mode=pl.Buffered(1))


def _rms(x, w):
    return x * lax.rsqrt(jnp.mean(x * x, axis=-1, keepdims=True) + EPS) * w


def _dot(a, b):
    return jnp.dot(a, b, preferred_element_type=F32)


def _tile(n, pref):
    t = min(n, pref)
    assert n % t == 0, (n, t)
    return t


def _ffn_body(*refs, chunks, final):
    if final:
        h_ref, nw_ref, wg_ref, wu_ref, wd_ref, fw_ref, o_ref = refs
    else:
        h_ref, nw_ref, wg_ref, wu_ref, wd_ref, o_ref = refs
    h = h_ref[...]
    xn = _rms(h, nw_ref[...]).astype(BF)
    acc = jnp.zeros(h.shape, F32)
    for a, b in chunks:
        g = _dot(xn, wg_ref[:, a:b])
        u = _dot(xn, wu_ref[:, a:b])
        act = (jax.nn.silu(g) * u).astype(BF)
        acc = acc + _dot(act, wd_ref[a:b, :])
    out = h + 0.5 * acc
    if final:
        out = _rms(out, fw_ref[...])
    o_ref[...] = out


def _ffn(h, norm_w, w_gu, w_down, layer, final_w=None):
    t, d = h.shape
    f = w_down.shape[1]
    tm = _tile(t, 512)
    step = 768
    chunks = tuple((a, min(a + step, f)) for a in range(0, f, step))
    in_specs = [
        pl.BlockSpec((tm, d), lambda i: (i, 0)),
        pl.BlockSpec((None, 1, d), lambda i: (layer, 0, 0)),
        _resident((None, d, f), lambda i: (layer, 0, 0)),
        _resident((None, d, f), lambda i: (layer, 0, 1)),
        _resident((None, f, d), lambda i: (layer, 0, 0)),
    ]
    args = [h, norm_w, w_gu, w_gu, w_down]
    if final_w is not None:
        in_specs.append(pl.BlockSpec((1, d), lambda i: (0, 0)))
        args.append(final_w)
    return pl.pallas_call(
        functools.partial(_ffn_body, chunks=chunks, final=final_w is not None),
        grid=(t // tm,),
        in_specs=in_specs,
        out_specs=pl.BlockSpec((tm, d), lambda i: (i, 0)),
        out_shape=jax.ShapeDtypeStruct((t, d), F32),
        compiler_params=_params("parallel"),
        name="ffn",
    )(*args)


def _kvproj_body(m_ref, nw_ref, w_ref, o_ref):
    mn = _rms(m_ref[...], nw_ref[...]).astype(BF)
    o_ref[...] = _dot(mn, w_ref[...]).astype(BF)


def _kvproj(mem2d, mem_norm, wkv):
    n, d = mem2d.shape
    nl, _, d2 = wkv.shape
    tm = _tile(n, 512)
    return pl.pallas_call(
        _kvproj_body,
        grid=(nl, n // tm),
        in_specs=[
            pl.BlockSpec((tm, d), lambda l, i: (i, 0)),
            pl.BlockSpec((1, d), lambda l, i: (0, 0)),
            pl.BlockSpec((None, d, d2), lambda l, i: (l, 0, 0)),
        ],
        out_specs=pl.BlockSpec((None, tm, d2), lambda l, i: (l, i, 0)),
        out_shape=jax.ShapeDtypeStruct((nl, n, d2), BF),
        compiler_params=_params("parallel", "parallel"),
        name="xa_kvproj",
    )(mem2d, mem_norm, wkv)


def _xa_body(h_ref, nw_ref, wq_ref, k_ref, v_ref, wo_ref, o_ref, *, heads):
    h = h_ref[...]
    hn = _rms(h, nw_ref[...]).astype(BF)
    dh = h.shape[-1] // heads
    q = (_dot(hn, wq_ref[...]) * (dh ** -0.5)).astype(BF)
    outs = []
    for i in range(heads):
        sl = slice(i * dh, (i + 1) * dh)
        s = lax.dot_general(q[:, sl], k_ref[:, sl], NT_DIMS, preferred_element_type=F32)
        p = jnp.exp(s - jnp.max(s, axis=-1, keepdims=True))
        l = jnp.sum(p, axis=-1, keepdims=True)
        outs.append((_dot(p.astype(BF), v_ref[:, sl]) / l).astype(BF))
    o = jnp.concatenate(outs, axis=-1)
    o_ref[...] = h + _dot(o, wo_ref[...])


def _xattn(h, norm_w, wq, kv, wo, layer, batch):
    t, d = h.shape
    s = t // batch
    nm = kv.shape[2]
    tm = _tile(s, 512)
    ns = s // tm
    return pl.pallas_call(
        functools.partial(_xa_body, heads=XA_HEADS),
        grid=(batch, ns),
        in_specs=[
            pl.BlockSpec((tm, d), lambda b, i: (b * ns + i, 0)),
            pl.BlockSpec((None, 1, d), lambda b, i: (layer, 0, 0)),
            _resident((None, d, d), lambda b, i: (layer, 0, 0)),
            pl.BlockSpec((None, None, nm, d), lambda b, i: (layer, b, 0, 0)),
            pl.BlockSpec((None, None, nm, d), lambda b, i: (layer, b, 0, 1)),
            _resident((None, d, d), lambda b, i: (layer, 0, 0)),
        ],
        out_specs=pl.BlockSpec((tm, d), lambda b, i: (b * ns + i, 0)),
        out_shape=jax.ShapeDtypeStruct((t, d), F32),
        compiler_params=_params("parallel", "parallel"),
        name="xattn",
    )(h, norm_w, wq, kv, kv, wo)


def _inproj_body(h_ref, nw_ref, w_ref, qn_ref, kvn_ref, wqa_ref, wqb_ref, wkn_ref, wv_ref,
                 cos_ref, sin_ref, z_ref, xbc_ref, dt_ref, q_ref, k_ref, v_ref, *, cols, scale):
    c_z, c_xbc, c_dt, c_cq, c_ckv, c_kr = cols
    u = _rms(h_ref[...], nw_ref[...]).astype(BF)
    z_ref[...] = _dot(u, w_ref[:, c_z[0]:c_z[1]]).astype(BF)
    xbc_ref[...] = _dot(u, w_ref[:, c_xbc[0]:c_xbc[1]])
    dt_ref[...] = _dot(u, w_ref[:, c_dt[0]:c_dt[1]])
    cqn = _rms(_dot(u, w_ref[:, c_cq[0]:c_cq[1]]), qn_ref[...]).astype(BF)
    ckvn = _rms(_dot(u, w_ref[:, c_ckv[0]:c_ckv[1]]), kvn_ref[...]).astype(BF)
    kr = _dot(u, w_ref[:, c_kr[0]:c_kr[1]])
    cos_t = cos_ref[...]
    sin_t = sin_ref[...]
    kp = kr[:, :LANES] * cos_t + kr[:, LANES:] * sin_t
    qa = _dot(cqn, wqa_ref[...])
    qb = _dot(cqn, wqb_ref[...])
    kn = _dot(ckvn, wkn_ref[...])
    for i in range(qa.shape[-1] // LANES):
        sl = slice(i * LANES, (i + 1) * LANES)
        q_ref[:, sl] = ((qa[:, sl] * cos_t + qb[:, sl] * sin_t) * scale).astype(BF)
        k_ref[:, sl] = (kn[:, sl] + kp).astype(BF)
    v_ref[...] = _dot(ckvn, wv_ref[...]).astype(BF)


def _inproj(h, norm_w, w_all, cols, q_norm, kv_norm, wqa, wqb, wkn, wv, cos_t, sin_t, layer, e):
    t, d = h.shape
    tm = _tile(t, 512)
    wc = w_all.shape[-1]
    n_z = cols[0][1] - cols[0][0]
    n_xbc = cols[1][1] - cols[1][0]
    hq = wqa.shape[-1]
    row = lambda i: (i, 0)
    return pl.pallas_call(
        functools.partial(_inproj_body, cols=cols, scale=(MLA_NOPE + MLA_ROPE) ** -0.5),
        grid=(t // tm,),
        in_specs=[
            pl.BlockSpec((tm, d), row),
            pl.BlockSpec((None, 1, d), lambda i: (layer, 0, 0)),
            _resident((None, d, wc), lambda i: (e, 0, 0)),
            pl.BlockSpec((None, 1, MLA_Q_RANK), lambda i: (e, 0, 0)),
            pl.BlockSpec((None, 1, MLA_KV_RANK), lambda i: (e, 0, 0)),
            _resident((None, MLA_Q_RANK, hq), lambda i: (e, 0, 0)),
            _resident((None, MLA_Q_RANK, hq), lambda i: (e, 0, 0)),
            _resident((None, MLA_KV_RANK, hq), lambda i: (e, 0, 0)),
            _resident((None, MLA_KV_RANK, hq), lambda i: (e, 0, 0)),
            pl.BlockSpec((tm, LANES), row),
            pl.BlockSpec((tm, LANES), row),
        ],
        out_specs=[
            pl.BlockSpec((tm, n_z), row),
            pl.BlockSpec((tm, n_xbc), row),
            pl.BlockSpec((tm, LANES), row),
            pl.BlockSpec((tm, hq), row),
            pl.BlockSpec((tm, hq), row),
            pl.BlockSpec((tm, hq), row),
        ],
        out_shape=[
            jax.ShapeDtypeStruct((t, n_z), BF),
            jax.ShapeDtypeStruct((t, n_xbc), F32),
            jax.ShapeDtypeStruct((t, LANES), F32),
            jax.ShapeDtypeStruct((t, hq), BF),
            jax.ShapeDtypeStruct((t, hq), BF),
            jax.ShapeDtypeStruct((t, hq), BF),
        ],
        compiler_params=_params("parallel"),
        name="mix_inproj",
    )(h, norm_w, w_all, q_norm, kv_norm, wqa, wqb, wkn, wv, cos_t, sin_t)


def _conv_body(x_ref, w_ref, b_ref, o_ref):
    x = x_ref[...]
    s = x.shape[0]
    half = SSD_CONV // 2
    row = lax.broadcasted_iota(jnp.int32, x.shape, 0)
    acc = x * w_ref[half:half + 1, :] + b_ref[...]
    for k in range(SSD_CONV):
        d = k - half
        if d == 0:
            continue
        shifted = pltpu.roll(x, (-d) % s, axis=0)
        valid = (row < s - d) if d > 0 else (row >= -d)
        acc = acc + jnp.where(valid, shifted, 0.0) * w_ref[k:k + 1, :]
    o_ref[...] = jax.nn.silu(acc)


def _conv(xbc, conv_w, conv_b, e, batch):
    t, c = xbc.shape
    s = t // batch
    tc = 256
    return pl.pallas_call(
        _conv_body,
        grid=(batch, c // tc),
        in_specs=[
            pl.BlockSpec((s, tc), lambda b, j: (b, j)),
            pl.BlockSpec((None, SSD_CONV, tc), lambda b, j: (e, 0, j)),
            pl.BlockSpec((None, 1, tc), lambda b, j: (e, 0, j)),
        ],
        out_specs=pl.BlockSpec((s, tc), lambda b, j: (b, j)),
        out_shape=jax.ShapeDtypeStruct((t, c), F32),
        compiler_params=_params("parallel", "parallel"),
        name="ssd_conv",
    )(xbc, conv_w, conv_b)


def _ssd_decays(dt_ref, bias_ref, a_ref):
    q = dt_ref.shape[0]
    dt = jax.nn.softplus(dt_ref[...] + bias_ref[...])
    la = dt * a_ref[...]
    row = lax.broadcasted_iota(jnp.int32, (q, q), 0)
    col = lax.broadcasted_iota(jnp.int32, (q, q), 1)
    tril = row >= col
    cum = jnp.dot(tril.astype(F32), la, preferred_element_type=F32,
                  precision=lax.Precision.HIGHEST)
    tot = cum[q - 1:q, :]
    rev = tot - cum + la
    return dt, cum, rev, tot, tril, row <= col


def _pair_select(lo, v, h0, h1):
    return jnp.where(lo, v[:, h0:h0 + 1], v[:, h1:h1 + 1])


def _ssd_fwd_body(x_ref, b_ref, c_ref, dt_ref, bias_ref, a_ref, d_ref, y_ref, st_ref):
    @pl.when(pl.program_id(1) == 0)
    def _():
        st_ref[...] = jnp.zeros(st_ref.shape, F32)

    q = x_ref.shape[0]
    nh, hpg, n = SSD_HEADS, SSD_HEADS // SSD_GROUPS, SSD_STATE
    pw = 2 * SSD_HEAD_DIM
    dt, cum, rev, tot, tril, triu = _ssd_decays(dt_ref, bias_ref, a_ref)
    cum_t, rev_t, dt_t = cum.T, rev.T, dt.T
    ecum = jnp.exp(cum)
    wf = jnp.exp(tot - cum) * dt
    etot = jnp.exp(tot)
    lo = lax.broadcasted_iota(jnp.int32, (q, pw), 1) < SSD_HEAD_DIM
    lo1 = lo[:1]
    neg = -jnp.inf
    for g in range(SSD_GROUPS):
        bg = b_ref[:, g * n:(g + 1) * n].astype(BF)
        cg = c_ref[:, g * n:(g + 1) * n].astype(BF)
        cb = lax.dot_general(cg, bg, NT_DIMS, preferred_element_type=F32)
        yoff = _dot(cg, st_ref[g].astype(BF))
        xw, dec = [], []
        for j in range(hpg // 2):
            h0 = g * hpg + 2 * j
            h1 = h0 + 1
            psl = slice(h0 * SSD_HEAD_DIM, h0 * SSD_HEAD_DIM + pw)
            xp = x_ref[:, psl]
            yp = xp * d_ref[:, psl]
            for h, keep in ((h0, lo), (h1, jnp.logical_not(lo))):
                hb = nh + h
                df = jnp.exp(jnp.where(tril, cum[:, h:h + 1] - cum_t[h:h + 1, :], neg)) * dt_t[h:h + 1, :]
                db = jnp.exp(jnp.where(triu, rev[:, hb:hb + 1] - rev_t[hb:hb + 1, :], neg)) * dt_t[hb:hb + 1, :]
                m = (cb * (df + db)).astype(BF)
                yp = yp + _dot(m, jnp.where(keep, xp, 0.0).astype(BF))
            yp = yp + yoff[:, j * pw:(j + 1) * pw] * _pair_select(lo, ecum, h0, h1)
            y_ref[:, psl] = yp
            xw.append((xp * _pair_select(lo, wf, h0, h1)).astype(BF))
            dec.append(_pair_select(lo1, etot, h0, h1))
        upd = lax.dot_general(bg, jnp.concatenate(xw, axis=-1), TN_DIMS, preferred_element_type=F32)
        st_ref[g] = st_ref[g] * jnp.concatenate(dec, axis=-1) + upd


def _ssd_bwd_body(x_ref, b_ref, c_ref, dt_ref, bias_ref, a_ref, y1_ref, z_ref, nw_ref, o_ref, st_ref):
    @pl.when(pl.program_id(1) == 0)
    def _():
        st_ref[...] = jnp.zeros(st_ref.shape, F32)

    q = x_ref.shape[0]
    nh, hpg, n = SSD_HEADS, SSD_HEADS // SSD_GROUPS, SSD_STATE
    pw = 2 * SSD_HEAD_DIM
    dt, cum, rev, tot, _, _ = _ssd_decays(dt_ref, bias_ref, a_ref)
    erev = jnp.exp(rev)
    wb = jnp.exp(tot - rev) * dt
    etot = jnp.exp(tot)
    lo = lax.broadcasted_iota(jnp.int32, (q, pw), 1) < SSD_HEAD_DIM
    lo1 = lo[:1]
    ys = []
    for g in range(SSD_GROUPS):
        bg = b_ref[:, g * n:(g + 1) * n].astype(BF)
        cg = c_ref[:, g * n:(g + 1) * n].astype(BF)
        yoff = _dot(cg, st_ref[g].astype(BF))
        xw, dec = [], []
        for j in range(hpg // 2):
            h0 = nh + g * hpg + 2 * j
            h1 = h0 + 1
            psl = slice((h0 - nh) * SSD_HEAD_DIM, (h0 - nh) * SSD_HEAD_DIM + pw)
            xp = x_ref[:, psl]
            ys.append(y1_ref[:, psl] + yoff[:, j * pw:(j + 1) * pw] * _pair_select(lo, erev, h0, h1))
            xw.append((xp * _pair_select(lo, wb, h0, h1)).astype(BF))
            dec.append(_pair_select(lo1, etot, h0, h1))
        upd = lax.dot_general(bg, jnp.concatenate(xw, axis=-1), TN_DIMS, preferred_element_type=F32)
        st_ref[g] = st_ref[g] * jnp.concatenate(dec, axis=-1) + upd
    y = jnp.concatenate(ys, axis=-1)
    gated = y * jax.nn.silu(z_ref[...].astype(F32))
    o_ref[...] = _rms(gated, nw_ref[...]).astype(BF)


def _ssd(xbc, dt_raw, z, dt_bias, a_neg, d_exp, ssd_norm, e, batch):
    t = xbc.shape[0]
    inner = SSD_HEADS * SSD_HEAD_DIM
    gn = SSD_GROUPS * SSD_STATE
    q = SSD_CHUNK
    nc = t // batch // q
    b_blk = inner // gn
    st_shape = (SSD_GROUPS, SSD_STATE, inner // SSD_GROUPS)
    fwd = lambda b, c: (b * nc + c, 0)
    par = lambda b, c: (e, 0, 0)
    y1 = pl.pallas_call(
        _ssd_fwd_body,
        grid=(batch, nc),
        in_specs=[
            pl.BlockSpec((q, inner), fwd),
            pl.BlockSpec((q, gn), lambda b, c: (b * nc + c, b_blk)),
            pl.BlockSpec((q, gn), lambda b, c: (b * nc + c, b_blk + 1)),
            pl.BlockSpec((q, LANES), fwd),
            pl.BlockSpec((None, 1, LANES), par),
            pl.BlockSpec((None, 1, LANES), par),
            pl.BlockSpec((None, 1, inner), par),
        ],
        out_specs=pl.BlockSpec((q, inner), fwd),
        out_shape=jax.ShapeDtypeStruct((t, inner), F32),
        scratch_shapes=[pltpu.VMEM(st_shape, F32)],
        compiler_params=_params("parallel", "arbitrary"),
        name="ssd_fwd",
    )(xbc, xbc, xbc, dt_raw, dt_bias, a_neg, d_exp)
    bwd = lambda b, c: (b * nc + nc - 1 - c, 0)
    return pl.pallas_call(
        _ssd_bwd_body,
        grid=(batch, nc),
        in_specs=[
            pl.BlockSpec((q, inner), bwd),
            pl.BlockSpec((q, gn), lambda b, c: (b * nc + nc - 1 - c, b_blk)),
            pl.BlockSpec((q, gn), lambda b, c: (b * nc + nc - 1 - c, b_blk + 1)),
            pl.BlockSpec((q, LANES), bwd),
            pl.BlockSpec((None, 1, LANES), par),
            pl.BlockSpec((None, 1, LANES), par),
            pl.BlockSpec((q, inner), bwd),
            pl.BlockSpec((q, inner), bwd),
            pl.BlockSpec((None, 1, inner), par),
        ],
        out_specs=pl.BlockSpec((q, inner), bwd),
        out_shape=jax.ShapeDtypeStruct((t, inner), BF),
        scratch_shapes=[pltpu.VMEM(st_shape, F32)],
        compiler_params=_params("parallel", "arbitrary"),
        name="ssd_bwd",
    )(xbc, xbc, xbc, dt_raw, dt_bias, a_neg, y1, z, ssd_norm)


def _mla_body(q_ref, k_ref, v_ref, o_ref):
    acc = jnp.zeros(o_ref.shape, F32)
    for i in range(q_ref.shape[-1] // LANES):
        sl = slice(i * LANES, (i + 1) * LANES)
        s = lax.dot_general(q_ref[:, sl], k_ref[:, sl], NT_DIMS, preferred_element_type=F32)
        p = jnp.exp(s - jnp.max(s, axis=-1, keepdims=True))
        l = jnp.sum(p, axis=-1, keepdims=True)
        acc = acc + _dot(p.astype(BF), v_ref[:, sl]) / l
    o_ref[...] = acc.astype(BF)


def _mla(q, k, v, batch):
    t, hq = q.shape
    s = t // batch
    tq = _tile(s, 512)
    nq = s // tq
    pw = 2 * LANES
    q3, k3, v3 = (a.reshape(batch, s, hq) for a in (q, k, v))
    o = pl.pallas_call(
        _mla_body,
        grid=(batch, hq // pw, nq),
        in_specs=[
            pl.BlockSpec((None, tq, pw), lambda b, p, i: (b, i, p)),
            pl.BlockSpec((None, s, pw), lambda b, p, i: (b, 0, p)),
            pl.BlockSpec((None, s, pw), lambda b, p, i: (b, 0, p)),
        ],
        out_specs=pl.BlockSpec((None, tq, LANES), lambda b, p, i: (b, i, p)),
        out_shape=jax.ShapeDtypeStruct((batch, s, MLA_HEADS * MLA_V), BF),
        compiler_params=_params("parallel", "parallel", "parallel"),
        name="mla_attn",
    )(q3, k3, v3)
    return o.reshape(t, MLA_HEADS * MLA_V)


def _outproj_body(h_ref, y_ref, o_ref, wy_ref, wo_ref, out_ref):
    out_ref[...] = h_ref[...] + _dot(y_ref[...], wy_ref[...]) + _dot(o_ref[...], wo_ref[...])


def _outproj(h, y_ssd, o_mla, w_out, e):
    t, d = h.shape
    ny, no = y_ssd.shape[1], o_mla.shape[1]
    assert ny % no == 0
    tm = _tile(t, 512)
    row = lambda i: (i, 0)
    return pl.pallas_call(
        _outproj_body,
        grid=(t // tm,),
        in_specs=[
            pl.BlockSpec((tm, d), row),
            pl.BlockSpec((tm, ny), row),
            pl.BlockSpec((tm, no), row),
            _resident((None, ny, d), lambda i: (e, 0, 0)),
            _resident((None, no, d), lambda i: (e, ny // no, 0)),
        ],
        out_specs=pl.BlockSpec((tm, d), row),
        out_shape=jax.ShapeDtypeStruct((t, d), F32),
        compiler_params=_params("parallel"),
        name="mix_outproj",
    )(h, y_ssd, o_mla, w_out, w_out)


def _fnet_ch_body(h_ref, nw_ref, cc_ref, sc_ref, xc_ref, xs_ref):
    u = _rms(h_ref[...], nw_ref[...]).astype(BF)
    gc = cc_ref.shape[0]
    for g in range(u.shape[-1] // gc):
        sl = slice(g * gc, (g + 1) * gc)
        xc_ref[:, sl] = _dot(u[:, sl], cc_ref[...]).astype(BF)
        xs_ref[:, sl] = _dot(u[:, sl], sc_ref[...]).astype(BF)


def _fnet_seq_body(h_ref, cs_ref, ss_ref, xc_ref, xs_ref, w_ref, o_ref, *, scale):
    tm = h_ref.shape[0]
    rows = pl.ds(pl.multiple_of(pl.program_id(1) * tm, tm), tm)
    y = _dot(cs_ref[rows, :], xc_ref[...]) - _dot(ss_ref[rows, :], xs_ref[...])
    o_ref[...] = h_ref[...] + _dot((y * scale).astype(BF), w_ref[...])


def _dft_tables(n):
    i = jnp.arange(n, dtype=jnp.int32)
    ang = ((i[:, None] * i[None, :]) % n).astype(F32) * (2.0 * math.pi / n)
    return jnp.cos(ang).astype(BF), jnp.sin(ang).astype(BF)


def _fnet(h, norm_w, w_out, layer, o, batch, tables):
    t, d = h.shape
    s = t // batch
    cc, sc, cs, ss = tables
    gc = cc.shape[0]
    tm = _tile(t, 512)
    row = lambda i: (i, 0)
    xc, xs = pl.pallas_call(
        _fnet_ch_body,
        grid=(t // tm,),
        in_specs=[
            pl.BlockSpec((tm, d), row),
            pl.BlockSpec((None, 1, d), lambda i: (layer, 0, 0)),
            _resident((gc, gc), lambda i: (0, 0)),
            _resident((gc, gc), lambda i: (0, 0)),
        ],
        out_specs=[pl.BlockSpec((tm, d), row), pl.BlockSpec((tm, d), row)],
        out_shape=[jax.ShapeDtypeStruct((t, d), BF), jax.ShapeDtypeStruct((t, d), BF)],
        compiler_params=_params("parallel"),
        name="fnet_channel_dft",
    )(h, norm_w, cc, sc)
    ts = _tile(s, 256)
    ns = s // ts
    return pl.pallas_call(
        functools.partial(_fnet_seq_body, scale=(s * gc) ** -0.5),
        grid=(batch, ns),
        in_specs=[
            pl.BlockSpec((ts, d), lambda b, i: (b * ns + i, 0)),
            _resident((s, s), lambda b, i: (0, 0)),
            _resident((s, s), lambda b, i: (0, 0)),
            pl.BlockSpec((s, d), lambda b, i: (b, 0)),
            pl.BlockSpec((s, d), lambda b, i: (b, 0)),
            _resident((None, d, d), lambda b, i: (o, 0, 0)),
        ],
        out_specs=pl.BlockSpec((ts, d), lambda b, i: (b * ns + i, 0)),
        out_shape=jax.ShapeDtypeStruct((t, d), F32),
        compiler_params=_params("parallel", "arbitrary"),
        name="fnet_seq_dft",
    )(h, cs, ss, xc, xs, w_out)


def _mixer_weights(w_in, w_uq, w_ukv, dt_bias, a_log, ssd_d):
    ne, d, _ = w_in.shape
    inner = SSD_HEADS * SSD_HEAD_DIM
    conv_ch = inner + 2 * SSD_GROUPS * SSD_STATE
    o_z, o_xbc = 0, inner
    o_dt = o_xbc + conv_ch
    o_cq = o_dt + 2 * SSD_HEADS
    o_ckv = o_cq + MLA_Q_RANK
    o_kr = o_ckv + MLA_KV_RANK
    half = MLA_ROPE // 2
    pad = LANES - MLA_NOPE - MLA_ROPE
    zeros = lambda *s: jnp.zeros(s, w_in.dtype)
    w_dt = jnp.concatenate([w_in[:, :, o_dt:o_cq], zeros(ne, d, LANES - 2 * SSD_HEADS)], axis=-1)
    kr1 = w_in[:, :, o_kr:o_kr + half]
    kr2 = w_in[:, :, o_kr + half:o_kr + MLA_ROPE]
    kr_a = jnp.concatenate([zeros(ne, d, MLA_NOPE), kr1, kr2, zeros(ne, d, pad)], axis=-1)
    kr_b = jnp.concatenate([zeros(ne, d, MLA_NOPE), -kr2, kr1, zeros(ne, d, pad)], axis=-1)
    pieces = [w_in[:, :, o_z:o_xbc], w_in[:, :, o_xbc:o_dt], w_dt, w_in[:, :, o_cq:o_ckv],
              w_in[:, :, o_ckv:o_kr], jnp.concatenate([kr_a, kr_b], axis=-1)]
    cols, c = [], 0
    for p in pieces:
        cols.append((c, c + p.shape[-1]))
        c += p.shape[-1]
    w_all = jnp.concatenate(pieces, axis=-1).astype(BF)

    uq = w_uq.reshape(ne, MLA_Q_RANK, MLA_HEADS, MLA_NOPE + MLA_ROPE)
    q_nope, q1, q2 = uq[..., :MLA_NOPE], uq[..., MLA_NOPE:MLA_NOPE + half], uq[..., MLA_NOPE + half:]
    zq = lambda n: jnp.zeros((ne, MLA_Q_RANK, MLA_HEADS, n), w_uq.dtype)
    hq = MLA_HEADS * LANES
    wqa = jnp.concatenate([q_nope, q1, q2, zq(pad)], axis=-1).reshape(ne, MLA_Q_RANK, hq).astype(BF)
    wqb = jnp.concatenate([zq(MLA_NOPE), -q2, q1, zq(pad)], axis=-1).reshape(ne, MLA_Q_RANK, hq).astype(BF)

    ukv = w_ukv.reshape(ne, MLA_KV_RANK, MLA_HEADS, MLA_NOPE + MLA_V)
    zkv = lambda *s: jnp.zeros((ne, MLA_KV_RANK) + s, w_ukv.dtype)
    wkn = jnp.concatenate([ukv[..., :MLA_NOPE], zkv(MLA_HEADS, LANES - MLA_NOPE)], axis=-1)
    wkn = wkn.reshape(ne, MLA_KV_RANK, hq).astype(BF)
    vv = ukv[..., MLA_NOPE:].reshape(ne, MLA_KV_RANK, MLA_HEADS // 2, 2, MLA_V)
    zv = zkv(MLA_HEADS // 2, MLA_V)
    wv = jnp.stack([jnp.concatenate([vv[:, :, :, 0], zv], axis=-1),
                    jnp.concatenate([zv, vv[:, :, :, 1]], axis=-1)], axis=3)
    wv = wv.reshape(ne, MLA_KV_RANK, hq).astype(BF)

    padl = lambda a: jnp.concatenate([a, jnp.zeros((ne, LANES - a.shape[-1]), a.dtype)], axis=-1)[:, None, :]
    bias = padl(dt_bias.reshape(ne, 2 * SSD_HEADS))
    a_neg = padl(-jnp.exp(a_log.reshape(ne, 2 * SSD_HEADS)))
    d_exp = jnp.repeat(ssd_d, SSD_HEAD_DIM, axis=-1)[:, None, :]
    return w_all, tuple(cols), wqa, wqb, wkn, wv, bias, a_neg, d_exp


def _rope_tables(positions):
    half = MLA_ROPE // 2
    inv = 1.0 / (ROPE_THETA ** (jnp.arange(0, MLA_ROPE, 2, dtype=F32) / MLA_ROPE))
    ang = positions.astype(F32).reshape(-1, 1) * inv
    cos, sin = jnp.cos(ang), jnp.sin(ang)
    t = cos.shape[0]
    pad = LANES - MLA_NOPE - MLA_ROPE
    cos_t = jnp.concatenate([jnp.ones((t, MLA_NOPE), F32), cos, cos, jnp.zeros((t, pad), F32)], axis=-1)
    sin_t = jnp.concatenate([jnp.zeros((t, MLA_NOPE), F32), sin, sin, jnp.zeros((t, pad), F32)], axis=-1)
    return cos_t, sin_t


def kernel(x, mem, positions, mem_norm, final_norm, ffn1_norm, ffn1_w_gu, ffn1_w_down, mix_norm, xa_norm,
           xa_wq, xa_wkv, xa_wo, ffn2_norm, ffn2_w_gu, ffn2_w_down, w_in, conv_w, conv_b, dt_bias, a_log,
           ssd_d, ssd_norm, q_norm, w_uq, kv_norm, w_ukv, w_out, fnet_w_out):
    batch, seq, d = x.shape
    depth = ffn1_norm.shape[0]
    t = batch * seq
    bf = lambda a: a.astype(BF)
    row3 = lambda a: a[:, None, :]

    kv = _kvproj(mem.reshape(-1, d), mem_norm[None, :], bf(xa_wkv)).reshape(depth, batch, mem.shape[1], 2 * d)
    w_all, cols, wqa, wqb, wkn, wv, bias, a_neg, d_exp = _mixer_weights(w_in, w_uq, w_ukv, dt_bias, a_log, ssd_d)
    cos_t, sin_t = _rope_tables(positions)
    gc = d // FNET_GROUPS
    tables = _dft_tables(gc) + _dft_tables(seq)
    f1n, f2n, mxn, xan = row3(ffn1_norm), row3(ffn2_norm), row3(mix_norm), row3(xa_norm)
    f1gu, f1d, f2gu, f2d = bf(ffn1_w_gu), bf(ffn1_w_down), bf(ffn2_w_gu), bf(ffn2_w_down)
    wq, wo, w_mix_out, w_fnet = bf(xa_wq), bf(xa_wo), bf(w_out), bf(fnet_w_out)
    ssd_nw, qn, kvn = row3(ssd_norm), row3(q_norm), row3(kv_norm)

    h = x.reshape(t, d)
    for layer in range(depth):
        h = _ffn(h, f1n, f1gu, f1d, layer)
        if layer % 2 == 0:
            e = layer // 2
            z, xbc, dt_raw, q, k, v = _inproj(h, mxn, w_all, cols, qn, kvn, wqa, wqb, wkn, wv,
                                              cos_t, sin_t, layer, e)
            xbc = _conv(xbc, conv_w, row3(conv_b), e, batch)
            y_ssd = _ssd(xbc, dt_raw, z, bias, a_neg, d_exp, ssd_nw, e, batch)
            o_mla = _mla(q, k, v, batch)
            h = _outproj(h, y_ssd, o_mla, w_mix_out, e)
        else:
            h = _fnet(h, mxn, w_fnet, layer, layer // 2, batch, tables)
        h = _xattn(h, xan, wq, kv, wo, layer, batch)
        h = _ffn(h, f2n, f2gu, f2d, layer, final_w=final_norm[None, :] if layer == depth - 1 else None)
    return h.reshape(batch, seq, d)
```

```python
import functools
import math

import numpy as np
import jax
import jax.numpy as jnp
from jax import lax
from jax.experimental import pallas as pl
from jax.experimental.pallas import tpu as pltpu

EPS = 1e-6
BF = jnp.bfloat16
F32 = jnp.float32

V7X_VMEM_BYTES = 64 * 1024 * 1024
VMEM_LIMIT = V7X_VMEM_BYTES - 8 * 1024 * 1024
LANES = 128

SSD_HEADS = 16
SSD_HEAD_DIM = 64
SSD_GROUPS = 2
SSD_STATE = 128
SSD_CONV = 5
SSD_CHUNK = 128
MLA_HEADS = 8
MLA_Q_RANK = 512
MLA_KV_RANK = 256
MLA_NOPE = 64
MLA_ROPE = 32
MLA_V = 64
ROPE_THETA = 10000.0
FNET_GROUPS = 4
XA_HEADS = 4

NT_DIMS = (((1,), (1,)), ((), ()))
TN_DIMS = (((0,), (0,)), ((), ()))


def _params(*sem):
    return pltpu.CompilerParams(dimension_semantics=sem, vmem_limit_bytes=VMEM_LIMIT)


def _resident(shape, index_map):
    return pl.BlockSpec(shape, index_map, pipeline_mode=pl.Buffered(1))


def _rms(x, w):
    return x * lax.rsqrt(jnp.mean(x * x, axis=-1, keepdims=True) + EPS) * w


def _dot(a, b):
    return jnp.dot(a, b, preferred_element_type=F32)


def _tile(n, pref):
    t = min(n, pref)
    assert n % t == 0, (n, t)
    return t


def _ffn_body(*refs, chunks, final):
    if final:
        h_ref, nw_ref, wg_ref, wu_ref, wd_ref, fw_ref, o_ref = refs
    else:
        h_ref, nw_ref, wg_ref, wu_ref, wd_ref, o_ref = refs
    h = h_ref[...]
    xn = _rms(h, nw_ref[...]).astype(BF)
    acc = jnp.zeros(h.shape, F32)
    for a, b in chunks:
        g = _dot(xn, wg_ref[:, a:b])
        u = _dot(xn, wu_ref[:, a:b])
        act = (jax.nn.silu(g) * u).astype(BF)
        acc = acc + _dot(act, wd_ref[a:b, :])
    out = h + 0.5 * acc
    if final:
        out = _rms(out, fw_ref[...])
    o_ref[...] = out


def _ffn(h, norm_w, w_gu, w_down, layer, final_w=None):
    t, d = h.shape
    f = w_down.shape[1]
    tm = _tile(t, 512)
    step = 768
    chunks = tuple((a, min(a + step, f)) for a in range(0, f, step))
    in_specs = [
        pl.BlockSpec((tm, d), lambda i: (i, 0)),
        pl.BlockSpec((None, 1, d), lambda i: (layer, 0, 0)),
        _resident((None, d, f), lambda i: (layer, 0, 0)),
        _resident((None, d, f), lambda i: (layer, 0, 1)),
        _resident((None, f, d), lambda i: (layer, 0, 0)),
    ]
    args = [h, norm_w, w_gu, w_gu, w_down]
    if final_w is not None:
        in_specs.append(pl.BlockSpec((1, d), lambda i: (0, 0)))
        args.append(final_w)
    return pl.pallas_call(
        functools.partial(_ffn_body, chunks=chunks, final=final_w is not None),
        grid=(t // tm,),
        in_specs=in_specs,
        out_specs=pl.BlockSpec((tm, d), lambda i: (i, 0)),
        out_shape=jax.ShapeDtypeStruct((t, d), F32),
        compiler_params=_params("parallel"),
        name="ffn",
    )(*args)


def _kvproj_body(m_ref, nw_ref, w_ref, o_ref):
    mn = _rms(m_ref[...], nw_ref[...]).astype(BF)
    o_ref[...] = _dot(mn, w_ref[...]).astype(BF)


def _kvproj(mem2d, mem_norm, wkv):
    n, d = mem2d.shape
    nl, _, d2 = wkv.shape
    tm = _tile(n, 512)
    return pl.pallas_call(
        _kvproj_body,
        grid=(nl, n // tm),
        in_specs=[
            pl.BlockSpec((tm, d), lambda l, i: (i, 0)),
            pl.BlockSpec((1, d), lambda l, i: (0, 0)),
            pl.BlockSpec((None, d, d2), lambda l, i: (l, 0, 0)),
        ],
        out_specs=pl.BlockSpec((None, tm, d2), lambda l, i: (l, i, 0)),
        out_shape=jax.ShapeDtypeStruct((nl, n, d2), BF),
        compiler_params=_params("parallel", "parallel"),
        name="xa_kvproj",
    )(mem2d, mem_norm, wkv)


def _xa_body(h_ref, nw_ref, wq_ref, k_ref, v_ref, wo_ref, o_ref, *, heads):
    h = h_ref[...]
    hn = _rms(h, nw_ref[...]).astype(BF)
    dh = h.shape[-1] // heads
    q = (_dot(hn, wq_ref[...]) * (dh ** -0.5)).astype(BF)
    outs = []
    for i in range(heads):
        sl = slice(i * dh, (i + 1) * dh)
        s = lax.dot_general(q[:, sl], k_ref[:, sl], NT_DIMS, preferred_element_type=F32)
        p = jnp.exp(s - jnp.max(s, axis=-1, keepdims=True))
        l = jnp.sum(p, axis=-1, keepdims=True)
        outs.append((_dot(p.astype(BF), v_ref[:, sl]) / l).astype(BF))
    o = jnp.concatenate(outs, axis=-1)
    o_ref[...] = h + _dot(o, wo_ref[...])


def _xattn(h, norm_w, wq, kv, wo, layer, batch):
    t, d = h.shape
    s = t // batch
    nm = kv.shape[2]
    tm = _tile(s, 512)
    ns = s // tm
    return pl.pallas_call(
        functools.partial(_xa_body, heads=XA_HEADS),
        grid=(batch, ns),
        in_specs=[
            pl.BlockSpec((tm, d), lambda b, i: (b * ns + i, 0)),
            pl.BlockSpec((None, 1, d), lambda b, i: (layer, 0, 0)),
            _resident((None, d, d), lambda b, i: (layer, 0, 0)),
            pl.BlockSpec((None, None, nm, d), lambda b, i: (layer, b, 0, 0)),
            pl.BlockSpec((None, None, nm, d), lambda b, i: (layer, b, 0, 1)),
            _resident((None, d, d), lambda b, i: (layer, 0, 0)),
        ],
        out_specs=pl.BlockSpec((tm, d), lambda b, i: (b * ns + i, 0)),
        out_shape=jax.ShapeDtypeStruct((t, d), F32),
        compiler_params=_params("parallel", "parallel"),
        name="xattn",
    )(h, norm_w, wq, kv, kv, wo)


def _inproj_body(h_ref, nw_ref, w_ref, qn_ref, kvn_ref, wqa_ref, wqb_ref, wkn_ref, wv_ref,
                 cos_ref, sin_ref, z_ref, xbc_ref, dt_ref, q_ref, k_ref, v_ref, *, cols, scale):
    c_z, c_xbc, c_dt, c_cq, c_ckv, c_kr = cols
    u = _rms(h_ref[...], nw_ref[...]).astype(BF)
    z_ref[...] = _dot(u, w_ref[:, c_z[0]:c_z[1]]).astype(BF)
    xbc_ref[...] = _dot(u, w_ref[:, c_xbc[0]:c_xbc[1]])
    dt_ref[...] = _dot(u, w_ref[:, c_dt[0]:c_dt[1]])
    cqn = _rms(_dot(u, w_ref[:, c_cq[0]:c_cq[1]]), qn_ref[...]).astype(BF)
    ckvn = _rms(_dot(u, w_ref[:, c_ckv[0]:c_ckv[1]]), kvn_ref[...]).astype(BF)
    kr = _dot(u, w_ref[:, c_kr[0]:c_kr[1]])
    cos_t = cos_ref[...]
    sin_t = sin_ref[...]
    kp = kr[:, :LANES] * cos_t + kr[:, LANES:] * sin_t
    qa = _dot(cqn, wqa_ref[...])
    qb = _dot(cqn, wqb_ref[...])
    kn = _dot(ckvn, wkn_ref[...])
    for i in range(qa.shape[-1] // LANES):
        sl = slice(i * LANES, (i + 1) * LANES)
        q_ref[:, sl] = ((qa[:, sl] * cos_t + qb[:, sl] * sin_t) * scale).astype(BF)
        k_ref[:, sl] = (kn[:, sl] + kp).astype(BF)
    col = lax.broadcasted_iota(jnp.int32, (1, v_ref.shape[-1]), 1) % (2 * LANES)
    ones = jnp.where((col == MLA_V) | (col == LANES), 1.0, 0.0)
    v_ref[...] = (_dot(ckvn, wv_ref[...]) + ones).astype(BF)


def _inproj(h, norm_w, w_all, cols, q_norm, kv_norm, wqa, wqb, wkn, wv, cos_t, sin_t, layer, e):
    t, d = h.shape
    tm = _tile(t, 512)
    wc = w_all.shape[-1]
    n_z = cols[0][1] - cols[0][0]
    n_xbc = cols[1][1] - cols[1][0]
    hq = wqa.shape[-1]
    row = lambda i: (i, 0)
    return pl.pallas_call(
        functools.partial(_inproj_body, cols=cols, scale=math.log2(math.e) * (MLA_NOPE + MLA_ROPE) ** -0.5),
        grid=(t // tm,),
        in_specs=[
            pl.BlockSpec((tm, d), row),
            pl.BlockSpec((None, 1, d), lambda i: (layer, 0, 0)),
            _resident((None, d, wc), lambda i: (e, 0, 0)),
            pl.BlockSpec((None, 1, MLA_Q_RANK), lambda i: (e, 0, 0)),
            pl.BlockSpec((None, 1, MLA_KV_RANK), lambda i: (e, 0, 0)),
            _resident((None, MLA_Q_RANK, hq), lambda i: (e, 0, 0)),
            _resident((None, MLA_Q_RANK, hq), lambda i: (e, 0, 0)),
            _resident((None, MLA_KV_RANK, hq), lambda i: (e, 0, 0)),
            _resident((None, MLA_KV_RANK, hq), lambda i: (e, 0, 0)),
            pl.BlockSpec((tm, LANES), row),
            pl.BlockSpec((tm, LANES), row),
        ],
        out_specs=[
            pl.BlockSpec((tm, n_z), row),
            pl.BlockSpec((tm, n_xbc), row),
            pl.BlockSpec((tm, LANES), row),
            pl.BlockSpec((tm, hq), row),
            pl.BlockSpec((tm, hq), row),
            pl.BlockSpec((tm, hq), row),
        ],
        out_shape=[
            jax.ShapeDtypeStruct((t, n_z), BF),
            jax.ShapeDtypeStruct((t, n_xbc), F32),
            jax.ShapeDtypeStruct((t, LANES), F32),
            jax.ShapeDtypeStruct((t, hq), BF),
            jax.ShapeDtypeStruct((t, hq), BF),
            jax.ShapeDtypeStruct((t, hq), BF),
        ],
        compiler_params=_params("parallel"),
        name="mix_inproj",
    )(h, norm_w, w_all, q_norm, kv_norm, wqa, wqb, wkn, wv, cos_t, sin_t)


def _conv_body(x_ref, w_ref, b_ref, o_ref):
    x = x_ref[...]
    s = x.shape[0]
    half = SSD_CONV // 2
    row = lax.broadcasted_iota(jnp.int32, x.shape, 0)
    acc = x * w_ref[half:half + 1, :] + b_ref[...]
    for k in range(SSD_CONV):
        d = k - half
        if d == 0:
            continue
        shifted = pltpu.roll(x, (-d) % s, axis=0)
        valid = (row < s - d) if d > 0 else (row >= -d)
        acc = acc + jnp.where(valid, shifted, 0.0) * w_ref[k:k + 1, :]
    o_ref[...] = jax.nn.silu(acc)


def _conv(xbc, conv_w, conv_b, e, batch):
    t, c = xbc.shape
    s = t // batch
    tc = 256
    return pl.pallas_call(
        _conv_body,
        grid=(batch, c // tc),
        in_specs=[
            pl.BlockSpec((s, tc), lambda b, j: (b, j)),
            pl.BlockSpec((None, SSD_CONV, tc), lambda b, j: (e, 0, j)),
            pl.BlockSpec((None, 1, tc), lambda b, j: (e, 0, j)),
        ],
        out_specs=pl.BlockSpec((s, tc), lambda b, j: (b, j)),
        out_shape=jax.ShapeDtypeStruct((t, c), F32),
        compiler_params=_params("parallel", "parallel"),
        name="ssd_conv",
    )(xbc, conv_w, conv_b)


def _ssd_decays(dt_ref, bias_ref, a_ref):
    q = dt_ref.shape[0]
    dt = jax.nn.softplus(dt_ref[...] + bias_ref[...])
    la = dt * a_ref[...]
    row = lax.broadcasted_iota(jnp.int32, (q, q), 0)
    col = lax.broadcasted_iota(jnp.int32, (q, q), 1)
    tril = row >= col
    cum = jnp.dot(tril.astype(F32), la, preferred_element_type=F32,
                  precision=lax.Precision.HIGHEST)
    tot = cum[q - 1:q, :]
    rev = tot - cum + la
    return dt, cum, rev, tot, tril, row <= col


def _pair_select(lo, v, h0, h1):
    return jnp.where(lo, v[:, h0:h0 + 1], v[:, h1:h1 + 1])


def _ssd_fwd_body(x_ref, b_ref, c_ref, dt_ref, bias_ref, a_ref, d_ref, y_ref, st_ref):
    @pl.when(pl.program_id(1) == 0)
    def _():
        st_ref[...] = jnp.zeros(st_ref.shape, F32)

    q = x_ref.shape[0]
    nh, hpg, n = SSD_HEADS, SSD_HEADS // SSD_GROUPS, SSD_STATE
    pw = 2 * SSD_HEAD_DIM
    dt, cum, rev, tot, tril, triu = _ssd_decays(dt_ref, bias_ref, a_ref)
    cum_t, rev_t, dt_t = cum.T, rev.T, dt.T
    ecum = jnp.exp(cum)
    wf = jnp.exp(tot - cum) * dt
    etot = jnp.exp(tot)
    lo = lax.broadcasted_iota(jnp.int32, (q, pw), 1) < SSD_HEAD_DIM
    lo1 = lo[:1]
    neg = -jnp.inf
    for g in range(SSD_GROUPS):
        bg = b_ref[:, g * n:(g + 1) * n].astype(BF)
        cg = c_ref[:, g * n:(g + 1) * n].astype(BF)
        cb = lax.dot_general(cg, bg, NT_DIMS, preferred_element_type=F32)
        yoff = _dot(cg, st_ref[g].astype(BF))
        xw, dec = [], []
        for j in range(hpg // 2):
            h0 = g * hpg + 2 * j
            h1 = h0 + 1
            psl = slice(h0 * SSD_HEAD_DIM, h0 * SSD_HEAD_DIM + pw)
            xp = x_ref[:, psl]
            yp = xp * d_ref[:, psl]
            for h, keep in ((h0, lo), (h1, jnp.logical_not(lo))):
                hb = nh + h
                df = jnp.exp(jnp.where(tril, cum[:, h:h + 1] - cum_t[h:h + 1, :], neg)) * dt_t[h:h + 1, :]
                db = jnp.exp(jnp.where(triu, rev[:, hb:hb + 1] - rev_t[hb:hb + 1, :], neg)) * dt_t[hb:hb + 1, :]
                m = (cb * (df + db)).astype(BF)
                yp = yp + _dot(m, jnp.where(keep, xp, 0.0).astype(BF))
            yp = yp + yoff[:, j * pw:(j + 1) * pw] * _pair_select(lo, ecum, h0, h1)
            y_ref[:, psl] = yp
            xw.append((xp * _pair_select(lo, wf, h0, h1)).astype(BF))
            dec.append(_pair_select(lo1, etot, h0, h1))
        upd = lax.dot_general(bg, jnp.concatenate(xw, axis=-1), TN_DIMS, preferred_element_type=F32)
        st_ref[g] = st_ref[g] * jnp.concatenate(dec, axis=-1) + upd


def _ssd_bwd_body(x_ref, b_ref, c_ref, dt_ref, bias_ref, a_ref, y1_ref, z_ref, nw_ref, o_ref, st_ref):
    @pl.when(pl.program_id(1) == 0)
    def _():
        st_ref[...] = jnp.zeros(st_ref.shape, F32)

    q = x_ref.shape[0]
    nh, hpg, n = SSD_HEADS, SSD_HEADS // SSD_GROUPS, SSD_STATE
    pw = 2 * SSD_HEAD_DIM
    dt, cum, rev, tot, _, _ = _ssd_decays(dt_ref, bias_ref, a_ref)
    erev = jnp.exp(rev)
    wb = jnp.exp(tot - rev) * dt
    etot = jnp.exp(tot)
    lo = lax.broadcasted_iota(jnp.int32, (q, pw), 1) < SSD_HEAD_DIM
    lo1 = lo[:1]
    ys = []
    for g in range(SSD_GROUPS):
        bg = b_ref[:, g * n:(g + 1) * n].astype(BF)
        cg = c_ref[:, g * n:(g + 1) * n].astype(BF)
        yoff = _dot(cg, st_ref[g].astype(BF))
        xw, dec = [], []
        for j in range(hpg // 2):
            h0 = nh + g * hpg + 2 * j
            h1 = h0 + 1
            psl = slice((h0 - nh) * SSD_HEAD_DIM, (h0 - nh) * SSD_HEAD_DIM + pw)
            xp = x_ref[:, psl]
            ys.append(y1_ref[:, psl] + yoff[:, j * pw:(j + 1) * pw] * _pair_select(lo, erev, h0, h1))
            xw.append((xp * _pair_select(lo, wb, h0, h1)).astype(BF))
            dec.append(_pair_select(lo1, etot, h0, h1))
        upd = lax.dot_general(bg, jnp.concatenate(xw, axis=-1), TN_DIMS, preferred_element_type=F32)
        st_ref[g] = st_ref[g] * jnp.concatenate(dec, axis=-1) + upd
    y = jnp.concatenate(ys, axis=-1)
    gated = y * jax.nn.silu(z_ref[...].astype(F32))
    o_ref[...] = _rms(gated, nw_ref[...]).astype(BF)


def _ssd(xbc, dt_raw, z, dt_bias, a_neg, d_exp, ssd_norm, e, batch):
    t = xbc.shape[0]
    inner = SSD_HEADS * SSD_HEAD_DIM
    gn = SSD_GROUPS * SSD_STATE
    q = SSD_CHUNK
    nc = t // batch // q
    b_blk = inner // gn
    st_shape = (SSD_GROUPS, SSD_STATE, inner // SSD_GROUPS)
    fwd = lambda b, c: (b * nc + c, 0)
    par = lambda b, c: (e, 0, 0)
    y1 = pl.pallas_call(
        _ssd_fwd_body,
        grid=(batch, nc),
        in_specs=[
            pl.BlockSpec((q, inner), fwd),
            pl.BlockSpec((q, gn), lambda b, c: (b * nc + c, b_blk)),
            pl.BlockSpec((q, gn), lambda b, c: (b * nc + c, b_blk + 1)),
            pl.BlockSpec((q, LANES), fwd),
            pl.BlockSpec((None, 1, LANES), par),
            pl.BlockSpec((None, 1, LANES), par),
            pl.BlockSpec((None, 1, inner), par),
        ],
        out_specs=pl.BlockSpec((q, inner), fwd),
        out_shape=jax.ShapeDtypeStruct((t, inner), F32),
        scratch_shapes=[pltpu.VMEM(st_shape, F32)],
        compiler_params=_params("parallel", "arbitrary"),
        name="ssd_fwd",
    )(xbc, xbc, xbc, dt_raw, dt_bias, a_neg, d_exp)
    bwd = lambda b, c: (b * nc + nc - 1 - c, 0)
    return pl.pallas_call(
        _ssd_bwd_body,
        grid=(batch, nc),
        in_specs=[
            pl.BlockSpec((q, inner), bwd),
            pl.BlockSpec((q, gn), lambda b, c: (b * nc + nc - 1 - c, b_blk)),
            pl.BlockSpec((q, gn), lambda b, c: (b * nc + nc - 1 - c, b_blk + 1)),
            pl.BlockSpec((q, LANES), bwd),
            pl.BlockSpec((None, 1, LANES), par),
            pl.BlockSpec((None, 1, LANES), par),
            pl.BlockSpec((q, inner), bwd),
            pl.BlockSpec((q, inner), bwd),
            pl.BlockSpec((None, 1, inner), par),
        ],
        out_specs=pl.BlockSpec((q, inner), bwd),
        out_shape=jax.ShapeDtypeStruct((t, inner), BF),
        scratch_shapes=[pltpu.VMEM(st_shape, F32)],
        compiler_params=_params("parallel", "arbitrary"),
        name="ssd_bwd",
    )(xbc, xbc, xbc, dt_raw, dt_bias, a_neg, y1, z, ssd_norm)


def _lane_fold(x, op):
    out = x[:, :LANES]
    for j in range(1, x.shape[-1] // LANES):
        out = op(out, x[:, j * LANES:(j + 1) * LANES])
    return out


def _mla_body(q_ref, k_ref, v_ref, o_ref, s_ref, m_ref, *, kt):
    nk = k_ref.shape[0]
    _, rb, _ = s_ref.shape
    nrb = q_ref.shape[0] // rb

    def score_pass(r, head):
        rows = pl.ds(pl.multiple_of(r * rb, rb), rb)
        sl = slice(head * LANES, (head + 1) * LANES)
        q = q_ref[rows, sl]
        m_acc = None
        for c in range(0, nk, kt):
            s = lax.dot_general(q, k_ref[c:c + kt, sl], NT_DIMS, preferred_element_type=F32)
            s_ref[head, :, c:c + kt] = s
            mc = _lane_fold(s, jnp.maximum)
            m_acc = mc if m_acc is None else jnp.maximum(m_acc, mc)
        m_ref[head] = jnp.max(m_acc, axis=-1, keepdims=True)

    def value_pass(head):
        sl = slice(head * LANES, (head + 1) * LANES)
        m = m_ref[head]
        o = jnp.zeros((rb, LANES), F32)
        for c in range(0, nk, kt):
            p = jnp.exp2(s_ref[head, :, c:c + kt] - m)
            o = o + _dot(p.astype(BF), v_ref[c:c + kt, sl])
        return o

    def store(r, o0, o1):
        lo = lax.broadcasted_iota(jnp.int32, (rb, LANES), 1) < MLA_V
        out = jnp.where(lo, o0 / o0[:, MLA_V:MLA_V + 1], o1 / o1[:, 0:1])
        o_ref[pl.ds(pl.multiple_of(r * rb, rb), rb), :] = out.astype(BF)

    def row_block(r, carry):
        score_pass(r, 1)
        o0 = value_pass(0)
        score_pass(r + 1, 0)
        o1 = value_pass(1)
        store(r, o0, o1)
        return carry

    score_pass(0, 0)
    lax.fori_loop(0, nrb - 1, row_block, 0)
    score_pass(nrb - 1, 1)
    o0 = value_pass(0)
    o1 = value_pass(1)
    store(nrb - 1, o0, o1)


def _mla(q, k, v, batch):
    t, hq = q.shape
    s = t // batch
    pw = 2 * LANES
    rb = _tile(s, 256)
    q3, k3, v3 = (a.reshape(batch, s, hq) for a in (q, k, v))
    blk = pl.BlockSpec((None, s, pw), lambda b, p: (b, 0, p))
    o = pl.pallas_call(
        functools.partial(_mla_body, kt=_tile(s, 256)),
        grid=(batch, hq // pw),
        in_specs=[blk, blk, blk],
        out_specs=pl.BlockSpec((None, s, LANES), lambda b, p: (b, 0, p)),
        out_shape=jax.ShapeDtypeStruct((batch, s, MLA_HEADS * MLA_V), BF),
        scratch_shapes=[pltpu.VMEM((pw // LANES, rb, s), F32), pltpu.VMEM((pw // LANES, rb, 1), F32)],
        compiler_params=_params("parallel", "parallel"),
        name="mla_attn",
    )(q3, k3, v3)
    return o.reshape(t, MLA_HEADS * MLA_V)


def _outproj_body(h_ref, y_ref, o_ref, wy_ref, wo_ref, out_ref):
    out_ref[...] = h_ref[...] + _dot(y_ref[...], wy_ref[...]) + _dot(o_ref[...], wo_ref[...])


def _outproj(h, y_ssd, o_mla, w_out, e):
    t, d = h.shape
    ny, no = y_ssd.shape[1], o_mla.shape[1]
    assert ny % no == 0
    tm = _tile(t, 512)
    row = lambda i: (i, 0)
    return pl.pallas_call(
        _outproj_body,
        grid=(t // tm,),
        in_specs=[
            pl.BlockSpec((tm, d), row),
            pl.BlockSpec((tm, ny), row),
            pl.BlockSpec((tm, no), row),
            _resident((None, ny, d), lambda i: (e, 0, 0)),
            _resident((None, no, d), lambda i: (e, ny // no, 0)),
        ],
        out_specs=pl.BlockSpec((tm, d), row),
        out_shape=jax.ShapeDtypeStruct((t, d), F32),
        compiler_params=_params("parallel"),
        name="mix_outproj",
    )(h, y_ssd, o_mla, w_out, w_out)


def _fnet_fold_body(ha_ref, hm_ref, hx_ref, nw_ref, cc_ref, sc_ref, ec_ref, es_ref):
    tm = ha_ref.shape[0]
    nw = nw_ref[...]
    u_a = _rms(ha_ref[...], nw)
    u_m = _rms(hm_ref[...], nw)
    u_x = _rms(hx_ref[...], nw)[0:1]
    r = lax.broadcasted_iota(jnp.int32, (tm, tm), 0)
    c = lax.broadcasted_iota(jnp.int32, (tm, tm), 1)
    perm = jnp.where(r + c == tm, 1.0, 0.0).astype(BF)
    hi = u_m.astype(BF)
    rest = u_m - hi.astype(F32)
    mid = rest.astype(BF)
    low = (rest - mid.astype(F32)).astype(BF)
    mirror = _dot(perm, hi) + _dot(perm, mid) + _dot(perm, low)
    first = lax.broadcasted_iota(jnp.int32, u_a.shape, 0) == 0
    mirror = jnp.where(first, u_x, mirror)
    ue = (u_a + mirror).astype(BF)
    uo = (u_a - mirror).astype(BF)
    gc = cc_ref.shape[0]
    for g in range(ue.shape[-1] // gc):
        sl = slice(g * gc, (g + 1) * gc)
        ec_ref[:, sl] = _dot(ue[:, sl], cc_ref[...]).astype(BF)
        es_ref[:, sl] = _dot(uo[:, sl], sc_ref[...]).astype(BF)


def _fnet_seq_body(h_ref, hh_ref, nw_ref, cc_ref, cs_ref, ss_ref, ec_ref, es_ref, w_ref, o_ref, *, scale):
    ts = h_ref.shape[0]
    rows = pl.ds(pl.multiple_of(pl.program_id(1) * ts, ts), ts)
    y = _dot(cs_ref[rows, :], ec_ref[...]) - _dot(ss_ref[rows, :], es_ref[...])
    u_h = _rms(hh_ref[...], nw_ref[...]).astype(BF)
    gc = cc_ref.shape[0]
    x_h = jnp.concatenate([_dot(u_h[:, g * gc:(g + 1) * gc], cc_ref[...])
                           for g in range(u_h.shape[-1] // gc)], axis=-1)[0:1]
    odd = lax.broadcasted_iota(jnp.int32, (ts, 1), 0) % 2 == 1
    y = y + jnp.where(odd, -1.0, 1.0) * x_h
    o_ref[...] = h_ref[...] + _dot((y * scale).astype(BF), w_ref[...])


def _dft_tables(n, fold):
    j = np.arange(n)[:, None]
    k = np.arange(n // 2 if fold else n)[None, :]
    ang = ((j * k) % n) * (2.0 * np.pi / n)
    cos, sin = np.cos(ang), np.sin(ang)
    if fold:
        cos[:, 0] = 0.5
    return jnp.asarray(cos, BF), jnp.asarray(sin, BF)


def _fnet(h, norm_w, w_out, layer, o, batch, tables):
    t, d = h.shape
    s = t // batch
    cc, sc, cs, ss = tables
    gc = cc.shape[0]
    sub = 8
    tm = _tile(s // 2, 256)
    nt, nf = s // tm, s // 2 // tm
    nwspec = pl.BlockSpec((None, 1, d), lambda b, i: (layer, 0, 0))
    table = _resident((gc, gc), lambda b, i: (0, 0))
    ec, es = pl.pallas_call(
        _fnet_fold_body,
        grid=(batch, nf),
        in_specs=[
            pl.BlockSpec((tm, d), lambda b, i: (b * nt + i, 0)),
            pl.BlockSpec((tm, d), lambda b, i: (b * nt + nt - 1 - i, 0)),
            pl.BlockSpec((sub, d), lambda b, i: (b * (s // sub) + ((nt - i) % nt) * (tm // sub), 0)),
            nwspec, table, table,
        ],
        out_specs=[pl.BlockSpec((tm, d), lambda b, i: (b * nf + i, 0))] * 2,
        out_shape=[jax.ShapeDtypeStruct((t // 2, d), BF)] * 2,
        compiler_params=_params("parallel", "parallel"),
        name="fnet_fold_channel_dft",
    )(h, h, h, norm_w, cc, sc)
    ts = _tile(s, 256)
    ns = s // ts
    return pl.pallas_call(
        functools.partial(_fnet_seq_body, scale=(s * gc) ** -0.5),
        grid=(batch, ns),
        in_specs=[
            pl.BlockSpec((ts, d), lambda b, i: (b * ns + i, 0)),
            pl.BlockSpec((sub, d), lambda b, i: (b * (s // sub) + s // 2 // sub, 0)),
            nwspec, table,
            _resident((s, s // 2), lambda b, i: (0, 0)),
            _resident((s, s // 2), lambda b, i: (0, 0)),
            pl.BlockSpec((s // 2, d), lambda b, i: (b, 0)),
            pl.BlockSpec((s // 2, d), lambda b, i: (b, 0)),
            _resident((None, d, d), lambda b, i: (o, 0, 0)),
        ],
        out_specs=pl.BlockSpec((ts, d), lambda b, i: (b * ns + i, 0)),
        out_shape=jax.ShapeDtypeStruct((t, d), F32),
        compiler_params=_params("parallel", "arbitrary"),
        name="fnet_seq_dft",
    )(h, h, norm_w, cc, cs, ss, ec, es, w_out)


def _mixer_weights(w_in, w_uq, w_ukv, dt_bias, a_log, ssd_d):
    ne, d, _ = w_in.shape
    inner = SSD_HEADS * SSD_HEAD_DIM
    conv_ch = inner + 2 * SSD_GROUPS * SSD_STATE
    o_z, o_xbc = 0, inner
    o_dt = o_xbc + conv_ch
    o_cq = o_dt + 2 * SSD_HEADS
    o_ckv = o_cq + MLA_Q_RANK
    o_kr = o_ckv + MLA_KV_RANK
    half = MLA_ROPE // 2
    pad = LANES - MLA_NOPE - MLA_ROPE
    zeros = lambda *s: jnp.zeros(s, w_in.dtype)
    w_dt = jnp.concatenate([w_in[:, :, o_dt:o_cq], zeros(ne, d, LANES - 2 * SSD_HEADS)], axis=-1)
    kr1 = w_in[:, :, o_kr:o_kr + half]
    kr2 = w_in[:, :, o_kr + half:o_kr + MLA_ROPE]
    kr_a = jnp.concatenate([zeros(ne, d, MLA_NOPE), kr1, kr2, zeros(ne, d, pad)], axis=-1)
    kr_b = jnp.concatenate([zeros(ne, d, MLA_NOPE), -kr2, kr1, zeros(ne, d, pad)], axis=-1)
    pieces = [w_in[:, :, o_z:o_xbc], w_in[:, :, o_xbc:o_dt], w_dt, w_in[:, :, o_cq:o_ckv],
              w_in[:, :, o_ckv:o_kr], jnp.concatenate([kr_a, kr_b], axis=-1)]
    cols, c = [], 0
    for p in pieces:
        cols.append((c, c + p.shape[-1]))
        c += p.shape[-1]
    w_all = jnp.concatenate(pieces, axis=-1).astype(BF)

    uq = w_uq.reshape(ne, MLA_Q_RANK, MLA_HEADS, MLA_NOPE + MLA_ROPE)
    q_nope, q1, q2 = uq[..., :MLA_NOPE], uq[..., MLA_NOPE:MLA_NOPE + half], uq[..., MLA_NOPE + half:]
    zq = lambda n: jnp.zeros((ne, MLA_Q_RANK, MLA_HEADS, n), w_uq.dtype)
    hq = MLA_HEADS * LANES
    wqa = jnp.concatenate([q_nope, q1, q2, zq(pad)], axis=-1).reshape(ne, MLA_Q_RANK, hq).astype(BF)
    wqb = jnp.concatenate([zq(MLA_NOPE), -q2, q1, zq(pad)], axis=-1).reshape(ne, MLA_Q_RANK, hq).astype(BF)

    ukv = w_ukv.reshape(ne, MLA_KV_RANK, MLA_HEADS, MLA_NOPE + MLA_V)
    zkv = lambda *s: jnp.zeros((ne, MLA_KV_RANK) + s, w_ukv.dtype)
    wkn = jnp.concatenate([ukv[..., :MLA_NOPE], zkv(MLA_HEADS, LANES - MLA_NOPE)], axis=-1)
    wkn = wkn.reshape(ne, MLA_KV_RANK, hq).astype(BF)
    vv = ukv[..., MLA_NOPE:].reshape(ne, MLA_KV_RANK, MLA_HEADS // 2, 2, MLA_V)
    zv = zkv(MLA_HEADS // 2, MLA_V)
    wv = jnp.stack([jnp.concatenate([vv[:, :, :, 0], zv], axis=-1),
                    jnp.concatenate([zv, vv[:, :, :, 1]], axis=-1)], axis=3)
    wv = wv.reshape(ne, MLA_KV_RANK, hq).astype(BF)

    padl = lambda a: jnp.concatenate([a, jnp.zeros((ne, LANES - a.shape[-1]), a.dtype)], axis=-1)[:, None, :]
    bias = padl(dt_bias.reshape(ne, 2 * SSD_HEADS))
    a_neg = padl(-jnp.exp(a_log.reshape(ne, 2 * SSD_HEADS)))
    d_exp = jnp.repeat(ssd_d, SSD_HEAD_DIM, axis=-1)[:, None, :]
    return w_all, tuple(cols), wqa, wqb, wkn, wv, bias, a_neg, d_exp


def _rope_tables(positions):
    half = MLA_ROPE // 2
    inv = 1.0 / (ROPE_THETA ** (jnp.arange(0, MLA_ROPE, 2, dtype=F32) / MLA_ROPE))
    ang = positions.astype(F32).reshape(-1, 1) * inv
    cos, sin = jnp.cos(ang), jnp.sin(ang)
    t = cos.shape[0]
    pad = LANES - MLA_NOPE - MLA_ROPE
    cos_t = jnp.concatenate([jnp.ones((t, MLA_NOPE), F32), cos, cos, jnp.zeros((t, pad), F32)], axis=-1)
    sin_t = jnp.concatenate([jnp.zeros((t, MLA_NOPE), F32), sin, sin, jnp.zeros((t, pad), F32)], axis=-1)
    return cos_t, sin_t


def kernel(x, mem, positions, mem_norm, final_norm, ffn1_norm, ffn1_w_gu, ffn1_w_down, mix_norm, xa_norm,
           xa_wq, xa_wkv, xa_wo, ffn2_norm, ffn2_w_gu, ffn2_w_down, w_in, conv_w, conv_b, dt_bias, a_log,
           ssd_d, ssd_norm, q_norm, w_uq, kv_norm, w_ukv, w_out, fnet_w_out):
    batch, seq, d = x.shape
    depth = ffn1_norm.shape[0]
    t = batch * seq
    bf = lambda a: a.astype(BF)
    row3 = lambda a: a[:, None, :]

    kv = _kvproj(mem.reshape(-1, d), mem_norm[None, :], bf(xa_wkv)).reshape(depth, batch, mem.shape[1], 2 * d)
    w_all, cols, wqa, wqb, wkn, wv, bias, a_neg, d_exp = _mixer_weights(w_in, w_uq, w_ukv, dt_bias, a_log, ssd_d)
    cos_t, sin_t = _rope_tables(positions)
    gc = d // FNET_GROUPS
    tables = _dft_tables(gc, fold=False) + _dft_tables(seq, fold=True)
    f1n, f2n, mxn, xan = row3(ffn1_norm), row3(ffn2_norm), row3(mix_norm), row3(xa_norm)
    f1gu, f1d, f2gu, f2d = bf(ffn1_w_gu), bf(ffn1_w_down), bf(ffn2_w_gu), bf(ffn2_w_down)
    wq, wo, w_mix_out, w_fnet = bf(xa_wq), bf(xa_wo), bf(w_out), bf(fnet_w_out)
    ssd_nw, qn, kvn = row3(ssd_norm), row3(q_norm), row3(kv_norm)

    h = x.reshape(t, d)
    for layer in range(depth):
        h = _ffn(h, f1n, f1gu, f1d, layer)
        if layer % 2 == 0:
            e = layer // 2
            z, xbc, dt_raw, q, k, v = _inproj(h, mxn, w_all, cols, qn, kvn, wqa, wqb, wkn, wv,
                                              cos_t, sin_t, layer, e)
            xbc = _conv(xbc, conv_w, row3(conv_b), e, batch)
            y_ssd = _ssd(xbc, dt_raw, z, bias, a_neg, d_exp, ssd_nw, e, batch)
            o_mla = _mla(q, k, v, batch)
            h = _outproj(h, y_ssd, o_mla, w_mix_out, e)
        else:
            h = _fnet(h, mxn, w_fnet, layer, layer // 2, batch, tables)
        h = _xattn(h, xan, wq, kv, wo, layer, batch)
        h = _ffn(h, f2n, f2gu, f2d, layer, final_w=final_norm[None, :] if layer == depth - 1 else None)
    return h.reshape(batch, seq, d)
```

```python
import functools
import math

import numpy as np
import jax
import jax.numpy as jnp
from jax import lax
from jax.experimental import pallas as pl
from jax.experimental.pallas import tpu as pltpu

EPS = 1e-6
BF = jnp.bfloat16
F32 = jnp.float32

V7X_VMEM_BYTES = 64 * 1024 * 1024
VMEM_LIMIT = V7X_VMEM_BYTES - 8 * 1024 * 1024
LANES = 128

SSD_HEADS = 16
SSD_HEAD_DIM = 64
SSD_GROUPS = 2
SSD_STATE = 128
SSD_CONV = 5
SSD_CHUNK = 128
MLA_HEADS = 8
MLA_Q_RANK = 512
MLA_KV_RANK = 256
MLA_NOPE = 64
MLA_ROPE = 32
MLA_V = 64
ROPE_THETA = 10000.0
FNET_GROUPS = 4
XA_HEADS = 4

NT_DIMS = (((1,), (1,)), ((), ()))
TN_DIMS = (((0,), (0,)), ((), ()))


def _params(*sem):
    return pltpu.CompilerParams(dimension_semantics=sem, vmem_limit_bytes=VMEM_LIMIT)


def _resident(shape, index_map):
    return pl.BlockSpec(shape, index_map, pipeline_mode=pl.Buffered(1))


def _rms(x, w):
    return x * lax.rsqrt(jnp.mean(x * x, axis=-1, keepdims=True) + EPS) * w


def _dot(a, b):
    return jnp.dot(a, b, preferred_element_type=F32)


def _tile(n, pref):
    t = min(n, pref)
    assert n % t == 0, (n, t)
    return t


def _ffn_body(*refs, chunks, final):
    if final:
        h_ref, nw_ref, wg_ref, wu_ref, wd_ref, fw_ref, o_ref = refs
    else:
        h_ref, nw_ref, wg_ref, wu_ref, wd_ref, o_ref = refs
    h = h_ref[...]
    xn = _rms(h, nw_ref[...]).astype(BF)
    acc = jnp.zeros(h.shape, F32)
    for a, b in chunks:
        g = _dot(xn, wg_ref[:, a:b])
        u = _dot(xn, wu_ref[:, a:b])
        act = (jax.nn.silu(g) * u).astype(BF)
        acc = acc + _dot(act, wd_ref[a:b, :])
    out = h + 0.5 * acc
    if final:
        out = _rms(out, fw_ref[...])
    o_ref[...] = out


def _ffn(h, norm_w, w_gu, w_down, layer, final_w=None):
    t, d = h.shape
    f = w_down.shape[1]
    tm = _tile(t, 512)
    step = 768
    chunks = tuple((a, min(a + step, f)) for a in range(0, f, step))
    in_specs = [
        pl.BlockSpec((tm, d), lambda i: (i, 0)),
        pl.BlockSpec((None, 1, d), lambda i: (layer, 0, 0)),
        _resident((None, d, f), lambda i: (layer, 0, 0)),
        _resident((None, d, f), lambda i: (layer, 0, 1)),
        _resident((None, f, d), lambda i: (layer, 0, 0)),
    ]
    args = [h, norm_w, w_gu, w_gu, w_down]
    if final_w is not None:
        in_specs.append(pl.BlockSpec((1, d), lambda i: (0, 0)))
        args.append(final_w)
    return pl.pallas_call(
        functools.partial(_ffn_body, chunks=chunks, final=final_w is not None),
        grid=(t // tm,),
        in_specs=in_specs,
        out_specs=pl.BlockSpec((tm, d), lambda i: (i, 0)),
        out_shape=jax.ShapeDtypeStruct((t, d), F32),
        compiler_params=_params("parallel"),
        name="ffn",
    )(*args)


def _kvproj_body(m_ref, nw_ref, w_ref, o_ref):
    mn = _rms(m_ref[...], nw_ref[...]).astype(BF)
    o_ref[...] = _dot(mn, w_ref[...]).astype(BF)


def _kvproj(mem2d, mem_norm, wkv):
    n, d = mem2d.shape
    nl, _, d2 = wkv.shape
    tm = _tile(n, 512)
    return pl.pallas_call(
        _kvproj_body,
        grid=(nl, n // tm),
        in_specs=[
            pl.BlockSpec((tm, d), lambda l, i: (i, 0)),
            pl.BlockSpec((1, d), lambda l, i: (0, 0)),
            pl.BlockSpec((None, d, d2), lambda l, i: (l, 0, 0)),
        ],
        out_specs=pl.BlockSpec((None, tm, d2), lambda l, i: (l, i, 0)),
        out_shape=jax.ShapeDtypeStruct((nl, n, d2), BF),
        compiler_params=_params("parallel", "parallel"),
        name="xa_kvproj",
    )(mem2d, mem_norm, wkv)


def _xa_body(h_ref, nw_ref, wq_ref, k_ref, v_ref, wo_ref, o_ref, *, heads):
    h = h_ref[...]
    hn = _rms(h, nw_ref[...]).astype(BF)
    dh = h.shape[-1] // heads
    q = (_dot(hn, wq_ref[...]) * (dh ** -0.5)).astype(BF)
    outs = []
    for i in range(heads):
        sl = slice(i * dh, (i + 1) * dh)
        s = lax.dot_general(q[:, sl], k_ref[:, sl], NT_DIMS, preferred_element_type=F32)
        p = jnp.exp(s - jnp.max(s, axis=-1, keepdims=True))
        l = jnp.sum(p, axis=-1, keepdims=True)
        outs.append((_dot(p.astype(BF), v_ref[:, sl]) / l).astype(BF))
    o = jnp.concatenate(outs, axis=-1)
    o_ref[...] = h + _dot(o, wo_ref[...])


def _xattn(h, norm_w, wq, kv, wo, layer, batch):
    t, d = h.shape
    s = t // batch
    nm = kv.shape[2]
    tm = _tile(s, 512)
    ns = s // tm
    return pl.pallas_call(
        functools.partial(_xa_body, heads=XA_HEADS),
        grid=(batch, ns),
        in_specs=[
            pl.BlockSpec((tm, d), lambda b, i: (b * ns + i, 0)),
            pl.BlockSpec((None, 1, d), lambda b, i: (layer, 0, 0)),
            _resident((None, d, d), lambda b, i: (layer, 0, 0)),
            pl.BlockSpec((None, None, nm, d), lambda b, i: (layer, b, 0, 0)),
            pl.BlockSpec((None, None, nm, d), lambda b, i: (layer, b, 0, 1)),
            _resident((None, d, d), lambda b, i: (layer, 0, 0)),
        ],
        out_specs=pl.BlockSpec((tm, d), lambda b, i: (b * ns + i, 0)),
        out_shape=jax.ShapeDtypeStruct((t, d), F32),
        compiler_params=_params("parallel", "parallel"),
        name="xattn",
    )(h, norm_w, wq, kv, kv, wo)


def _inproj_body(h_ref, nw_ref, w_ref, qn_ref, kvn_ref, wqa_ref, wqb_ref, wkn_ref, wv_ref,
                 cos_ref, sin_ref, z_ref, xbc_ref, dt_ref, q_ref, k_ref, v_ref, *, cols, scale):
    c_z, c_xbc, c_dt, c_cq, c_ckv, c_kr = cols
    u = _rms(h_ref[...], nw_ref[...]).astype(BF)
    z_ref[...] = _dot(u, w_ref[:, c_z[0]:c_z[1]]).astype(BF)
    xbc_ref[...] = _dot(u, w_ref[:, c_xbc[0]:c_xbc[1]])
    dt_ref[...] = _dot(u, w_ref[:, c_dt[0]:c_dt[1]])
    cqn = _rms(_dot(u, w_ref[:, c_cq[0]:c_cq[1]]), qn_ref[...]).astype(BF)
    ckvn = _rms(_dot(u, w_ref[:, c_ckv[0]:c_ckv[1]]), kvn_ref[...]).astype(BF)
    kr = _dot(u, w_ref[:, c_kr[0]:c_kr[1]])
    cos_t = cos_ref[...]
    sin_t = sin_ref[...]
    kp = kr[:, :LANES] * cos_t + kr[:, LANES:] * sin_t
    qa = _dot(cqn, wqa_ref[...])
    qb = _dot(cqn, wqb_ref[...])
    kn = _dot(ckvn, wkn_ref[...])
    for i in range(qa.shape[-1] // LANES):
        sl = slice(i * LANES, (i + 1) * LANES)
        q_ref[:, sl] = ((qa[:, sl] * cos_t + qb[:, sl] * sin_t) * scale).astype(BF)
        k_ref[:, sl] = (kn[:, sl] + kp).astype(BF)
    col = lax.broadcasted_iota(jnp.int32, (1, v_ref.shape[-1]), 1) % (2 * LANES)
    ones = jnp.where((col == MLA_V) | (col == LANES), 1.0, 0.0)
    v_ref[...] = (_dot(ckvn, wv_ref[...]) + ones).astype(BF)


def _inproj(h, norm_w, w_all, cols, q_norm, kv_norm, wqa, wqb, wkn, wv, cos_t, sin_t, layer, e):
    t, d = h.shape
    tm = _tile(t, 512)
    wc = w_all.shape[-1]
    n_z = cols[0][1] - cols[0][0]
    n_xbc = cols[1][1] - cols[1][0]
    hq = wqa.shape[-1]
    row = lambda i: (i, 0)
    return pl.pallas_call(
        functools.partial(_inproj_body, cols=cols, scale=math.log2(math.e) * (MLA_NOPE + MLA_ROPE) ** -0.5),
        grid=(t // tm,),
        in_specs=[
            pl.BlockSpec((tm, d), row),
            pl.BlockSpec((None, 1, d), lambda i: (layer, 0, 0)),
            _resident((None, d, wc), lambda i: (e, 0, 0)),
            pl.BlockSpec((None, 1, MLA_Q_RANK), lambda i: (e, 0, 0)),
            pl.BlockSpec((None, 1, MLA_KV_RANK), lambda i: (e, 0, 0)),
            _resident((None, MLA_Q_RANK, hq), lambda i: (e, 0, 0)),
            _resident((None, MLA_Q_RANK, hq), lambda i: (e, 0, 0)),
            _resident((None, MLA_KV_RANK, hq), lambda i: (e, 0, 0)),
            _resident((None, MLA_KV_RANK, hq), lambda i: (e, 0, 0)),
            pl.BlockSpec((tm, LANES), row),
            pl.BlockSpec((tm, LANES), row),
        ],
        out_specs=[
            pl.BlockSpec((tm, n_z), row),
            pl.BlockSpec((tm, n_xbc), row),
            pl.BlockSpec((tm, LANES), row),
            pl.BlockSpec((tm, hq), row),
            pl.BlockSpec((tm, hq), row),
            pl.BlockSpec((tm, hq), row),
        ],
        out_shape=[
            jax.ShapeDtypeStruct((t, n_z), BF),
            jax.ShapeDtypeStruct((t, n_xbc), F32),
            jax.ShapeDtypeStruct((t, LANES), F32),
            jax.ShapeDtypeStruct((t, hq), BF),
            jax.ShapeDtypeStruct((t, hq), BF),
            jax.ShapeDtypeStruct((t, hq), BF),
        ],
        compiler_params=_params("parallel"),
        name="mix_inproj",
    )(h, norm_w, w_all, q_norm, kv_norm, wqa, wqb, wkn, wv, cos_t, sin_t)


def _split3_dot(a_bf, x):
    hi = x.astype(BF)
    r1 = x - hi.astype(F32)
    mid = r1.astype(BF)
    low = (r1 - mid.astype(F32)).astype(BF)
    return _dot(a_bf, hi) + _dot(a_bf, mid) + _dot(a_bf, low)


def _ssd_states_phase(z, nc, xc_ref, xp_ref, xn_ref, dt_ref, cw_ref, cb_ref, bias_ref, a_ref, e_ref,
                      st_ref, x_ref, b_ref, c_ref, sel_ref, rows_ref, dec_ref, carry_ref, pad_ref):
    q = xc_ref.shape[0]
    nh, n = SSD_HEADS, SSD_STATE
    inner = nh * SSD_HEAD_DIM
    gw = inner // SSD_GROUPS
    half = SSD_CONV // 2
    edge = xp_ref.shape[0]
    cur = xc_ref[...]
    pad_ref[0:edge, :] = jnp.where(z > 0, xp_ref[...], 0.0)
    pad_ref[edge:edge + q, :] = cur
    pad_ref[edge + q:, :] = jnp.where(z < nc - 1, xn_ref[...], 0.0)
    acc = cur * cw_ref[half:half + 1, :] + cb_ref[...]
    for k in range(SSD_CONV):
        d = k - half
        if d != 0:
            acc = acc + pad_ref[edge + d:edge + d + q, :] * cw_ref[k:k + 1, :]
    xbc = jax.nn.silu(acc)
    x = xbc[:, :inner]
    bm = xbc[:, inner:inner + SSD_GROUPS * n].astype(BF)
    x_ref[z] = x.astype(BF)
    b_ref[z] = bm
    c_ref[z] = xbc[:, inner + SSD_GROUPS * n:].astype(BF)
    dt = jax.nn.softplus(dt_ref[...] + bias_ref[...])
    la = dt * (a_ref[...] * math.log2(math.e))
    row = lax.broadcasted_iota(jnp.int32, (q, q), 0)
    col = lax.broadcasted_iota(jnp.int32, (q, q), 1)
    tril = jnp.where(row >= col, 1.0, 0.0).astype(BF)
    cum = _split3_dot(tril, la)
    tot = cum[q - 1:q, :]
    rev = tot - cum + la
    fwd_lane = lax.broadcasted_iota(jnp.int32, (q, LANES), 1) < nh
    sel = jnp.where(fwd_lane, cum, rev)
    sel_ref[z] = sel
    dt_t = dt.T
    rows_ref[z, 0:2 * nh, :] = (sel - jnp.log2(dt)).T[0:2 * nh]
    rows_ref[z, 2 * nh:3 * nh, :] = jnp.log2(dt_t[0:nh] + dt_t[nh:2 * nh])
    w_all = (jnp.exp2(tot - sel) * dt).astype(BF)
    wexp = _dot(w_all, e_ref[...])
    etot = jnp.exp2(tot)
    lo1 = lax.broadcasted_iota(jnp.int32, (1, LANES), 1) < SSD_HEAD_DIM
    ppg = gw // LANES
    for d in range(2):
        xw = (x * wexp[:, d * inner:(d + 1) * inner]).astype(BF)
        for g in range(SSD_GROUPS):
            upd = lax.dot_general(bm[:, g * n:(g + 1) * n], xw[:, g * gw:(g + 1) * gw], TN_DIMS,
                                  preferred_element_type=F32)
            for j in range(ppg):
                p = g * ppg + j
                h0 = d * nh + 2 * p
                dec = jnp.where(lo1, etot[:, h0:h0 + 1], etot[:, h0 + 1:h0 + 2])
                contrib = upd[:, j * LANES:(j + 1) * LANES]
                if d == 0:
                    state = carry_ref[0, p]
                    st_ref[z, 0, p] = state
                    carry_ref[0, p] = state * dec + contrib
                else:
                    st_ref[z, 1, p] = contrib
                    dec_ref[z, p] = jnp.broadcast_to(dec, dec_ref.shape[2:])


def _ssd_output_phase(z, z_ref, d_ref, nw_ref, o_ref, st_ref, x_ref, b_ref, c_ref, sel_ref, rows_ref,
                      dec_ref, carry_ref):
    q = z_ref.shape[0]
    nh, n, hpg = SSD_HEADS, SSD_STATE, SSD_HEADS // SSD_GROUPS
    sel = sel_ref[z]
    rows = rows_ref[z]
    row = lax.broadcasted_iota(jnp.int32, (q, q), 0)
    col = lax.broadcasted_iota(jnp.int32, (q, q), 1)
    lower = row > col
    diag = row == col
    lo = lax.broadcasted_iota(jnp.int32, (q, LANES), 1) < SSD_HEAD_DIM
    ys = []
    for g in range(SSD_GROUPS):
        bg = b_ref[z, :, g * n:(g + 1) * n]
        cg = c_ref[z, :, g * n:(g + 1) * n]
        cb = lax.dot_general(cg, bg, NT_DIMS, preferred_element_type=F32)
        cg32 = cg.astype(F32)
        for j in range(hpg // 2):
            p = g * (hpg // 2) + j
            lhs = []
            for h in (2 * p, 2 * p + 1):
                hb = nh + h
                a_f = jnp.broadcast_to(sel[:, h:h + 1], (q, q))
                a_b = jnp.broadcast_to(sel[:, hb:hb + 1], (q, q))
                seg = jnp.where(lower, a_f - rows[h:h + 1, :],
                                jnp.where(diag, rows[2 * nh + h:2 * nh + h + 1, :], a_b - rows[hb:hb + 1, :]))
                m = (cb * jnp.exp2(seg)).astype(BF)
                cef = (cg32 * jnp.exp2(a_f)).astype(BF)
                ceb = (cg32 * jnp.exp2(a_b)).astype(BF)
                lhs.append(jnp.concatenate([m, cef, ceb], axis=1))
            xp = x_ref[z, :, p * LANES:(p + 1) * LANES]
            back = carry_ref[1, p]
            carry_ref[1, p] = back * dec_ref[z, p][0:1] + st_ref[z, 1, p]
            rhs = jnp.concatenate([xp, st_ref[z, 0, p].astype(BF), back.astype(BF)], axis=0)
            out = _dot(jnp.concatenate(lhs, axis=0), rhs)
            ys.append(jnp.where(lo, out[:q], out[q:]) + xp.astype(F32) * d_ref[:, p * LANES:(p + 1) * LANES])
    y = jnp.concatenate(ys, axis=-1)
    gated = y * jax.nn.silu(z_ref[...].astype(F32))
    o_ref[...] = _rms(gated, nw_ref[...]).astype(BF)


def _ssd_body(xc_ref, xp_ref, xn_ref, dt_ref, z_ref, cw_ref, cb_ref, bias_ref, a_ref, d_ref, nw_ref, e_ref,
              o_ref, st_ref, x_ref, b_ref, c_ref, sel_ref, rows_ref, dec_ref, carry_ref, pad_ref):
    t = pl.program_id(1)
    nc = st_ref.shape[0]

    @pl.when(t == 0)
    def _():
        carry_ref[0] = jnp.zeros(carry_ref.shape[1:], F32)

    @pl.when(t == nc)
    def _():
        carry_ref[1] = jnp.zeros(carry_ref.shape[1:], F32)

    @pl.when(t < nc)
    def _():
        _ssd_states_phase(t, nc, xc_ref, xp_ref, xn_ref, dt_ref, cw_ref, cb_ref, bias_ref, a_ref, e_ref,
                          st_ref, x_ref, b_ref, c_ref, sel_ref, rows_ref, dec_ref, carry_ref, pad_ref)

    @pl.when(t >= nc)
    def _():
        _ssd_output_phase(2 * nc - 1 - t, z_ref, d_ref, nw_ref, o_ref, st_ref, x_ref, b_ref, c_ref, sel_ref,
                          rows_ref, dec_ref, carry_ref)


def _head_expand_table():
    inner = SSD_HEADS * SSD_HEAD_DIM
    e = np.zeros((LANES, 2 * inner), np.float32)
    for h in range(2 * SSD_HEADS):
        e[h, h * SSD_HEAD_DIM:(h + 1) * SSD_HEAD_DIM] = 1.0
    return jnp.asarray(e, BF)


def _ssd(xbc, dt_raw, z, conv_w, conv_b, dt_bias, a_neg, d_exp, ssd_norm, e, batch):
    t, c = xbc.shape
    s = t // batch
    inner = SSD_HEADS * SSD_HEAD_DIM
    gn = SSD_GROUPS * SSD_STATE
    q = SSD_CHUNK
    nc = s // q
    edge = 8
    epc = q // edge
    chunk1 = lambda i: jnp.minimum(i, nc - 1)
    chunk3 = lambda i: nc - 1 - jnp.maximum(i - nc, 0)
    par = lambda b, i: (e, 0, 0)
    return pl.pallas_call(
        _ssd_body,
        grid=(batch, 2 * nc),
        in_specs=[
            pl.BlockSpec((q, c), lambda b, i: (b * nc + chunk1(i), 0)),
            pl.BlockSpec((edge, c), lambda b, i: (b * nc * epc + jnp.maximum(chunk1(i) * epc - 1, 0), 0)),
            pl.BlockSpec((edge, c), lambda b, i: (b * nc * epc + jnp.minimum((chunk1(i) + 1) * epc, nc * epc - 1), 0)),
            pl.BlockSpec((q, LANES), lambda b, i: (b * nc + chunk1(i), 0)),
            pl.BlockSpec((q, inner), lambda b, i: (b * nc + chunk3(i), 0)),
            pl.BlockSpec((None, SSD_CONV, c), par),
            pl.BlockSpec((None, 1, c), par),
            pl.BlockSpec((None, 1, LANES), par),
            pl.BlockSpec((None, 1, LANES), par),
            pl.BlockSpec((None, 1, inner), par),
            pl.BlockSpec((None, 1, inner), par),
            _resident((LANES, 2 * inner), lambda b, i: (0, 0)),
        ],
        out_specs=pl.BlockSpec((q, inner), lambda b, i: (b * nc + chunk3(i), 0)),
        out_shape=jax.ShapeDtypeStruct((t, inner), BF),
        scratch_shapes=[
            pltpu.VMEM((nc, 2, SSD_HEADS // 2, SSD_STATE, LANES), F32),
            pltpu.VMEM((nc, q, inner), BF),
            pltpu.VMEM((nc, q, gn), BF),
            pltpu.VMEM((nc, q, gn), BF),
            pltpu.VMEM((nc, q, LANES), F32),
            pltpu.VMEM((nc, 3 * SSD_HEADS, q), F32),
            pltpu.VMEM((nc, SSD_HEADS // 2, edge, LANES), F32),
            pltpu.VMEM((2, SSD_HEADS // 2, SSD_STATE, LANES), F32),
            pltpu.VMEM((q + 2 * edge, c), F32),
        ],
        compiler_params=_params("parallel", "arbitrary"),
        name="ssd",
    )(xbc, xbc, xbc, dt_raw, z, conv_w, conv_b, dt_bias, a_neg, d_exp, ssd_norm, _head_expand_table())


def _lane_fold(x, op):
    out = x[:, :LANES]
    for j in range(1, x.shape[-1] // LANES):
        out = op(out, x[:, j * LANES:(j + 1) * LANES])
    return out


def _mla_body(q_ref, k_ref, v_ref, o_ref, s_ref, m_ref, *, kt):
    nk = k_ref.shape[0]
    _, rb, _ = s_ref.shape
    nrb = q_ref.shape[0] // rb

    def score_pass(r, head):
        rows = pl.ds(pl.multiple_of(r * rb, rb), rb)
        sl = slice(head * LANES, (head + 1) * LANES)
        q = q_ref[rows, sl]
        m_acc = None
        for c in range(0, nk, kt):
            s = lax.dot_general(q, k_ref[c:c + kt, sl], NT_DIMS, preferred_element_type=F32)
            s_ref[head, :, c:c + kt] = s
            mc = _lane_fold(s, jnp.maximum)
            m_acc = mc if m_acc is None else jnp.maximum(m_acc, mc)
        m_ref[head] = jnp.max(m_acc, axis=-1, keepdims=True)

    def value_pass(head):
        sl = slice(head * LANES, (head + 1) * LANES)
        m = m_ref[head]
        o = jnp.zeros((rb, LANES), F32)
        for c in range(0, nk, kt):
            p = jnp.exp2(s_ref[head, :, c:c + kt] - m)
            o = o + _dot(p.astype(BF), v_ref[c:c + kt, sl])
        return o

    def store(r, o0, o1):
        lo = lax.broadcasted_iota(jnp.int32, (rb, LANES), 1) < MLA_V
        out = jnp.where(lo, o0 / o0[:, MLA_V:MLA_V + 1], o1 / o1[:, 0:1])
        o_ref[pl.ds(pl.multiple_of(r * rb, rb), rb), :] = out.astype(BF)

    def row_block(r, carry):
        score_pass(r, 1)
        o0 = value_pass(0)
        score_pass(r + 1, 0)
        o1 = value_pass(1)
        store(r, o0, o1)
        return carry

    score_pass(0, 0)
    lax.fori_loop(0, nrb - 1, row_block, 0)
    score_pass(nrb - 1, 1)
    o0 = value_pass(0)
    o1 = value_pass(1)
    store(nrb - 1, o0, o1)


def _mla(q, k, v, batch):
    t, hq = q.shape
    s = t // batch
    pw = 2 * LANES
    rb = _tile(s, 256)
    q3, k3, v3 = (a.reshape(batch, s, hq) for a in (q, k, v))
    blk = pl.BlockSpec((None, s, pw), lambda b, p: (b, 0, p))
    o = pl.pallas_call(
        functools.partial(_mla_body, kt=_tile(s, 256)),
        grid=(batch, hq // pw),
        in_specs=[blk, blk, blk],
        out_specs=pl.BlockSpec((None, s, LANES), lambda b, p: (b, 0, p)),
        out_shape=jax.ShapeDtypeStruct((batch, s, MLA_HEADS * MLA_V), BF),
        scratch_shapes=[pltpu.VMEM((pw // LANES, rb, s), F32), pltpu.VMEM((pw // LANES, rb, 1), F32)],
        compiler_params=_params("parallel", "parallel"),
        name="mla_attn",
    )(q3, k3, v3)
    return o.reshape(t, MLA_HEADS * MLA_V)


def _outproj_body(h_ref, y_ref, o_ref, wy_ref, wo_ref, out_ref):
    out_ref[...] = h_ref[...] + _dot(y_ref[...], wy_ref[...]) + _dot(o_ref[...], wo_ref[...])


def _outproj(h, y_ssd, o_mla, w_out, e):
    t, d = h.shape
    ny, no = y_ssd.shape[1], o_mla.shape[1]
    assert ny % no == 0
    tm = _tile(t, 512)
    row = lambda i: (i, 0)
    return pl.pallas_call(
        _outproj_body,
        grid=(t // tm,),
        in_specs=[
            pl.BlockSpec((tm, d), row),
            pl.BlockSpec((tm, ny), row),
            pl.BlockSpec((tm, no), row),
            _resident((None, ny, d), lambda i: (e, 0, 0)),
            _resident((None, no, d), lambda i: (e, ny // no, 0)),
        ],
        out_specs=pl.BlockSpec((tm, d), row),
        out_shape=jax.ShapeDtypeStruct((t, d), F32),
        compiler_params=_params("parallel"),
        name="mix_outproj",
    )(h, y_ssd, o_mla, w_out, w_out)


def _fnet_fold_body(ha_ref, hm_ref, hx_ref, nw_ref, cc_ref, sc_ref, ec_ref, es_ref):
    tm = ha_ref.shape[0]
    nw = nw_ref[...]
    u_a = _rms(ha_ref[...], nw)
    u_m = _rms(hm_ref[...], nw)
    u_x = _rms(hx_ref[...], nw)[0:1]
    r = lax.broadcasted_iota(jnp.int32, (tm, tm), 0)
    c = lax.broadcasted_iota(jnp.int32, (tm, tm), 1)
    perm = jnp.where(r + c == tm, 1.0, 0.0).astype(BF)
    hi = u_m.astype(BF)
    rest = u_m - hi.astype(F32)
    mid = rest.astype(BF)
    low = (rest - mid.astype(F32)).astype(BF)
    mirror = _dot(perm, hi) + _dot(perm, mid) + _dot(perm, low)
    first = lax.broadcasted_iota(jnp.int32, u_a.shape, 0) == 0
    mirror = jnp.where(first, u_x, mirror)
    ue = (u_a + mirror).astype(BF)
    uo = (u_a - mirror).astype(BF)
    gc = cc_ref.shape[0]
    for g in range(ue.shape[-1] // gc):
        sl = slice(g * gc, (g + 1) * gc)
        ec_ref[:, sl] = _dot(ue[:, sl], cc_ref[...]).astype(BF)
        es_ref[:, sl] = _dot(uo[:, sl], sc_ref[...]).astype(BF)


def _fnet_seq_body(h_ref, hh_ref, nw_ref, cc_ref, cs_ref, ss_ref, ec_ref, es_ref, w_ref, o_ref, *, scale):
    ts = h_ref.shape[0]
    rows = pl.ds(pl.multiple_of(pl.program_id(1) * ts, ts), ts)
    y = _dot(cs_ref[rows, :], ec_ref[...]) - _dot(ss_ref[rows, :], es_ref[...])
    u_h = _rms(hh_ref[...], nw_ref[...]).astype(BF)
    gc = cc_ref.shape[0]
    x_h = jnp.concatenate([_dot(u_h[:, g * gc:(g + 1) * gc], cc_ref[...])
                           for g in range(u_h.shape[-1] // gc)], axis=-1)[0:1]
    odd = lax.broadcasted_iota(jnp.int32, (ts, 1), 0) % 2 == 1
    y = y + jnp.where(odd, -1.0, 1.0) * x_h
    o_ref[...] = h_ref[...] + _dot((y * scale).astype(BF), w_ref[...])


def _dft_tables(n, fold):
    j = np.arange(n)[:, None]
    k = np.arange(n // 2 if fold else n)[None, :]
    ang = ((j * k) % n) * (2.0 * np.pi / n)
    cos, sin = np.cos(ang), np.sin(ang)
    if fold:
        cos[:, 0] = 0.5
    return jnp.asarray(cos, BF), jnp.asarray(sin, BF)


def _fnet(h, norm_w, w_out, layer, o, batch, tables):
    t, d = h.shape
    s = t // batch
    cc, sc, cs, ss = tables
    gc = cc.shape[0]
    sub = 8
    tm = _tile(s // 2, 256)
    nt, nf = s // tm, s // 2 // tm
    nwspec = pl.BlockSpec((None, 1, d), lambda b, i: (layer, 0, 0))
    table = _resident((gc, gc), lambda b, i: (0, 0))
    ec, es = pl.pallas_call(
        _fnet_fold_body,
        grid=(batch, nf),
        in_specs=[
            pl.BlockSpec((tm, d), lambda b, i: (b * nt + i, 0)),
            pl.BlockSpec((tm, d), lambda b, i: (b * nt + nt - 1 - i, 0)),
            pl.BlockSpec((sub, d), lambda b, i: (b * (s // sub) + ((nt - i) % nt) * (tm // sub), 0)),
            nwspec, table, table,
        ],
        out_specs=[pl.BlockSpec((tm, d), lambda b, i: (b * nf + i, 0))] * 2,
        out_shape=[jax.ShapeDtypeStruct((t // 2, d), BF)] * 2,
        compiler_params=_params("parallel", "parallel"),
        name="fnet_fold_channel_dft",
    )(h, h, h, norm_w, cc, sc)
    ts = _tile(s, 256)
    ns = s // ts
    return pl.pallas_call(
        functools.partial(_fnet_seq_body, scale=(s * gc) ** -0.5),
        grid=(batch, ns),
        in_specs=[
            pl.BlockSpec((ts, d), lambda b, i: (b * ns + i, 0)),
            pl.BlockSpec((sub, d), lambda b, i: (b * (s // sub) + s // 2 // sub, 0)),
            nwspec, table,
            _resident((s, s // 2), lambda b, i: (0, 0)),
            _resident((s, s // 2), lambda b, i: (0, 0)),
            pl.BlockSpec((s // 2, d), lambda b, i: (b, 0)),
            pl.BlockSpec((s // 2, d), lambda b, i: (b, 0)),
            _resident((None, d, d), lambda b, i: (o, 0, 0)),
        ],
        out_specs=pl.BlockSpec((ts, d), lambda b, i: (b * ns + i, 0)),
        out_shape=jax.ShapeDtypeStruct((t, d), F32),
        compiler_params=_params("parallel", "arbitrary"),
        name="fnet_seq_dft",
    )(h, h, norm_w, cc, cs, ss, ec, es, w_out)


def _mixer_weights(w_in, w_uq, w_ukv, dt_bias, a_log, ssd_d):
    ne, d, _ = w_in.shape
    inner = SSD_HEADS * SSD_HEAD_DIM
    conv_ch = inner + 2 * SSD_GROUPS * SSD_STATE
    o_z, o_xbc = 0, inner
    o_dt = o_xbc + conv_ch
    o_cq = o_dt + 2 * SSD_HEADS
    o_ckv = o_cq + MLA_Q_RANK
    o_kr = o_ckv + MLA_KV_RANK
    half = MLA_ROPE // 2
    pad = LANES - MLA_NOPE - MLA_ROPE
    zeros = lambda *s: jnp.zeros(s, w_in.dtype)
    w_dt = jnp.concatenate([w_in[:, :, o_dt:o_cq], zeros(ne, d, LANES - 2 * SSD_HEADS)], axis=-1)
    kr1 = w_in[:, :, o_kr:o_kr + half]
    kr2 = w_in[:, :, o_kr + half:o_kr + MLA_ROPE]
    kr_a = jnp.concatenate([zeros(ne, d, MLA_NOPE), kr1, kr2, zeros(ne, d, pad)], axis=-1)
    kr_b = jnp.concatenate([zeros(ne, d, MLA_NOPE), -kr2, kr1, zeros(ne, d, pad)], axis=-1)
    pieces = [w_in[:, :, o_z:o_xbc], w_in[:, :, o_xbc:o_dt], w_dt, w_in[:, :, o_cq:o_ckv],
              w_in[:, :, o_ckv:o_kr], jnp.concatenate([kr_a, kr_b], axis=-1)]
    cols, c = [], 0
    for p in pieces:
        cols.append((c, c + p.shape[-1]))
        c += p.shape[-1]
    w_all = jnp.concatenate(pieces, axis=-1).astype(BF)

    uq = w_uq.reshape(ne, MLA_Q_RANK, MLA_HEADS, MLA_NOPE + MLA_ROPE)
    q_nope, q1, q2 = uq[..., :MLA_NOPE], uq[..., MLA_NOPE:MLA_NOPE + half], uq[..., MLA_NOPE + half:]
    zq = lambda n: jnp.zeros((ne, MLA_Q_RANK, MLA_HEADS, n), w_uq.dtype)
    hq = MLA_HEADS * LANES
    wqa = jnp.concatenate([q_nope, q1, q2, zq(pad)], axis=-1).reshape(ne, MLA_Q_RANK, hq).astype(BF)
    wqb = jnp.concatenate([zq(MLA_NOPE), -q2, q1, zq(pad)], axis=-1).reshape(ne, MLA_Q_RANK, hq).astype(BF)

    ukv = w_ukv.reshape(ne, MLA_KV_RANK, MLA_HEADS, MLA_NOPE + MLA_V)
    zkv = lambda *s: jnp.zeros((ne, MLA_KV_RANK) + s, w_ukv.dtype)
    wkn = jnp.concatenate([ukv[..., :MLA_NOPE], zkv(MLA_HEADS, LANES - MLA_NOPE)], axis=-1)
    wkn = wkn.reshape(ne, MLA_KV_RANK, hq).astype(BF)
    vv = ukv[..., MLA_NOPE:].reshape(ne, MLA_KV_RANK, MLA_HEADS // 2, 2, MLA_V)
    zv = zkv(MLA_HEADS // 2, MLA_V)
    wv = jnp.stack([jnp.concatenate([vv[:, :, :, 0], zv], axis=-1),
                    jnp.concatenate([zv, vv[:, :, :, 1]], axis=-1)], axis=3)
    wv = wv.reshape(ne, MLA_KV_RANK, hq).astype(BF)

    padl = lambda a: jnp.concatenate([a, jnp.zeros((ne, LANES - a.shape[-1]), a.dtype)], axis=-1)[:, None, :]
    bias = padl(dt_bias.reshape(ne, 2 * SSD_HEADS))
    a_neg = padl(-jnp.exp(a_log.reshape(ne, 2 * SSD_HEADS)))
    d_exp = jnp.repeat(ssd_d, SSD_HEAD_DIM, axis=-1)[:, None, :]
    return w_all, tuple(cols), wqa, wqb, wkn, wv, bias, a_neg, d_exp


def _rope_tables(positions):
    half = MLA_ROPE // 2
    inv = 1.0 / (ROPE_THETA ** (jnp.arange(0, MLA_ROPE, 2, dtype=F32) / MLA_ROPE))
    ang = positions.astype(F32).reshape(-1, 1) * inv
    cos, sin = jnp.cos(ang), jnp.sin(ang)
    t = cos.shape[0]
    pad = LANES - MLA_NOPE - MLA_ROPE
    cos_t = jnp.concatenate([jnp.ones((t, MLA_NOPE), F32), cos, cos, jnp.zeros((t, pad), F32)], axis=-1)
    sin_t = jnp.concatenate([jnp.zeros((t, MLA_NOPE), F32), sin, sin, jnp.zeros((t, pad), F32)], axis=-1)
    return cos_t, sin_t


def kernel(x, mem, positions, mem_norm, final_norm, ffn1_norm, ffn1_w_gu, ffn1_w_down, mix_norm, xa_norm,
           xa_wq, xa_wkv, xa_wo, ffn2_norm, ffn2_w_gu, ffn2_w_down, w_in, conv_w, conv_b, dt_bias, a_log,
           ssd_d, ssd_norm, q_norm, w_uq, kv_norm, w_ukv, w_out, fnet_w_out):
    batch, seq, d = x.shape
    depth = ffn1_norm.shape[0]
    t = batch * seq
    bf = lambda a: a.astype(BF)
    row3 = lambda a: a[:, None, :]

    kv = _kvproj(mem.reshape(-1, d), mem_norm[None, :], bf(xa_wkv)).reshape(depth, batch, mem.shape[1], 2 * d)
    w_all, cols, wqa, wqb, wkn, wv, bias, a_neg, d_exp = _mixer_weights(w_in, w_uq, w_ukv, dt_bias, a_log, ssd_d)
    cos_t, sin_t = _rope_tables(positions)
    gc = d // FNET_GROUPS
    tables = _dft_tables(gc, fold=False) + _dft_tables(seq, fold=True)
    f1n, f2n, mxn, xan = row3(ffn1_norm), row3(ffn2_norm), row3(mix_norm), row3(xa_norm)
    f1gu, f1d, f2gu, f2d = bf(ffn1_w_gu), bf(ffn1_w_down), bf(ffn2_w_gu), bf(ffn2_w_down)
    wq, wo, w_mix_out, w_fnet = bf(xa_wq), bf(xa_wo), bf(w_out), bf(fnet_w_out)
    ssd_nw, qn, kvn = row3(ssd_norm), row3(q_norm), row3(kv_norm)

    h = x.reshape(t, d)
    for layer in range(depth):
        h = _ffn(h, f1n, f1gu, f1d, layer)
        if layer % 2 == 0:
            e = layer // 2
            z, xbc, dt_raw, q, k, v = _inproj(h, mxn, w_all, cols, qn, kvn, wqa, wqb, wkn, wv,
                                              cos_t, sin_t, layer, e)
            y_ssd = _ssd(xbc, dt_raw, z, conv_w, row3(conv_b), bias, a_neg, d_exp, ssd_nw, e, batch)
            o_mla = _mla(q, k, v, batch)
            h = _outproj(h, y_ssd, o_mla, w_mix_out, e)
        else:
            h = _fnet(h, mxn, w_fnet, layer, layer // 2, batch, tables)
        h = _xattn(h, xan, wq, kv, wo, layer, batch)
        h = _ffn(h, f2n, f2gu, f2d, layer, final_w=final_norm[None, :] if layer == depth - 1 else None)
    return h.reshape(batch, seq, d)
```

```python
import functools
import math

import numpy as np
import jax
import jax.numpy as jnp
from jax import lax
from jax.experimental import pallas as pl
from jax.experimental.pallas import tpu as pltpu

EPS = 1e-6
BF = jnp.bfloat16
F32 = jnp.float32

V7X_VMEM_BYTES = 64 * 1024 * 1024
VMEM_LIMIT = V7X_VMEM_BYTES - 8 * 1024 * 1024
LANES = 128

SSD_HEADS = 16
SSD_HEAD_DIM = 64
SSD_GROUPS = 2
SSD_STATE = 128
SSD_CONV = 5
SSD_CHUNK = 128
MLA_HEADS = 8
MLA_Q_RANK = 512
MLA_KV_RANK = 256
MLA_NOPE = 64
MLA_ROPE = 32
MLA_V = 64
ROPE_THETA = 10000.0
FNET_GROUPS = 4
XA_HEADS = 4

NT_DIMS = (((1,), (1,)), ((), ()))
TN_DIMS = (((0,), (0,)), ((), ()))


def _params(*sem):
    return pltpu.CompilerParams(dimension_semantics=sem, vmem_limit_bytes=VMEM_LIMIT)


def _resident(shape, index_map):
    return pl.BlockSpec(shape, index_map, pipeline_mode=pl.Buffered(1))


def _rms(x, w):
    return x * lax.rsqrt(jnp.mean(x * x, axis=-1, keepdims=True) + EPS) * w


def _dot(a, b):
    return jnp.dot(a, b, preferred_element_type=F32)


def _tile(n, pref):
    t = min(n, pref)
    assert n % t == 0, (n, t)
    return t


def _ffn_body(*refs, chunks, final):
    if final:
        h_ref, nw_ref, wg_ref, wu_ref, wd_ref, fw_ref, o_ref = refs
    else:
        h_ref, nw_ref, wg_ref, wu_ref, wd_ref, o_ref = refs
    h = h_ref[...]
    xn = _rms(h, nw_ref[...]).astype(BF)
    acc = jnp.zeros(h.shape, F32)
    for a, b in chunks:
        g = _dot(xn, wg_ref[:, a:b])
        u = _dot(xn, wu_ref[:, a:b])
        act = (jax.nn.silu(g) * u).astype(BF)
        acc = acc + _dot(act, wd_ref[a:b, :])
    out = h + 0.5 * acc
    if final:
        out = _rms(out, fw_ref[...])
    o_ref[...] = out


def _ffn(h, norm_w, w_gu, w_down, layer, final_w=None):
    t, d = h.shape
    f = w_down.shape[1]
    tm = _tile(t, 512)
    step = 768
    chunks = tuple((a, min(a + step, f)) for a in range(0, f, step))
    in_specs = [
        pl.BlockSpec((tm, d), lambda i: (i, 0)),
        pl.BlockSpec((None, 1, d), lambda i: (layer, 0, 0)),
        _resident((None, d, f), lambda i: (layer, 0, 0)),
        _resident((None, d, f), lambda i: (layer, 0, 1)),
        _resident((None, f, d), lambda i: (layer, 0, 0)),
    ]
    args = [h, norm_w, w_gu, w_gu, w_down]
    if final_w is not None:
        in_specs.append(pl.BlockSpec((1, d), lambda i: (0, 0)))
        args.append(final_w)
    return pl.pallas_call(
        functools.partial(_ffn_body, chunks=chunks, final=final_w is not None),
        grid=(t // tm,),
        in_specs=in_specs,
        out_specs=pl.BlockSpec((tm, d), lambda i: (i, 0)),
        out_shape=jax.ShapeDtypeStruct((t, d), F32),
        compiler_params=_params("parallel"),
        name="ffn",
    )(*args)


def _kvproj_body(m_ref, nw_ref, w_ref, o_ref):
    mn = _rms(m_ref[...], nw_ref[...]).astype(BF)
    o_ref[...] = _dot(mn, w_ref[...]).astype(BF)


def _kvproj(mem2d, mem_norm, wkv):
    n, d = mem2d.shape
    nl, _, d2 = wkv.shape
    tm = _tile(n, 512)
    return pl.pallas_call(
        _kvproj_body,
        grid=(nl, n // tm),
        in_specs=[
            pl.BlockSpec((tm, d), lambda l, i: (i, 0)),
            pl.BlockSpec((1, d), lambda l, i: (0, 0)),
            pl.BlockSpec((None, d, d2), lambda l, i: (l, 0, 0)),
        ],
        out_specs=pl.BlockSpec((None, tm, d2), lambda l, i: (l, i, 0)),
        out_shape=jax.ShapeDtypeStruct((nl, n, d2), BF),
        compiler_params=_params("parallel", "parallel"),
        name="xa_kvproj",
    )(mem2d, mem_norm, wkv)


def _xa_body(*refs, heads, mixed):
    if mixed:
        h_ref, nw_ref, wq_ref, k_ref, v_ref, wo_ref, y_ref, a_ref, wy_ref, wa_ref, o_ref = refs
        h = h_ref[...] + _dot(y_ref[...], wy_ref[...]) + _dot(a_ref[...], wa_ref[...])
    else:
        h_ref, nw_ref, wq_ref, k_ref, v_ref, wo_ref, o_ref = refs
        h = h_ref[...]
    hn = _rms(h, nw_ref[...]).astype(BF)
    dh = h.shape[-1] // heads
    q = (_dot(hn, wq_ref[...]) * (dh ** -0.5)).astype(BF)
    outs = []
    for i in range(heads):
        sl = slice(i * dh, (i + 1) * dh)
        s = lax.dot_general(q[:, sl], k_ref[:, sl], NT_DIMS, preferred_element_type=F32)
        p = jnp.exp(s - jnp.max(s, axis=-1, keepdims=True))
        l = jnp.sum(p, axis=-1, keepdims=True)
        outs.append((_dot(p.astype(BF), v_ref[:, sl]) / l).astype(BF))
    o = jnp.concatenate(outs, axis=-1)
    o_ref[...] = h + _dot(o, wo_ref[...])


def _xattn(h, norm_w, wq, kv, wo, layer, batch, mix=None):
    t, d = h.shape
    s = t // batch
    nm = kv.shape[2]
    tm = _tile(s, 512)
    ns = s // tm
    row = lambda b, i: (b * ns + i, 0)
    in_specs = [
        pl.BlockSpec((tm, d), row),
        pl.BlockSpec((None, 1, d), lambda b, i: (layer, 0, 0)),
        _resident((None, d, d), lambda b, i: (layer, 0, 0)),
        pl.BlockSpec((None, None, nm, d), lambda b, i: (layer, b, 0, 0)),
        pl.BlockSpec((None, None, nm, d), lambda b, i: (layer, b, 0, 1)),
        _resident((None, d, d), lambda b, i: (layer, 0, 0)),
    ]
    args = [h, norm_w, wq, kv, kv, wo]
    if mix is not None:
        y_ssd, o_mla, w_out, e = mix
        ny, na = y_ssd.shape[1], o_mla.shape[1]
        assert ny % na == 0
        in_specs += [
            pl.BlockSpec((tm, ny), row),
            pl.BlockSpec((tm, na), row),
            _resident((None, ny, d), lambda b, i: (e, 0, 0)),
            _resident((None, na, d), lambda b, i: (e, ny // na, 0)),
        ]
        args += [y_ssd, o_mla, w_out, w_out]
    return pl.pallas_call(
        functools.partial(_xa_body, heads=XA_HEADS, mixed=mix is not None),
        grid=(batch, ns),
        in_specs=in_specs,
        out_specs=pl.BlockSpec((tm, d), row),
        out_shape=jax.ShapeDtypeStruct((t, d), F32),
        compiler_params=_params("parallel", "parallel"),
        name="xattn",
    )(*args)


def _inproj_body(h_ref, nw_ref, w_ref, qn_ref, kvn_ref, wqa_ref, wqb_ref, wkn_ref, wv_ref,
                 cos_ref, sin_ref, z_ref, xbc_ref, dt_ref, q_ref, k_ref, v_ref, *, cols, scale):
    c_z, c_xbc, c_dt, c_cq, c_ckv, c_kr = cols
    u = _rms(h_ref[...], nw_ref[...]).astype(BF)
    z_ref[...] = _dot(u, w_ref[:, c_z[0]:c_z[1]]).astype(BF)
    xbc_ref[...] = _dot(u, w_ref[:, c_xbc[0]:c_xbc[1]])
    dt_ref[...] = _dot(u, w_ref[:, c_dt[0]:c_dt[1]])
    cqn = _rms(_dot(u, w_ref[:, c_cq[0]:c_cq[1]]), qn_ref[...]).astype(BF)
    ckvn = _rms(_dot(u, w_ref[:, c_ckv[0]:c_ckv[1]]), kvn_ref[...]).astype(BF)
    kr = _dot(u, w_ref[:, c_kr[0]:c_kr[1]])
    cos_t = cos_ref[...]
    sin_t = sin_ref[...]
    kp = kr[:, :LANES] * cos_t + kr[:, LANES:] * sin_t
    qa = _dot(cqn, wqa_ref[...])
    qb = _dot(cqn, wqb_ref[...])
    kn = _dot(ckvn, wkn_ref[...])
    for i in range(qa.shape[-1] // LANES):
        sl = slice(i * LANES, (i + 1) * LANES)
        q_ref[:, sl] = ((qa[:, sl] * cos_t + qb[:, sl] * sin_t) * scale).astype(BF)
        k_ref[:, sl] = (kn[:, sl] + kp).astype(BF)
    col = lax.broadcasted_iota(jnp.int32, (1, v_ref.shape[-1]), 1) % (2 * LANES)
    ones = jnp.where((col == MLA_V) | (col == LANES), 1.0, 0.0)
    v_ref[...] = (_dot(ckvn, wv_ref[...]) + ones).astype(BF)


def _inproj(h, norm_w, w_all, cols, q_norm, kv_norm, wqa, wqb, wkn, wv, cos_t, sin_t, layer, e):
    t, d = h.shape
    tm = _tile(t, 512)
    wc = w_all.shape[-1]
    n_z = cols[0][1] - cols[0][0]
    n_xbc = cols[1][1] - cols[1][0]
    hq = wqa.shape[-1]
    row = lambda i: (i, 0)
    return pl.pallas_call(
        functools.partial(_inproj_body, cols=cols, scale=math.log2(math.e) * (MLA_NOPE + MLA_ROPE) ** -0.5),
        grid=(t // tm,),
        in_specs=[
            pl.BlockSpec((tm, d), row),
            pl.BlockSpec((None, 1, d), lambda i: (layer, 0, 0)),
            _resident((None, d, wc), lambda i: (e, 0, 0)),
            pl.BlockSpec((None, 1, MLA_Q_RANK), lambda i: (e, 0, 0)),
            pl.BlockSpec((None, 1, MLA_KV_RANK), lambda i: (e, 0, 0)),
            _resident((None, MLA_Q_RANK, hq), lambda i: (e, 0, 0)),
            _resident((None, MLA_Q_RANK, hq), lambda i: (e, 0, 0)),
            _resident((None, MLA_KV_RANK, hq), lambda i: (e, 0, 0)),
            _resident((None, MLA_KV_RANK, hq), lambda i: (e, 0, 0)),
            pl.BlockSpec((tm, LANES), row),
            pl.BlockSpec((tm, LANES), row),
        ],
        out_specs=[
            pl.BlockSpec((tm, n_z), row),
            pl.BlockSpec((tm, n_xbc), row),
            pl.BlockSpec((tm, LANES), row),
            pl.BlockSpec((tm, hq), row),
            pl.BlockSpec((tm, hq), row),
            pl.BlockSpec((tm, hq), row),
        ],
        out_shape=[
            jax.ShapeDtypeStruct((t, n_z), BF),
            jax.ShapeDtypeStruct((t, n_xbc), F32),
            jax.ShapeDtypeStruct((t, LANES), F32),
            jax.ShapeDtypeStruct((t, hq), BF),
            jax.ShapeDtypeStruct((t, hq), BF),
            jax.ShapeDtypeStruct((t, hq), BF),
        ],
        compiler_params=_params("parallel"),
        name="mix_inproj",
    )(h, norm_w, w_all, q_norm, kv_norm, wqa, wqb, wkn, wv, cos_t, sin_t)


def _split3_dot(a_bf, x):
    hi = x.astype(BF)
    r1 = x - hi.astype(F32)
    mid = r1.astype(BF)
    low = (r1 - mid.astype(F32)).astype(BF)
    return _dot(a_bf, hi) + _dot(a_bf, mid) + _dot(a_bf, low)


def _ssd_states_phase(z, j, first, last, xc_ref, xp_ref, xn_ref, dt_ref, cw_ref, cb_ref, bias_ref, a_ref, e_ref,
                      sm_ref, st_ref, x_ref, b_ref, c_ref, sel_ref, rows_ref, dec_ref, carry_ref):
    q = SSD_CHUNK
    nh, n = SSD_HEADS, SSD_STATE
    inner = nh * SSD_HEAD_DIM
    gw = inner // SSD_GROUPS
    half = SSD_CONV // 2
    edge = xp_ref.shape[0]
    nsub = xc_ref.shape[0] // q
    r0 = pl.multiple_of(j * q, q)
    cur = xc_ref[pl.ds(r0, q), :]
    before = xc_ref[pl.ds(pl.multiple_of(jnp.maximum(r0 - edge, 0), edge), edge), :]
    after = xc_ref[pl.ds(pl.multiple_of(jnp.minimum(r0 + q, (nsub - 1) * q + q - edge), edge), edge), :]
    prev = jnp.where(first, 0.0, jnp.where(j == 0, xp_ref[...], before))
    nxt = jnp.where(last, 0.0, jnp.where(j == nsub - 1, xn_ref[...], after))
    fill = jnp.zeros((sm_ref.shape[1] - q - 2 * edge, cur.shape[1]), F32)
    window = jnp.concatenate([prev, cur, nxt, fill], axis=0).astype(BF)
    shifted = _dot(sm_ref[...], window)
    acc = cur * cw_ref[half:half + 1, :] + cb_ref[...]
    blk = 0
    for k in range(SSD_CONV):
        if k != half:
            acc = acc + shifted[blk * q:(blk + 1) * q] * cw_ref[k:k + 1, :]
            blk += 1
    xbc = jax.nn.silu(acc)
    x = xbc[:, :inner]
    bm = xbc[:, inner:inner + SSD_GROUPS * n].astype(BF)
    x_ref[z] = x.astype(BF)
    b_ref[z] = bm
    c_ref[z] = xbc[:, inner + SSD_GROUPS * n:].astype(BF)
    dt = jax.nn.softplus(dt_ref[pl.ds(r0, q), :] + bias_ref[...])
    la = dt * (a_ref[...] * math.log2(math.e))
    row = lax.broadcasted_iota(jnp.int32, (q, q), 0)
    col = lax.broadcasted_iota(jnp.int32, (q, q), 1)
    tril = jnp.where(row >= col, 1.0, 0.0).astype(BF)
    cum = _split3_dot(tril, la)
    tot = cum[q - 1:q, :]
    rev = tot - cum + la
    fwd_lane = lax.broadcasted_iota(jnp.int32, (q, LANES), 1) < nh
    sel = jnp.where(fwd_lane, cum, rev)
    sel_ref[z] = sel
    dt_t = dt.T
    rows_ref[z, 0:2 * nh, :] = (sel - jnp.log2(dt)).T[0:2 * nh]
    rows_ref[z, 2 * nh:3 * nh, :] = jnp.log2(dt_t[0:nh] + dt_t[nh:2 * nh])
    w_all = (jnp.exp2(tot - sel) * dt).astype(BF)
    wexp = _dot(w_all, e_ref[...])
    etot = jnp.exp2(tot)
    lo1 = lax.broadcasted_iota(jnp.int32, (1, LANES), 1) < SSD_HEAD_DIM
    ppg = gw // LANES
    for d in range(2):
        xw = (x * wexp[:, d * inner:(d + 1) * inner]).astype(BF)
        for g in range(SSD_GROUPS):
            upd = lax.dot_general(bm[:, g * n:(g + 1) * n], xw[:, g * gw:(g + 1) * gw], TN_DIMS,
                                  preferred_element_type=F32)
            for j in range(ppg):
                p = g * ppg + j
                h0 = d * nh + 2 * p
                dec = jnp.where(lo1, etot[:, h0:h0 + 1], etot[:, h0 + 1:h0 + 2])
                contrib = upd[:, j * LANES:(j + 1) * LANES]
                if d == 0:
                    state = carry_ref[0, p]
                    st_ref[z, 0, p] = state
                    carry_ref[0, p] = state * dec + contrib
                else:
                    st_ref[z, 1, p] = contrib
                    dec_ref[z, p] = jnp.broadcast_to(dec, dec_ref.shape[2:])


def _ssd_output_phase(z, j, z_ref, d_ref, nw_ref, o_ref, st_ref, x_ref, b_ref, c_ref, sel_ref, rows_ref,
                      dec_ref, carry_ref):
    q = SSD_CHUNK
    r0 = pl.multiple_of(j * q, q)
    nh, n, hpg = SSD_HEADS, SSD_STATE, SSD_HEADS // SSD_GROUPS
    sel = sel_ref[z]
    rows = rows_ref[z]
    row = lax.broadcasted_iota(jnp.int32, (q, q), 0)
    col = lax.broadcasted_iota(jnp.int32, (q, q), 1)
    lower = row > col
    diag = row == col
    lo = lax.broadcasted_iota(jnp.int32, (q, LANES), 1) < SSD_HEAD_DIM
    ys = []
    for g in range(SSD_GROUPS):
        bg = b_ref[z, :, g * n:(g + 1) * n]
        cg = c_ref[z, :, g * n:(g + 1) * n]
        cb = lax.dot_general(cg, bg, NT_DIMS, preferred_element_type=F32)
        cg32 = cg.astype(F32)
        for j in range(hpg // 2):
            p = g * (hpg // 2) + j
            lhs = []
            for h in (2 * p, 2 * p + 1):
                hb = nh + h
                a_f = jnp.broadcast_to(sel[:, h:h + 1], (q, q))
                a_b = jnp.broadcast_to(sel[:, hb:hb + 1], (q, q))
                seg = jnp.where(lower, a_f - rows[h:h + 1, :],
                                jnp.where(diag, rows[2 * nh + h:2 * nh + h + 1, :], a_b - rows[hb:hb + 1, :]))
                m = (cb * jnp.exp2(seg)).astype(BF)
                cef = (cg32 * jnp.exp2(a_f)).astype(BF)
                ceb = (cg32 * jnp.exp2(a_b)).astype(BF)
                lhs.append(jnp.concatenate([m, cef, ceb], axis=1))
            xp = x_ref[z, :, p * LANES:(p + 1) * LANES]
            back = carry_ref[1, p]
            carry_ref[1, p] = back * dec_ref[z, p][0:1] + st_ref[z, 1, p]
            rhs = jnp.concatenate([xp, st_ref[z, 0, p].astype(BF), back.astype(BF)], axis=0)
            out = _dot(jnp.concatenate(lhs, axis=0), rhs)
            ys.append(jnp.where(lo, out[:q], out[q:]) + xp.astype(F32) * d_ref[:, p * LANES:(p + 1) * LANES])
    y = jnp.concatenate(ys, axis=-1)
    gated = y * jax.nn.silu(z_ref[pl.ds(r0, q), :].astype(F32))
    o_ref[pl.ds(r0, q), :] = _rms(gated, nw_ref[...]).astype(BF)


def _ssd_body(xc_ref, xp_ref, xn_ref, dt_ref, z_ref, cw_ref, cb_ref, bias_ref, a_ref, d_ref, nw_ref, e_ref, sm_ref,
              o_ref, st_ref, x_ref, b_ref, c_ref, sel_ref, rows_ref, dec_ref, carry_ref):
    t = pl.program_id(1)
    nc = st_ref.shape[0]
    nsub = xc_ref.shape[0] // SSD_CHUNK
    nb = nc // nsub

    @pl.when(t == 0)
    def _():
        carry_ref[0] = jnp.zeros(carry_ref.shape[1:], F32)

    @pl.when(t == nb)
    def _():
        carry_ref[1] = jnp.zeros(carry_ref.shape[1:], F32)

    @pl.when(t < nb)
    def _():
        def sub(j, carry):
            z = t * nsub + j
            _ssd_states_phase(z, j, z == 0, z == nc - 1, xc_ref, xp_ref, xn_ref, dt_ref, cw_ref, cb_ref,
                              bias_ref, a_ref, e_ref, sm_ref, st_ref, x_ref, b_ref, c_ref, sel_ref, rows_ref,
                              dec_ref, carry_ref)
            return carry
        lax.fori_loop(0, nsub, sub, 0)

    @pl.when(t >= nb)
    def _():
        def sub(i, carry):
            j = nsub - 1 - i
            z = (2 * nb - 1 - t) * nsub + j
            _ssd_output_phase(z, j, z_ref, d_ref, nw_ref, o_ref, st_ref, x_ref, b_ref, c_ref, sel_ref,
                              rows_ref, dec_ref, carry_ref)
            return carry
        lax.fori_loop(0, nsub, sub, 0)


def _conv_shift_table(q, edge, rows):
    half = SSD_CONV // 2
    m = np.zeros(((SSD_CONV - 1) * q, rows), np.float32)
    blk = 0
    for k in range(SSD_CONV):
        if k != half:
            m[blk * q + np.arange(q), edge + np.arange(q) + k - half] = 1.0
            blk += 1
    return jnp.asarray(m, BF)


def _head_expand_table():
    inner = SSD_HEADS * SSD_HEAD_DIM
    e = np.zeros((LANES, 2 * inner), np.float32)
    for h in range(2 * SSD_HEADS):
        e[h, h * SSD_HEAD_DIM:(h + 1) * SSD_HEAD_DIM] = 1.0
    return jnp.asarray(e, BF)


def _ssd(xbc, dt_raw, z, conv_w, conv_b, dt_bias, a_neg, d_exp, ssd_norm, e, batch):
    t, c = xbc.shape
    s = t // batch
    inner = SSD_HEADS * SSD_HEAD_DIM
    gn = SSD_GROUPS * SSD_STATE
    q = SSD_CHUNK
    nc = s // q
    nsub = 4 if nc % 4 == 0 else 1
    nb = nc // nsub
    rows = nsub * q
    edge = 8
    epb = rows // edge
    blk1 = lambda i: jnp.minimum(i, nb - 1)
    blk3 = lambda i: nb - 1 - jnp.maximum(i - nb, 0)
    par = lambda b, i: (e, 0, 0)
    window_rows = 2 * q
    return pl.pallas_call(
        _ssd_body,
        grid=(batch, 2 * nb),
        in_specs=[
            pl.BlockSpec((rows, c), lambda b, i: (b * nb + blk1(i), 0)),
            pl.BlockSpec((edge, c), lambda b, i: (b * nb * epb + jnp.maximum(blk1(i) * epb - 1, 0), 0)),
            pl.BlockSpec((edge, c), lambda b, i: (b * nb * epb + jnp.minimum((blk1(i) + 1) * epb, nb * epb - 1), 0)),
            pl.BlockSpec((rows, LANES), lambda b, i: (b * nb + blk1(i), 0)),
            pl.BlockSpec((rows, inner), lambda b, i: (b * nb + blk3(i), 0)),
            pl.BlockSpec((None, SSD_CONV, c), par),
            pl.BlockSpec((None, 1, c), par),
            pl.BlockSpec((None, 1, LANES), par),
            pl.BlockSpec((None, 1, LANES), par),
            pl.BlockSpec((None, 1, inner), par),
            pl.BlockSpec((None, 1, inner), par),
            _resident((LANES, 2 * inner), lambda b, i: (0, 0)),
            _resident(((SSD_CONV - 1) * q, window_rows), lambda b, i: (0, 0)),
        ],
        out_specs=pl.BlockSpec((rows, inner), lambda b, i: (b * nb + blk3(i), 0)),
        out_shape=jax.ShapeDtypeStruct((t, inner), BF),
        scratch_shapes=[
            pltpu.VMEM((nc, 2, SSD_HEADS // 2, SSD_STATE, LANES), F32),
            pltpu.VMEM((nc, q, inner), BF),
            pltpu.VMEM((nc, q, gn), BF),
            pltpu.VMEM((nc, q, gn), BF),
            pltpu.VMEM((nc, q, LANES), F32),
            pltpu.VMEM((nc, 3 * SSD_HEADS, q), F32),
            pltpu.VMEM((nc, SSD_HEADS // 2, edge, LANES), F32),
            pltpu.VMEM((2, SSD_HEADS // 2, SSD_STATE, LANES), F32),
        ],
        compiler_params=_params("parallel", "arbitrary"),
        name="ssd",
    )(xbc, xbc, xbc, dt_raw, z, conv_w, conv_b, dt_bias, a_neg, d_exp, ssd_norm, _head_expand_table(),
      _conv_shift_table(q, edge, window_rows))


def _lane_fold(x, op):
    out = x[:, :LANES]
    for j in range(1, x.shape[-1] // LANES):
        out = op(out, x[:, j * LANES:(j + 1) * LANES])
    return out


def _mla_body(q_ref, k_ref, v_ref, o_ref, s_ref, m_ref, *, kt):
    nk = k_ref.shape[0]
    _, rb, _ = s_ref.shape
    nrb = q_ref.shape[0] // rb
    nheads = q_ref.shape[1] // LANES

    def score_pass(r, head):
        rows = pl.ds(pl.multiple_of(r * rb, rb), rb)
        sl = slice(head * LANES, (head + 1) * LANES)
        slot = head % 2
        q = q_ref[rows, sl]
        m_acc = None
        for c in range(0, nk, kt):
            s = lax.dot_general(q, k_ref[c:c + kt, sl], NT_DIMS, preferred_element_type=F32)
            s_ref[slot, :, c:c + kt] = s
            mc = _lane_fold(s, jnp.maximum)
            m_acc = mc if m_acc is None else jnp.maximum(m_acc, mc)
        m_ref[slot] = jnp.max(m_acc, axis=-1, keepdims=True)

    def value_pass(head):
        sl = slice(head * LANES, (head + 1) * LANES)
        slot = head % 2
        m = m_ref[slot]
        o = jnp.zeros((rb, LANES), F32)
        for c in range(0, nk, kt):
            p = jnp.exp2(s_ref[slot, :, c:c + kt] - m)
            o = o + _dot(p.astype(BF), v_ref[c:c + kt, sl])
        return o

    def row_block(r, carry):
        lo = lax.broadcasted_iota(jnp.int32, (rb, LANES), 1) < MLA_V
        outs = []
        for head in range(nheads):
            if head + 1 < nheads:
                score_pass(r, head + 1)
            else:
                score_pass(jnp.minimum(r + 1, nrb - 1), 0)
            outs.append(value_pass(head))
        pairs = [jnp.where(lo, o0 / o0[:, MLA_V:MLA_V + 1], o1 / o1[:, 0:1])
                 for o0, o1 in zip(outs[0::2], outs[1::2])]
        o_ref[pl.ds(pl.multiple_of(r * rb, rb), rb), :] = jnp.concatenate(pairs, axis=-1).astype(BF)
        return carry

    score_pass(0, 0)
    lax.fori_loop(0, nrb, row_block, 0)


def _mla(q, k, v, batch):
    t, hq = q.shape
    s = t // batch
    rb = _tile(s, 256)
    nv = MLA_HEADS * MLA_V
    q3, k3, v3 = (a.reshape(batch, s, hq) for a in (q, k, v))
    blk = pl.BlockSpec((None, s, hq), lambda b: (b, 0, 0))
    o = pl.pallas_call(
        functools.partial(_mla_body, kt=_tile(s, 256)),
        grid=(batch,),
        in_specs=[blk, blk, blk],
        out_specs=pl.BlockSpec((None, s, nv), lambda b: (b, 0, 0)),
        out_shape=jax.ShapeDtypeStruct((batch, s, nv), BF),
        scratch_shapes=[pltpu.VMEM((2, rb, s), F32), pltpu.VMEM((2, rb, 1), F32)],
        compiler_params=_params("parallel"),
        name="mla_attn",
    )(q3, k3, v3)
    return o.reshape(t, nv)


def _fnet_fold_body(ha_ref, hm_ref, hx_ref, nw_ref, cc_ref, sc_ref, ec_ref, es_ref):
    tm = ha_ref.shape[0]
    nw = nw_ref[...]
    u_a = _rms(ha_ref[...], nw)
    u_m = _rms(hm_ref[...], nw)
    u_x = _rms(hx_ref[...], nw)[0:1]
    r = lax.broadcasted_iota(jnp.int32, (tm, tm), 0)
    c = lax.broadcasted_iota(jnp.int32, (tm, tm), 1)
    perm = jnp.where(r + c == tm, 1.0, 0.0).astype(BF)
    hi = u_m.astype(BF)
    rest = u_m - hi.astype(F32)
    mid = rest.astype(BF)
    low = (rest - mid.astype(F32)).astype(BF)
    mirror = _dot(perm, hi) + _dot(perm, mid) + _dot(perm, low)
    first = lax.broadcasted_iota(jnp.int32, u_a.shape, 0) == 0
    mirror = jnp.where(first, u_x, mirror)
    ue = (u_a + mirror).astype(BF)
    uo = (u_a - mirror).astype(BF)
    gc = cc_ref.shape[0]
    for g in range(ue.shape[-1] // gc):
        sl = slice(g * gc, (g + 1) * gc)
        ec_ref[:, sl] = _dot(ue[:, sl], cc_ref[...]).astype(BF)
        es_ref[:, sl] = _dot(uo[:, sl], sc_ref[...]).astype(BF)


def _fnet_seq_body(h_ref, hh_ref, nw_ref, cc_ref, cs_ref, ss_ref, ec_ref, es_ref, w_ref, o_ref, *, scale):
    ts = h_ref.shape[0]
    rows = pl.ds(pl.multiple_of(pl.program_id(1) * ts, ts), ts)
    y = _dot(cs_ref[rows, :], ec_ref[...]) - _dot(ss_ref[rows, :], es_ref[...])
    u_h = _rms(hh_ref[...], nw_ref[...]).astype(BF)
    gc = cc_ref.shape[0]
    x_h = jnp.concatenate([_dot(u_h[:, g * gc:(g + 1) * gc], cc_ref[...])
                           for g in range(u_h.shape[-1] // gc)], axis=-1)[0:1]
    odd = lax.broadcasted_iota(jnp.int32, (ts, 1), 0) % 2 == 1
    y = y + jnp.where(odd, -1.0, 1.0) * x_h
    o_ref[...] = h_ref[...] + _dot((y * scale).astype(BF), w_ref[...])


def _dft_tables(n, fold):
    j = np.arange(n)[:, None]
    k = np.arange(n // 2 if fold else n)[None, :]
    ang = ((j * k) % n) * (2.0 * np.pi / n)
    cos, sin = np.cos(ang), np.sin(ang)
    if fold:
        cos[:, 0] = 0.5
    return jnp.asarray(cos, BF), jnp.asarray(sin, BF)


def _fnet(h, norm_w, w_out, layer, o, batch, tables):
    t, d = h.shape
    s = t // batch
    cc, sc, cs, ss = tables
    gc = cc.shape[0]
    sub = 8
    tm = _tile(s // 2, 256)
    nt, nf = s // tm, s // 2 // tm
    nwspec = pl.BlockSpec((None, 1, d), lambda b, i: (layer, 0, 0))
    table = _resident((gc, gc), lambda b, i: (0, 0))
    ec, es = pl.pallas_call(
        _fnet_fold_body,
        grid=(batch, nf),
        in_specs=[
            pl.BlockSpec((tm, d), lambda b, i: (b * nt + i, 0)),
            pl.BlockSpec((tm, d), lambda b, i: (b * nt + nt - 1 - i, 0)),
            pl.BlockSpec((sub, d), lambda b, i: (b * (s // sub) + ((nt - i) % nt) * (tm // sub), 0)),
            nwspec, table, table,
        ],
        out_specs=[pl.BlockSpec((tm, d), lambda b, i: (b * nf + i, 0))] * 2,
        out_shape=[jax.ShapeDtypeStruct((t // 2, d), BF)] * 2,
        compiler_params=_params("parallel", "parallel"),
        name="fnet_fold_channel_dft",
    )(h, h, h, norm_w, cc, sc)
    ts = _tile(s, 512)
    ns = s // ts
    return pl.pallas_call(
        functools.partial(_fnet_seq_body, scale=(s * gc) ** -0.5),
        grid=(batch, ns),
        in_specs=[
            pl.BlockSpec((ts, d), lambda b, i: (b * ns + i, 0)),
            pl.BlockSpec((sub, d), lambda b, i: (b * (s // sub) + s // 2 // sub, 0)),
            nwspec, table,
            _resident((s, s // 2), lambda b, i: (0, 0)),
            _resident((s, s // 2), lambda b, i: (0, 0)),
            pl.BlockSpec((s // 2, d), lambda b, i: (b, 0)),
            pl.BlockSpec((s // 2, d), lambda b, i: (b, 0)),
            _resident((None, d, d), lambda b, i: (o, 0, 0)),
        ],
        out_specs=pl.BlockSpec((ts, d), lambda b, i: (b * ns + i, 0)),
        out_shape=jax.ShapeDtypeStruct((t, d), F32),
        compiler_params=_params("parallel", "arbitrary"),
        name="fnet_seq_dft",
    )(h, h, norm_w, cc, cs, ss, ec, es, w_out)


def _mixer_weights(w_in, w_uq, w_ukv, dt_bias, a_log, ssd_d):
    ne, d, _ = w_in.shape
    inner = SSD_HEADS * SSD_HEAD_DIM
    conv_ch = inner + 2 * SSD_GROUPS * SSD_STATE
    o_z, o_xbc = 0, inner
    o_dt = o_xbc + conv_ch
    o_cq = o_dt + 2 * SSD_HEADS
    o_ckv = o_cq + MLA_Q_RANK
    o_kr = o_ckv + MLA_KV_RANK
    half = MLA_ROPE // 2
    pad = LANES - MLA_NOPE - MLA_ROPE
    zeros = lambda *s: jnp.zeros(s, w_in.dtype)
    w_dt = jnp.concatenate([w_in[:, :, o_dt:o_cq], zeros(ne, d, LANES - 2 * SSD_HEADS)], axis=-1)
    kr1 = w_in[:, :, o_kr:o_kr + half]
    kr2 = w_in[:, :, o_kr + half:o_kr + MLA_ROPE]
    kr_a = jnp.concatenate([zeros(ne, d, MLA_NOPE), kr1, kr2, zeros(ne, d, pad)], axis=-1)
    kr_b = jnp.concatenate([zeros(ne, d, MLA_NOPE), -kr2, kr1, zeros(ne, d, pad)], axis=-1)
    pieces = [w_in[:, :, o_z:o_xbc], w_in[:, :, o_xbc:o_dt], w_dt, w_in[:, :, o_cq:o_ckv],
              w_in[:, :, o_ckv:o_kr], jnp.concatenate([kr_a, kr_b], axis=-1)]
    cols, c = [], 0
    for p in pieces:
        cols.append((c, c + p.shape[-1]))
        c += p.shape[-1]
    w_all = jnp.concatenate(pieces, axis=-1).astype(BF)

    uq = w_uq.reshape(ne, MLA_Q_RANK, MLA_HEADS, MLA_NOPE + MLA_ROPE)
    q_nope, q1, q2 = uq[..., :MLA_NOPE], uq[..., MLA_NOPE:MLA_NOPE + half], uq[..., MLA_NOPE + half:]
    zq = lambda n: jnp.zeros((ne, MLA_Q_RANK, MLA_HEADS, n), w_uq.dtype)
    hq = MLA_HEADS * LANES
    wqa = jnp.concatenate([q_nope, q1, q2, zq(pad)], axis=-1).reshape(ne, MLA_Q_RANK, hq).astype(BF)
    wqb = jnp.concatenate([zq(MLA_NOPE), -q2, q1, zq(pad)], axis=-1).reshape(ne, MLA_Q_RANK, hq).astype(BF)

    ukv = w_ukv.reshape(ne, MLA_KV_RANK, MLA_HEADS, MLA_NOPE + MLA_V)
    zkv = lambda *s: jnp.zeros((ne, MLA_KV_RANK) + s, w_ukv.dtype)
    wkn = jnp.concatenate([ukv[..., :MLA_NOPE], zkv(MLA_HEADS, LANES - MLA_NOPE)], axis=-1)
    wkn = wkn.reshape(ne, MLA_KV_RANK, hq).astype(BF)
    vv = ukv[..., MLA_NOPE:].reshape(ne, MLA_KV_RANK, MLA_HEADS // 2, 2, MLA_V)
    zv = zkv(MLA_HEADS // 2, MLA_V)
    wv = jnp.stack([jnp.concatenate([vv[:, :, :, 0], zv], axis=-1),
                    jnp.concatenate([zv, vv[:, :, :, 1]], axis=-1)], axis=3)
    wv = wv.reshape(ne, MLA_KV_RANK, hq).astype(BF)

    padl = lambda a: jnp.concatenate([a, jnp.zeros((ne, LANES - a.shape[-1]), a.dtype)], axis=-1)[:, None, :]
    bias = padl(dt_bias.reshape(ne, 2 * SSD_HEADS))
    a_neg = padl(-jnp.exp(a_log.reshape(ne, 2 * SSD_HEADS)))
    d_exp = jnp.repeat(ssd_d, SSD_HEAD_DIM, axis=-1)[:, None, :]
    return w_all, tuple(cols), wqa, wqb, wkn, wv, bias, a_neg, d_exp


def _rope_tables(positions):
    half = MLA_ROPE // 2
    inv = 1.0 / (ROPE_THETA ** (jnp.arange(0, MLA_ROPE, 2, dtype=F32) / MLA_ROPE))
    ang = positions.astype(F32).reshape(-1, 1) * inv
    cos, sin = jnp.cos(ang), jnp.sin(ang)
    t = cos.shape[0]
    pad = LANES - MLA_NOPE - MLA_ROPE
    cos_t = jnp.concatenate([jnp.ones((t, MLA_NOPE), F32), cos, cos, jnp.zeros((t, pad), F32)], axis=-1)
    sin_t = jnp.concatenate([jnp.zeros((t, MLA_NOPE), F32), sin, sin, jnp.zeros((t, pad), F32)], axis=-1)
    return cos_t, sin_t


def kernel(x, mem, positions, mem_norm, final_norm, ffn1_norm, ffn1_w_gu, ffn1_w_down, mix_norm, xa_norm,
           xa_wq, xa_wkv, xa_wo, ffn2_norm, ffn2_w_gu, ffn2_w_down, w_in, conv_w, conv_b, dt_bias, a_log,
           ssd_d, ssd_norm, q_norm, w_uq, kv_norm, w_ukv, w_out, fnet_w_out):
    batch, seq, d = x.shape
    depth = ffn1_norm.shape[0]
    t = batch * seq
    bf = lambda a: a.astype(BF)
    row3 = lambda a: a[:, None, :]

    kv = _kvproj(mem.reshape(-1, d), mem_norm[None, :], bf(xa_wkv)).reshape(depth, batch, mem.shape[1], 2 * d)
    w_all, cols, wqa, wqb, wkn, wv, bias, a_neg, d_exp = _mixer_weights(w_in, w_uq, w_ukv, dt_bias, a_log, ssd_d)
    cos_t, sin_t = _rope_tables(positions)
    gc = d // FNET_GROUPS
    tables = _dft_tables(gc, fold=False) + _dft_tables(seq, fold=True)
    f1n, f2n, mxn, xan = row3(ffn1_norm), row3(ffn2_norm), row3(mix_norm), row3(xa_norm)
    f1gu, f1d, f2gu, f2d = bf(ffn1_w_gu), bf(ffn1_w_down), bf(ffn2_w_gu), bf(ffn2_w_down)
    wq, wo, w_mix_out, w_fnet = bf(xa_wq), bf(xa_wo), bf(w_out), bf(fnet_w_out)
    ssd_nw, qn, kvn = row3(ssd_norm), row3(q_norm), row3(kv_norm)

    h = x.reshape(t, d)
    for layer in range(depth):
        h = _ffn(h, f1n, f1gu, f1d, layer)
        if layer % 2 == 0:
            e = layer // 2
            z, xbc, dt_raw, q, k, v = _inproj(h, mxn, w_all, cols, qn, kvn, wqa, wqb, wkn, wv,
                                              cos_t, sin_t, layer, e)
            y_ssd = _ssd(xbc, dt_raw, z, conv_w, row3(conv_b), bias, a_neg, d_exp, ssd_nw, e, batch)
            o_mla = _mla(q, k, v, batch)
            mix = (y_ssd, o_mla, w_mix_out, e)
        else:
            h = _fnet(h, mxn, w_fnet, layer, layer // 2, batch, tables)
            mix = None
        h = _xattn(h, xan, wq, kv, wo, layer, batch, mix=mix)
        h = _ffn(h, f2n, f2gu, f2d, layer, final_w=final_norm[None, :] if layer == depth - 1 else None)
    return h.reshape(batch, seq, d)
```

```python
import functools
import math

import numpy as np
import jax
import jax.numpy as jnp
from jax import lax
from jax.experimental import pallas as pl
from jax.experimental.pallas import tpu as pltpu

EPS = 1e-6
BF = jnp.bfloat16
F32 = jnp.float32

V7X_VMEM_BYTES = 64 * 1024 * 1024
VMEM_LIMIT = V7X_VMEM_BYTES - 8 * 1024 * 1024
LANES = 128

SSD_HEADS = 16
SSD_HEAD_DIM = 64
SSD_GROUPS = 2
SSD_STATE = 128
SSD_CONV = 5
SSD_CHUNK = 128
MLA_HEADS = 8
MLA_Q_RANK = 512
MLA_KV_RANK = 256
MLA_NOPE = 64
MLA_ROPE = 32
MLA_V = 64
ROPE_THETA = 10000.0
FNET_GROUPS = 4
XA_HEADS = 4

NT_DIMS = (((1,), (1,)), ((), ()))
TN_DIMS = (((0,), (0,)), ((), ()))


def _params(*sem):
    return pltpu.CompilerParams(dimension_semantics=sem, vmem_limit_bytes=VMEM_LIMIT)


def _resident(shape, index_map):
    return pl.BlockSpec(shape, index_map, pipeline_mode=pl.Buffered(1))


def _rms(x, w):
    return x * lax.rsqrt(jnp.mean(x * x, axis=-1, keepdims=True) + EPS) * w


def _dot(a, b):
    return jnp.dot(a, b, preferred_element_type=F32)


def _tile(n, pref):
    t = min(n, pref)
    assert n % t == 0, (n, t)
    return t


def _ffn_body(*refs, chunks, final):
    if final:
        h_ref, nw_ref, wg_ref, wu_ref, wd_ref, fw_ref, o_ref = refs
    else:
        h_ref, nw_ref, wg_ref, wu_ref, wd_ref, o_ref = refs
    h = h_ref[...]
    xn = _rms(h, nw_ref[...]).astype(BF)
    acc = jnp.zeros(h.shape, F32)
    for a, b in chunks:
        g = _dot(xn, wg_ref[:, a:b])
        u = _dot(xn, wu_ref[:, a:b])
        act = (jax.nn.silu(g) * u).astype(BF)
        acc = acc + _dot(act, wd_ref[a:b, :])
    out = h + 0.5 * acc
    if final:
        out = _rms(out, fw_ref[...])
    o_ref[...] = out


def _ffn(h, norm_w, w_gu, w_down, layer, final_w=None):
    t, d = h.shape
    f = w_down.shape[1]
    tm = _tile(t, 1024)
    step = 768
    chunks = tuple((a, min(a + step, f)) for a in range(0, f, step))
    in_specs = [
        pl.BlockSpec((tm, d), lambda i: (i, 0)),
        pl.BlockSpec((None, 1, d), lambda i: (layer, 0, 0)),
        _resident((None, d, f), lambda i: (layer, 0, 0)),
        _resident((None, d, f), lambda i: (layer, 0, 1)),
        _resident((None, f, d), lambda i: (layer, 0, 0)),
    ]
    args = [h, norm_w, w_gu, w_gu, w_down]
    if final_w is not None:
        in_specs.append(pl.BlockSpec((1, d), lambda i: (0, 0)))
        args.append(final_w)
    return pl.pallas_call(
        functools.partial(_ffn_body, chunks=chunks, final=final_w is not None),
        grid=(t // tm,),
        in_specs=in_specs,
        out_specs=pl.BlockSpec((tm, d), lambda i: (i, 0)),
        out_shape=jax.ShapeDtypeStruct((t, d), F32),
        compiler_params=_params("parallel"),
        name="ffn",
    )(*args)


def _kvproj_body(m_ref, nw_ref, w_ref, o_ref):
    mn = _rms(m_ref[...], nw_ref[...]).astype(BF)
    o_ref[...] = _dot(mn, w_ref[...]).astype(BF)


def _kvproj(mem2d, mem_norm, wkv):
    n, d = mem2d.shape
    nl, _, d2 = wkv.shape
    tm = _tile(n, 512)
    return pl.pallas_call(
        _kvproj_body,
        grid=(nl, n // tm),
        in_specs=[
            pl.BlockSpec((tm, d), lambda l, i: (i, 0)),
            pl.BlockSpec((1, d), lambda l, i: (0, 0)),
            pl.BlockSpec((None, d, d2), lambda l, i: (l, 0, 0)),
        ],
        out_specs=pl.BlockSpec((None, tm, d2), lambda l, i: (l, i, 0)),
        out_shape=jax.ShapeDtypeStruct((nl, n, d2), BF),
        compiler_params=_params("parallel", "parallel"),
        name="xa_kvproj",
    )(mem2d, mem_norm, wkv)


def _xa_body(*refs, heads, mixed):
    if mixed:
        h_ref, nw_ref, wq_ref, k_ref, v_ref, wo_ref, y_ref, a_ref, wy_ref, wa_ref, o_ref = refs
        h = h_ref[...] + _dot(y_ref[...], wy_ref[...]) + _dot(a_ref[...], wa_ref[...])
    else:
        h_ref, nw_ref, wq_ref, k_ref, v_ref, wo_ref, o_ref = refs
        h = h_ref[...]
    hn = _rms(h, nw_ref[...]).astype(BF)
    dh = h.shape[-1] // heads
    q = (_dot(hn, wq_ref[...]) * (math.log2(math.e) * dh ** -0.5)).astype(BF)
    outs = []
    for i in range(heads):
        sl = slice(i * dh, (i + 1) * dh)
        s = lax.dot_general(q[:, sl], k_ref[:, sl], NT_DIMS, preferred_element_type=F32)
        p = jnp.exp2(s - jnp.max(s, axis=-1, keepdims=True))
        l = jnp.sum(p, axis=-1, keepdims=True)
        outs.append((_dot(p.astype(BF), v_ref[:, sl]) / l).astype(BF))
    o = jnp.concatenate(outs, axis=-1)
    o_ref[...] = h + _dot(o, wo_ref[...])


def _xattn(h, norm_w, wq, kv, wo, layer, batch, mix=None):
    t, d = h.shape
    s = t // batch
    nm = kv.shape[2]
    tm = _tile(s, 512)
    ns = s // tm
    row = lambda b, i: (b * ns + i, 0)
    in_specs = [
        pl.BlockSpec((tm, d), row),
        pl.BlockSpec((None, 1, d), lambda b, i: (layer, 0, 0)),
        _resident((None, d, d), lambda b, i: (layer, 0, 0)),
        pl.BlockSpec((None, None, nm, d), lambda b, i: (layer, b, 0, 0)),
        pl.BlockSpec((None, None, nm, d), lambda b, i: (layer, b, 0, 1)),
        _resident((None, d, d), lambda b, i: (layer, 0, 0)),
    ]
    args = [h, norm_w, wq, kv, kv, wo]
    if mix is not None:
        y_ssd, o_mla, w_out, e = mix
        ny, na = y_ssd.shape[1], o_mla.shape[1]
        assert ny % na == 0
        in_specs += [
            pl.BlockSpec((tm, ny), row),
            pl.BlockSpec((tm, na), row),
            _resident((None, ny, d), lambda b, i: (e, 0, 0)),
            _resident((None, na, d), lambda b, i: (e, ny // na, 0)),
        ]
        args += [y_ssd, o_mla, w_out, w_out]
    return pl.pallas_call(
        functools.partial(_xa_body, heads=XA_HEADS, mixed=mix is not None),
        grid=(batch, ns),
        in_specs=in_specs,
        out_specs=pl.BlockSpec((tm, d), row),
        out_shape=jax.ShapeDtypeStruct((t, d), F32),
        compiler_params=_params("parallel", "parallel"),
        name="xattn",
    )(*args)


def _inproj_body(h_ref, nw_ref, w_ref, qn_ref, kvn_ref, wqa_ref, wqb_ref, wkn_ref, wv_ref,
                 cos_ref, sin_ref, z_ref, xbc_ref, dt_ref, q_ref, k_ref, v_ref, *, cols, scale):
    c_z, c_xbc, c_dt, c_cq, c_ckv, c_kr = cols
    u = _rms(h_ref[...], nw_ref[...]).astype(BF)
    z_ref[...] = _dot(u, w_ref[:, c_z[0]:c_z[1]]).astype(BF)
    xbc_ref[...] = _dot(u, w_ref[:, c_xbc[0]:c_xbc[1]])
    dt_ref[...] = _dot(u, w_ref[:, c_dt[0]:c_dt[1]])
    cqn = _rms(_dot(u, w_ref[:, c_cq[0]:c_cq[1]]), qn_ref[...]).astype(BF)
    ckvn = _rms(_dot(u, w_ref[:, c_ckv[0]:c_ckv[1]]), kvn_ref[...]).astype(BF)
    kr = _dot(u, w_ref[:, c_kr[0]:c_kr[1]])
    cos_t = cos_ref[...]
    sin_t = sin_ref[...]
    kp = kr[:, :LANES] * cos_t + kr[:, LANES:] * sin_t
    qa = _dot(cqn, wqa_ref[...])
    qb = _dot(cqn, wqb_ref[...])
    kn = _dot(ckvn, wkn_ref[...])
    for i in range(qa.shape[-1] // LANES):
        sl = slice(i * LANES, (i + 1) * LANES)
        q_ref[:, sl] = ((qa[:, sl] * cos_t + qb[:, sl] * sin_t) * scale).astype(BF)
        k_ref[:, sl] = (kn[:, sl] + kp).astype(BF)
    col = lax.broadcasted_iota(jnp.int32, (1, v_ref.shape[-1]), 1) % (2 * LANES)
    ones = jnp.where((col == MLA_V) | (col == LANES), 1.0, 0.0)
    v_ref[...] = (_dot(ckvn, wv_ref[...]) + ones).astype(BF)


def _inproj(h, norm_w, w_all, cols, q_norm, kv_norm, wqa, wqb, wkn, wv, cos_t, sin_t, layer, e):
    t, d = h.shape
    tm = _tile(t, 512)
    wc = w_all.shape[-1]
    n_z = cols[0][1] - cols[0][0]
    n_xbc = cols[1][1] - cols[1][0]
    hq = wqa.shape[-1]
    row = lambda i: (i, 0)
    return pl.pallas_call(
        functools.partial(_inproj_body, cols=cols, scale=math.log2(math.e) * (MLA_NOPE + MLA_ROPE) ** -0.5),
        grid=(t // tm,),
        in_specs=[
            pl.BlockSpec((tm, d), row),
            pl.BlockSpec((None, 1, d), lambda i: (layer, 0, 0)),
            _resident((None, d, wc), lambda i: (e, 0, 0)),
            pl.BlockSpec((None, 1, MLA_Q_RANK), lambda i: (e, 0, 0)),
            pl.BlockSpec((None, 1, MLA_KV_RANK), lambda i: (e, 0, 0)),
            _resident((None, MLA_Q_RANK, hq), lambda i: (e, 0, 0)),
            _resident((None, MLA_Q_RANK, hq), lambda i: (e, 0, 0)),
            _resident((None, MLA_KV_RANK, hq), lambda i: (e, 0, 0)),
            _resident((None, MLA_KV_RANK, hq), lambda i: (e, 0, 0)),
            pl.BlockSpec((tm, LANES), row),
            pl.BlockSpec((tm, LANES), row),
        ],
        out_specs=[
            pl.BlockSpec((tm, n_z), row),
            pl.BlockSpec((tm, n_xbc), row),
            pl.BlockSpec((tm, LANES), row),
            pl.BlockSpec((tm, hq), row),
            pl.BlockSpec((tm, hq), row),
            pl.BlockSpec((tm, hq), row),
        ],
        out_shape=[
            jax.ShapeDtypeStruct((t, n_z), BF),
            jax.ShapeDtypeStruct((t, n_xbc), F32),
            jax.ShapeDtypeStruct((t, LANES), F32),
            jax.ShapeDtypeStruct((t, hq), BF),
            jax.ShapeDtypeStruct((t, hq), BF),
            jax.ShapeDtypeStruct((t, hq), BF),
        ],
        compiler_params=_params("parallel"),
        name="mix_inproj",
    )(h, norm_w, w_all, q_norm, kv_norm, wqa, wqb, wkn, wv, cos_t, sin_t)


def _split3_dot(a_bf, x):
    hi = x.astype(BF)
    r1 = x - hi.astype(F32)
    mid = r1.astype(BF)
    low = (r1 - mid.astype(F32)).astype(BF)
    return _dot(a_bf, hi) + _dot(a_bf, mid) + _dot(a_bf, low)


def _ssd_states_phase(z, j, first, last, xc_ref, xp_ref, xn_ref, dt_ref, cw_ref, cb_ref, bias_ref, a_ref, e_ref,
                      sm_ref, st_ref, x_ref, b_ref, c_ref, sel_ref, rows_ref, dec_ref, carry_ref):
    q = SSD_CHUNK
    nh, n = SSD_HEADS, SSD_STATE
    inner = nh * SSD_HEAD_DIM
    gw = inner // SSD_GROUPS
    half = SSD_CONV // 2
    edge = xp_ref.shape[0]
    nsub = xc_ref.shape[0] // q
    r0 = pl.multiple_of(j * q, q)
    cur = xc_ref[pl.ds(r0, q), :]
    before = xc_ref[pl.ds(pl.multiple_of(jnp.maximum(r0 - edge, 0), edge), edge), :]
    after = xc_ref[pl.ds(pl.multiple_of(jnp.minimum(r0 + q, (nsub - 1) * q + q - edge), edge), edge), :]
    prev = jnp.where(first, 0.0, jnp.where(j == 0, xp_ref[...], before))
    nxt = jnp.where(last, 0.0, jnp.where(j == nsub - 1, xn_ref[...], after))
    fill = jnp.zeros((sm_ref.shape[1] - q - 2 * edge, cur.shape[1]), F32)
    window = jnp.concatenate([prev, cur, nxt, fill], axis=0).astype(BF)
    shifted = _dot(sm_ref[...], window)
    acc = cur * cw_ref[half:half + 1, :] + cb_ref[...]
    blk = 0
    for k in range(SSD_CONV):
        if k != half:
            acc = acc + shifted[blk * q:(blk + 1) * q] * cw_ref[k:k + 1, :]
            blk += 1
    xbc = jax.nn.silu(acc)
    x = xbc[:, :inner]
    bm = xbc[:, inner:inner + SSD_GROUPS * n].astype(BF)
    x_ref[z] = x.astype(BF)
    b_ref[z] = bm
    c_ref[z] = xbc[:, inner + SSD_GROUPS * n:].astype(BF)
    dt = jax.nn.softplus(dt_ref[pl.ds(r0, q), :] + bias_ref[...])
    la = dt * (a_ref[...] * math.log2(math.e))
    row = lax.broadcasted_iota(jnp.int32, (q, q), 0)
    col = lax.broadcasted_iota(jnp.int32, (q, q), 1)
    tril = jnp.where(row >= col, 1.0, 0.0).astype(BF)
    cum = _split3_dot(tril, la)
    tot = cum[q - 1:q, :]
    rev = tot - cum + la
    fwd_lane = lax.broadcasted_iota(jnp.int32, (q, LANES), 1) < nh
    sel = jnp.where(fwd_lane, cum, rev)
    sel_ref[z] = sel
    dt_t = dt.T
    rows_ref[z, 0:2 * nh, :] = (sel - jnp.log2(dt)).T[0:2 * nh]
    rows_ref[z, 2 * nh:3 * nh, :] = jnp.log2(dt_t[0:nh] + dt_t[nh:2 * nh])
    w_all = (jnp.exp2(tot - sel) * dt).astype(BF)
    wexp = _dot(w_all, e_ref[...])
    etot = jnp.exp2(tot)
    lo1 = lax.broadcasted_iota(jnp.int32, (1, LANES), 1) < SSD_HEAD_DIM
    ppg = gw // LANES
    for d in range(2):
        xw = (x * wexp[:, d * inner:(d + 1) * inner]).astype(BF)
        for g in range(SSD_GROUPS):
            upd = lax.dot_general(bm[:, g * n:(g + 1) * n], xw[:, g * gw:(g + 1) * gw], TN_DIMS,
                                  preferred_element_type=F32)
            for j in range(ppg):
                p = g * ppg + j
                h0 = d * nh + 2 * p
                dec = jnp.where(lo1, etot[:, h0:h0 + 1], etot[:, h0 + 1:h0 + 2])
                contrib = upd[:, j * LANES:(j + 1) * LANES]
                if d == 0:
                    state = carry_ref[0, p]
                    st_ref[z, 0, p] = state
                    carry_ref[0, p] = state * dec + contrib
                else:
                    st_ref[z, 1, p] = contrib
                    dec_ref[z, p] = jnp.broadcast_to(dec, dec_ref.shape[2:])


def _ssd_output_phase(z, j, z_ref, d_ref, nw_ref, o_ref, st_ref, x_ref, b_ref, c_ref, sel_ref, rows_ref,
                      dec_ref, carry_ref):
    q = SSD_CHUNK
    r0 = pl.multiple_of(j * q, q)
    nh, n, hpg = SSD_HEADS, SSD_STATE, SSD_HEADS // SSD_GROUPS
    sel = sel_ref[z]
    rows = rows_ref[z]
    row = lax.broadcasted_iota(jnp.int32, (q, q), 0)
    col = lax.broadcasted_iota(jnp.int32, (q, q), 1)
    lower = row > col
    diag = row == col
    lo = lax.broadcasted_iota(jnp.int32, (q, LANES), 1) < SSD_HEAD_DIM
    ys = []
    for g in range(SSD_GROUPS):
        bg = b_ref[z, :, g * n:(g + 1) * n]
        cg = c_ref[z, :, g * n:(g + 1) * n]
        cb = lax.dot_general(cg, bg, NT_DIMS, preferred_element_type=F32)
        cg32 = cg.astype(F32)
        for j in range(hpg // 2):
            p = g * (hpg // 2) + j
            lhs = []
            for h in (2 * p, 2 * p + 1):
                hb = nh + h
                a_f = jnp.broadcast_to(sel[:, h:h + 1], (q, q))
                a_b = jnp.broadcast_to(sel[:, hb:hb + 1], (q, q))
                seg = jnp.where(lower, a_f - rows[h:h + 1, :],
                                jnp.where(diag, rows[2 * nh + h:2 * nh + h + 1, :], a_b - rows[hb:hb + 1, :]))
                m = (cb * jnp.exp2(seg)).astype(BF)
                cef = (cg32 * jnp.exp2(a_f)).astype(BF)
                ceb = (cg32 * jnp.exp2(a_b)).astype(BF)
                lhs.append(jnp.concatenate([m, cef, ceb], axis=1))
            xp = x_ref[z, :, p * LANES:(p + 1) * LANES]
            back = carry_ref[1, p]
            carry_ref[1, p] = back * dec_ref[z, p][0:1] + st_ref[z, 1, p]
            rhs = jnp.concatenate([xp, st_ref[z, 0, p].astype(BF), back.astype(BF)], axis=0)
            out = _dot(jnp.concatenate(lhs, axis=0), rhs)
            ys.append(jnp.where(lo, out[:q], out[q:]) + xp.astype(F32) * d_ref[:, p * LANES:(p + 1) * LANES])
    y = jnp.concatenate(ys, axis=-1)
    gated = y * jax.nn.silu(z_ref[pl.ds(r0, q), :].astype(F32))
    o_ref[pl.ds(r0, q), :] = _rms(gated, nw_ref[...]).astype(BF)


def _ssd_body(xc_ref, xp_ref, xn_ref, dt_ref, z_ref, cw_ref, cb_ref, bias_ref, a_ref, d_ref, nw_ref, e_ref, sm_ref,
              o_ref, st_ref, x_ref, b_ref, c_ref, sel_ref, rows_ref, dec_ref, carry_ref):
    t = pl.program_id(1)
    nc = st_ref.shape[0]
    nsub = xc_ref.shape[0] // SSD_CHUNK
    nb = nc // nsub

    @pl.when(t == 0)
    def _():
        carry_ref[0] = jnp.zeros(carry_ref.shape[1:], F32)

    @pl.when(t == nb)
    def _():
        carry_ref[1] = jnp.zeros(carry_ref.shape[1:], F32)

    @pl.when(t < nb)
    def _():
        def sub(j, carry):
            z = t * nsub + j
            _ssd_states_phase(z, j, z == 0, z == nc - 1, xc_ref, xp_ref, xn_ref, dt_ref, cw_ref, cb_ref,
                              bias_ref, a_ref, e_ref, sm_ref, st_ref, x_ref, b_ref, c_ref, sel_ref, rows_ref,
                              dec_ref, carry_ref)
            return carry
        lax.fori_loop(0, nsub, sub, 0)

    @pl.when(t >= nb)
    def _():
        def sub(i, carry):
            j = nsub - 1 - i
            z = (2 * nb - 1 - t) * nsub + j
            _ssd_output_phase(z, j, z_ref, d_ref, nw_ref, o_ref, st_ref, x_ref, b_ref, c_ref, sel_ref,
                              rows_ref, dec_ref, carry_ref)
            return carry
        lax.fori_loop(0, nsub, sub, 0)


def _conv_shift_table(q, edge, rows):
    half = SSD_CONV // 2
    m = np.zeros(((SSD_CONV - 1) * q, rows), np.float32)
    blk = 0
    for k in range(SSD_CONV):
        if k != half:
            m[blk * q + np.arange(q), edge + np.arange(q) + k - half] = 1.0
            blk += 1
    return jnp.asarray(m, BF)


def _head_expand_table():
    inner = SSD_HEADS * SSD_HEAD_DIM
    e = np.zeros((LANES, 2 * inner), np.float32)
    for h in range(2 * SSD_HEADS):
        e[h, h * SSD_HEAD_DIM:(h + 1) * SSD_HEAD_DIM] = 1.0
    return jnp.asarray(e, BF)


def _ssd(xbc, dt_raw, z, conv_w, conv_b, dt_bias, a_neg, d_exp, ssd_norm, e, batch):
    t, c = xbc.shape
    s = t // batch
    inner = SSD_HEADS * SSD_HEAD_DIM
    gn = SSD_GROUPS * SSD_STATE
    q = SSD_CHUNK
    nc = s // q
    nsub = 4 if nc % 4 == 0 else 1
    nb = nc // nsub
    rows = nsub * q
    edge = 8
    epb = rows // edge
    blk1 = lambda i: jnp.minimum(i, nb - 1)
    blk3 = lambda i: nb - 1 - jnp.maximum(i - nb, 0)
    par = lambda b, i: (e, 0, 0)
    window_rows = 2 * q
    return pl.pallas_call(
        _ssd_body,
        grid=(batch, 2 * nb),
        in_specs=[
            pl.BlockSpec((rows, c), lambda b, i: (b * nb + blk1(i), 0)),
            pl.BlockSpec((edge, c), lambda b, i: (b * nb * epb + jnp.maximum(blk1(i) * epb - 1, 0), 0)),
            pl.BlockSpec((edge, c), lambda b, i: (b * nb * epb + jnp.minimum((blk1(i) + 1) * epb, nb * epb - 1), 0)),
            pl.BlockSpec((rows, LANES), lambda b, i: (b * nb + blk1(i), 0)),
            pl.BlockSpec((rows, inner), lambda b, i: (b * nb + blk3(i), 0)),
            pl.BlockSpec((None, SSD_CONV, c), par),
            pl.BlockSpec((None, 1, c), par),
            pl.BlockSpec((None, 1, LANES), par),
            pl.BlockSpec((None, 1, LANES), par),
            pl.BlockSpec((None, 1, inner), par),
            pl.BlockSpec((None, 1, inner), par),
            _resident((LANES, 2 * inner), lambda b, i: (0, 0)),
            _resident(((SSD_CONV - 1) * q, window_rows), lambda b, i: (0, 0)),
        ],
        out_specs=pl.BlockSpec((rows, inner), lambda b, i: (b * nb + blk3(i), 0)),
        out_shape=jax.ShapeDtypeStruct((t, inner), BF),
        scratch_shapes=[
            pltpu.VMEM((nc, 2, SSD_HEADS // 2, SSD_STATE, LANES), F32),
            pltpu.VMEM((nc, q, inner), BF),
            pltpu.VMEM((nc, q, gn), BF),
            pltpu.VMEM((nc, q, gn), BF),
            pltpu.VMEM((nc, q, LANES), F32),
            pltpu.VMEM((nc, 3 * SSD_HEADS, q), F32),
            pltpu.VMEM((nc, SSD_HEADS // 2, edge, LANES), F32),
            pltpu.VMEM((2, SSD_HEADS // 2, SSD_STATE, LANES), F32),
        ],
        compiler_params=_params("parallel", "arbitrary"),
        name="ssd",
    )(xbc, xbc, xbc, dt_raw, z, conv_w, conv_b, dt_bias, a_neg, d_exp, ssd_norm, _head_expand_table(),
      _conv_shift_table(q, edge, window_rows))


def _lane_fold(x, op):
    out = x[:, :LANES]
    for j in range(1, x.shape[-1] // LANES):
        out = op(out, x[:, j * LANES:(j + 1) * LANES])
    return out


def _mla_body(q_ref, k_ref, v_ref, o_ref, s_ref, m_ref, *, kt):
    nk = k_ref.shape[0]
    _, rb, _ = s_ref.shape
    nrb = q_ref.shape[0] // rb
    nheads = q_ref.shape[1] // LANES

    def score_pass(r, head):
        rows = pl.ds(pl.multiple_of(r * rb, rb), rb)
        sl = slice(head * LANES, (head + 1) * LANES)
        slot = head % 2
        q = q_ref[rows, sl]
        m_acc = None
        for c in range(0, nk, kt):
            s = lax.dot_general(q, k_ref[c:c + kt, sl], NT_DIMS, preferred_element_type=F32)
            s_ref[slot, :, c:c + kt] = s
            mc = _lane_fold(s, jnp.maximum)
            m_acc = mc if m_acc is None else jnp.maximum(m_acc, mc)
        m_ref[slot] = jnp.max(m_acc, axis=-1, keepdims=True)

    def value_pass(head):
        sl = slice(head * LANES, (head + 1) * LANES)
        slot = head % 2
        m = m_ref[slot]
        o = jnp.zeros((rb, LANES), F32)
        for c in range(0, nk, kt):
            p = jnp.exp2(s_ref[slot, :, c:c + kt] - m)
            o = o + _dot(p.astype(BF), v_ref[c:c + kt, sl])
        return o

    def row_block(r, carry):
        lo = lax.broadcasted_iota(jnp.int32, (rb, LANES), 1) < MLA_V
        outs = []
        for head in range(nheads):
            if head + 1 < nheads:
                score_pass(r, head + 1)
            else:
                score_pass(jnp.minimum(r + 1, nrb - 1), 0)
            outs.append(value_pass(head))
        pairs = [jnp.where(lo, o0 / o0[:, MLA_V:MLA_V + 1], o1 / o1[:, 0:1])
                 for o0, o1 in zip(outs[0::2], outs[1::2])]
        o_ref[pl.ds(pl.multiple_of(r * rb, rb), rb), :] = jnp.concatenate(pairs, axis=-1).astype(BF)
        return carry

    score_pass(0, 0)
    lax.fori_loop(0, nrb, row_block, 0)


def _mla(q, k, v, batch):
    t, hq = q.shape
    s = t // batch
    rb = _tile(s, 256)
    nv = MLA_HEADS * MLA_V
    q3, k3, v3 = (a.reshape(batch, s, hq) for a in (q, k, v))
    blk = pl.BlockSpec((None, s, hq), lambda b: (b, 0, 0))
    o = pl.pallas_call(
        functools.partial(_mla_body, kt=_tile(s, 256)),
        grid=(batch,),
        in_specs=[blk, blk, blk],
        out_specs=pl.BlockSpec((None, s, nv), lambda b: (b, 0, 0)),
        out_shape=jax.ShapeDtypeStruct((batch, s, nv), BF),
        scratch_shapes=[pltpu.VMEM((2, rb, s), F32), pltpu.VMEM((2, rb, 1), F32)],
        compiler_params=_params("parallel"),
        name="mla_attn",
    )(q3, k3, v3)
    return o.reshape(t, nv)


def _fnet_fold_body(ha_ref, hm_ref, hx_ref, nw_ref, cc_ref, sc_ref, ec_ref, es_ref):
    tm = ha_ref.shape[0]
    nw = nw_ref[...]
    u_a = _rms(ha_ref[...], nw)
    u_m = _rms(hm_ref[...], nw)
    u_x = _rms(hx_ref[...], nw)[0:1]
    r = lax.broadcasted_iota(jnp.int32, (tm, tm), 0)
    c = lax.broadcasted_iota(jnp.int32, (tm, tm), 1)
    perm = jnp.where(r + c == tm, 1.0, 0.0).astype(BF)
    hi = u_m.astype(BF)
    low = (u_m - hi.astype(F32)).astype(BF)
    mirror = _dot(perm, hi) + _dot(perm, low)
    first = lax.broadcasted_iota(jnp.int32, u_a.shape, 0) == 0
    mirror = jnp.where(first, u_x, mirror)
    ue = (u_a + mirror).astype(BF)
    uo = (u_a - mirror).astype(BF)
    gc = cc_ref.shape[0]
    for g in range(ue.shape[-1] // gc):
        sl = slice(g * gc, (g + 1) * gc)
        ec_ref[:, sl] = _dot(ue[:, sl], cc_ref[...]).astype(BF)
        es_ref[:, sl] = _dot(uo[:, sl], sc_ref[...]).astype(BF)


def _fnet_seq_body(h_ref, hh_ref, nw_ref, cc_ref, cs_ref, ss_ref, ec_ref, es_ref, w_ref, o_ref, *, scale):
    ts = h_ref.shape[0]
    rows = pl.ds(pl.multiple_of(pl.program_id(1) * ts, ts), ts)
    y = _dot(cs_ref[rows, :], ec_ref[...]) - _dot(ss_ref[rows, :], es_ref[...])
    u_h = _rms(hh_ref[...], nw_ref[...]).astype(BF)
    gc = cc_ref.shape[0]
    x_h = jnp.concatenate([_dot(u_h[:, g * gc:(g + 1) * gc], cc_ref[...])
                           for g in range(u_h.shape[-1] // gc)], axis=-1)[0:1]
    odd = lax.broadcasted_iota(jnp.int32, (ts, 1), 0) % 2 == 1
    y = y + jnp.where(odd, -1.0, 1.0) * x_h
    o_ref[...] = h_ref[...] + _dot((y * scale).astype(BF), w_ref[...])


def _dft_tables(n, fold):
    j = np.arange(n)[:, None]
    k = np.arange(n // 2 if fold else n)[None, :]
    ang = ((j * k) % n) * (2.0 * np.pi / n)
    cos, sin = np.cos(ang), np.sin(ang)
    if fold:
        cos[:, 0] = 0.5
    return jnp.asarray(cos, BF), jnp.asarray(sin, BF)


def _fnet(h, norm_w, w_out, layer, o, batch, tables):
    t, d = h.shape
    s = t // batch
    cc, sc, cs, ss = tables
    gc = cc.shape[0]
    sub = 8
    tm = _tile(s // 2, 256)
    nt, nf = s // tm, s // 2 // tm
    nwspec = pl.BlockSpec((None, 1, d), lambda b, i: (layer, 0, 0))
    table = _resident((gc, gc), lambda b, i: (0, 0))
    ec, es = pl.pallas_call(
        _fnet_fold_body,
        grid=(batch, nf),
        in_specs=[
            pl.BlockSpec((tm, d), lambda b, i: (b * nt + i, 0)),
            pl.BlockSpec((tm, d), lambda b, i: (b * nt + nt - 1 - i, 0)),
            pl.BlockSpec((sub, d), lambda b, i: (b * (s // sub) + ((nt - i) % nt) * (tm // sub), 0)),
            nwspec, table, table,
        ],
        out_specs=[pl.BlockSpec((tm, d), lambda b, i: (b * nf + i, 0))] * 2,
        out_shape=[jax.ShapeDtypeStruct((t // 2, d), BF)] * 2,
        compiler_params=_params("parallel", "parallel"),
        name="fnet_fold_channel_dft",
    )(h, h, h, norm_w, cc, sc)
    ts = _tile(s, 512)
    ns = s // ts
    return pl.pallas_call(
        functools.partial(_fnet_seq_body, scale=(s * gc) ** -0.5),
        grid=(batch, ns),
        in_specs=[
            pl.BlockSpec((ts, d), lambda b, i: (b * ns + i, 0)),
            pl.BlockSpec((sub, d), lambda b, i: (b * (s // sub) + s // 2 // sub, 0)),
            nwspec, table,
            _resident((s, s // 2), lambda b, i: (0, 0)),
            _resident((s, s // 2), lambda b, i: (0, 0)),
            pl.BlockSpec((s // 2, d), lambda b, i: (b, 0)),
            pl.BlockSpec((s // 2, d), lambda b, i: (b, 0)),
            _resident((None, d, d), lambda b, i: (o, 0, 0)),
        ],
        out_specs=pl.BlockSpec((ts, d), lambda b, i: (b * ns + i, 0)),
        out_shape=jax.ShapeDtypeStruct((t, d), F32),
        compiler_params=_params("parallel", "arbitrary"),
        name="fnet_seq_dft",
    )(h, h, norm_w, cc, cs, ss, ec, es, w_out)


def _mixer_weights(w_in, w_uq, w_ukv, dt_bias, a_log, ssd_d):
    ne, d, _ = w_in.shape
    inner = SSD_HEADS * SSD_HEAD_DIM
    conv_ch = inner + 2 * SSD_GROUPS * SSD_STATE
    o_z, o_xbc = 0, inner
    o_dt = o_xbc + conv_ch
    o_cq = o_dt + 2 * SSD_HEADS
    o_ckv = o_cq + MLA_Q_RANK
    o_kr = o_ckv + MLA_KV_RANK
    half = MLA_ROPE // 2
    pad = LANES - MLA_NOPE - MLA_ROPE
    zeros = lambda *s: jnp.zeros(s, w_in.dtype)
    w_dt = jnp.concatenate([w_in[:, :, o_dt:o_cq], zeros(ne, d, LANES - 2 * SSD_HEADS)], axis=-1)
    kr1 = w_in[:, :, o_kr:o_kr + half]
    kr2 = w_in[:, :, o_kr + half:o_kr + MLA_ROPE]
    kr_a = jnp.concatenate([zeros(ne, d, MLA_NOPE), kr1, kr2, zeros(ne, d, pad)], axis=-1)
    kr_b = jnp.concatenate([zeros(ne, d, MLA_NOPE), -kr2, kr1, zeros(ne, d, pad)], axis=-1)
    pieces = [w_in[:, :, o_z:o_xbc], w_in[:, :, o_xbc:o_dt], w_dt, w_in[:, :, o_cq:o_ckv],
              w_in[:, :, o_ckv:o_kr], jnp.concatenate([kr_a, kr_b], axis=-1)]
    cols, c = [], 0
    for p in pieces:
        cols.append((c, c + p.shape[-1]))
        c += p.shape[-1]
    w_all = jnp.concatenate(pieces, axis=-1).astype(BF)

    uq = w_uq.reshape(ne, MLA_Q_RANK, MLA_HEADS, MLA_NOPE + MLA_ROPE)
    q_nope, q1, q2 = uq[..., :MLA_NOPE], uq[..., MLA_NOPE:MLA_NOPE + half], uq[..., MLA_NOPE + half:]
    zq = lambda n: jnp.zeros((ne, MLA_Q_RANK, MLA_HEADS, n), w_uq.dtype)
    hq = MLA_HEADS * LANES
    wqa = jnp.concatenate([q_nope, q1, q2, zq(pad)], axis=-1).reshape(ne, MLA_Q_RANK, hq).astype(BF)
    wqb = jnp.concatenate([zq(MLA_NOPE), -q2, q1, zq(pad)], axis=-1).reshape(ne, MLA_Q_RANK, hq).astype(BF)

    ukv = w_ukv.reshape(ne, MLA_KV_RANK, MLA_HEADS, MLA_NOPE + MLA_V)
    zkv = lambda *s: jnp.zeros((ne, MLA_KV_RANK) + s, w_ukv.dtype)
    wkn = jnp.concatenate([ukv[..., :MLA_NOPE], zkv(MLA_HEADS, LANES - MLA_NOPE)], axis=-1)
    wkn = wkn.reshape(ne, MLA_KV_RANK, hq).astype(BF)
    vv = ukv[..., MLA_NOPE:].reshape(ne, MLA_KV_RANK, MLA_HEADS // 2, 2, MLA_V)
    zv = zkv(MLA_HEADS // 2, MLA_V)
    wv = jnp.stack([jnp.concatenate([vv[:, :, :, 0], zv], axis=-1),
                    jnp.concatenate([zv, vv[:, :, :, 1]], axis=-1)], axis=3)
    wv = wv.reshape(ne, MLA_KV_RANK, hq).astype(BF)

    padl = lambda a: jnp.concatenate([a, jnp.zeros((ne, LANES - a.shape[-1]), a.dtype)], axis=-1)[:, None, :]
    bias = padl(dt_bias.reshape(ne, 2 * SSD_HEADS))
    a_neg = padl(-jnp.exp(a_log.reshape(ne, 2 * SSD_HEADS)))
    d_exp = jnp.repeat(ssd_d, SSD_HEAD_DIM, axis=-1)[:, None, :]
    return w_all, tuple(cols), wqa, wqb, wkn, wv, bias, a_neg, d_exp


def _rope_tables(positions):
    inv = 1.0 / (ROPE_THETA ** (jnp.arange(0, MLA_ROPE, 2, dtype=F32) / MLA_ROPE))
    ang = inv[:, None] * positions.astype(F32).reshape(1, -1)
    cos, sin = jnp.cos(ang), jnp.sin(ang)
    t = cos.shape[1]
    pad = LANES - MLA_NOPE - MLA_ROPE
    cos_t = jnp.concatenate([jnp.ones((MLA_NOPE, t), F32), cos, cos, jnp.zeros((pad, t), F32)], axis=0).T
    sin_t = jnp.concatenate([jnp.zeros((MLA_NOPE, t), F32), sin, sin, jnp.zeros((pad, t), F32)], axis=0).T
    return cos_t, sin_t


def kernel(x, mem, positions, mem_norm, final_norm, ffn1_norm, ffn1_w_gu, ffn1_w_down, mix_norm, xa_norm,
           xa_wq, xa_wkv, xa_wo, ffn2_norm, ffn2_w_gu, ffn2_w_down, w_in, conv_w, conv_b, dt_bias, a_log,
           ssd_d, ssd_norm, q_norm, w_uq, kv_norm, w_ukv, w_out, fnet_w_out):
    batch, seq, d = x.shape
    depth = ffn1_norm.shape[0]
    t = batch * seq
    bf = lambda a: a.astype(BF)
    row3 = lambda a: a[:, None, :]

    kv = _kvproj(mem.reshape(-1, d), mem_norm[None, :], bf(xa_wkv)).reshape(depth, batch, mem.shape[1], 2 * d)
    w_all, cols, wqa, wqb, wkn, wv, bias, a_neg, d_exp = _mixer_weights(w_in, w_uq, w_ukv, dt_bias, a_log, ssd_d)
    cos_t, sin_t = _rope_tables(positions)
    gc = d // FNET_GROUPS
    tables = _dft_tables(gc, fold=False) + _dft_tables(seq, fold=True)
    f1n, f2n, mxn, xan = row3(ffn1_norm), row3(ffn2_norm), row3(mix_norm), row3(xa_norm)
    f1gu, f1d, f2gu, f2d = bf(ffn1_w_gu), bf(ffn1_w_down), bf(ffn2_w_gu), bf(ffn2_w_down)
    wq, wo, w_mix_out, w_fnet = bf(xa_wq), bf(xa_wo), bf(w_out), bf(fnet_w_out)
    ssd_nw, qn, kvn = row3(ssd_norm), row3(q_norm), row3(kv_norm)

    h = x.reshape(t, d)
    for layer in range(depth):
        h = _ffn(h, f1n, f1gu, f1d, layer)
        if layer % 2 == 0:
            e = layer // 2
            z, xbc, dt_raw, q, k, v = _inproj(h, mxn, w_all, cols, qn, kvn, wqa, wqb, wkn, wv,
                                              cos_t, sin_t, layer, e)
            y_ssd = _ssd(xbc, dt_raw, z, conv_w, row3(conv_b), bias, a_neg, d_exp, ssd_nw, e, batch)
            o_mla = _mla(q, k, v, batch)
            mix = (y_ssd, o_mla, w_mix_out, e)
        else:
            h = _fnet(h, mxn, w_fnet, layer, layer // 2, batch, tables)
            mix = None
        h = _xattn(h, xan, wq, kv, wo, layer, batch, mix=mix)
        h = _ffn(h, f2n, f2gu, f2d, layer, final_w=final_norm[None, :] if layer == depth - 1 else None)
    return h.reshape(batch, seq, d)
```

```python
import functools
import math

import numpy as np
import jax
import jax.numpy as jnp
from jax import lax
from jax.experimental import pallas as pl
from jax.experimental.pallas import tpu as pltpu

EPS = 1e-6
BF = jnp.bfloat16
F32 = jnp.float32

V7X_VMEM_BYTES = 64 * 1024 * 1024
VMEM_LIMIT = V7X_VMEM_BYTES - 8 * 1024 * 1024
LANES = 128

SSD_HEADS = 16
SSD_HEAD_DIM = 64
SSD_GROUPS = 2
SSD_STATE = 128
SSD_CONV = 5
SSD_CHUNK = 128
MLA_HEADS = 8
MLA_Q_RANK = 512
MLA_KV_RANK = 256
MLA_NOPE = 64
MLA_ROPE = 32
MLA_V = 64
ROPE_THETA = 10000.0
FNET_GROUPS = 4
XA_HEADS = 4

NT_DIMS = (((1,), (1,)), ((), ()))
TN_DIMS = (((0,), (0,)), ((), ()))


def _params(*sem):
    return pltpu.CompilerParams(dimension_semantics=sem, vmem_limit_bytes=VMEM_LIMIT)


def _resident(shape, index_map):
    return pl.BlockSpec(shape, index_map, pipeline_mode=pl.Buffered(1))


def _rms(x, w):
    return x * lax.rsqrt(jnp.mean(x * x, axis=-1, keepdims=True) + EPS) * w


def _dot(a, b):
    return jnp.dot(a, b, preferred_element_type=F32)


def _tile(n, pref):
    t = min(n, pref)
    assert n % t == 0, (n, t)
    return t


def _ffn_body(*refs, chunks, final):
    if final:
        h_ref, nw_ref, wg_ref, wu_ref, wd_ref, fw_ref, o_ref = refs
    else:
        h_ref, nw_ref, wg_ref, wu_ref, wd_ref, o_ref = refs
    h = h_ref[...]
    xn = _rms(h, nw_ref[...]).astype(BF)
    acc = jnp.zeros(h.shape, F32)
    for a, b in chunks:
        g = _dot(xn, wg_ref[:, a:b])
        u = _dot(xn, wu_ref[:, a:b])
        act = (jax.nn.silu(g) * u).astype(BF)
        acc = acc + _dot(act, wd_ref[a:b, :])
    out = h + 0.5 * acc
    if final:
        out = _rms(out, fw_ref[...])
    o_ref[...] = out


def _ffn(h, norm_w, w_gu, w_down, layer, final_w=None):
    t, d = h.shape
    f = w_down.shape[1]
    tm = _tile(t, 1024)
    step = 768
    chunks = tuple((a, min(a + step, f)) for a in range(0, f, step))
    in_specs = [
        pl.BlockSpec((tm, d), lambda i: (i, 0)),
        pl.BlockSpec((None, 1, d), lambda i: (layer, 0, 0)),
        _resident((None, d, f), lambda i: (layer, 0, 0)),
        _resident((None, d, f), lambda i: (layer, 0, 1)),
        _resident((None, f, d), lambda i: (layer, 0, 0)),
    ]
    args = [h, norm_w, w_gu, w_gu, w_down]
    if final_w is not None:
        in_specs.append(pl.BlockSpec((1, d), lambda i: (0, 0)))
        args.append(final_w)
    return pl.pallas_call(
        functools.partial(_ffn_body, chunks=chunks, final=final_w is not None),
        grid=(t // tm,),
        in_specs=in_specs,
        out_specs=pl.BlockSpec((tm, d), lambda i: (i, 0)),
        out_shape=jax.ShapeDtypeStruct((t, d), F32),
        compiler_params=_params("parallel"),
        name="ffn",
    )(*args)


def _kvproj_body(m_ref, nw_ref, w_ref, o_ref):
    mn = _rms(m_ref[...], nw_ref[...]).astype(BF)
    o_ref[...] = _dot(mn, w_ref[...]).astype(BF)


def _kvproj(mem2d, mem_norm, wkv):
    n, d = mem2d.shape
    nl, _, d2 = wkv.shape
    tm = _tile(n, 512)
    return pl.pallas_call(
        _kvproj_body,
        grid=(nl, n // tm),
        in_specs=[
            pl.BlockSpec((tm, d), lambda l, i: (i, 0)),
            pl.BlockSpec((1, d), lambda l, i: (0, 0)),
            pl.BlockSpec((None, d, d2), lambda l, i: (l, 0, 0)),
        ],
        out_specs=pl.BlockSpec((None, tm, d2), lambda l, i: (l, i, 0)),
        out_shape=jax.ShapeDtypeStruct((nl, n, d2), BF),
        compiler_params=_params("parallel", "parallel"),
        name="xa_kvproj",
    )(mem2d, mem_norm, wkv)


def _xa_body(*refs, heads, mixed):
    if mixed:
        h_ref, nw_ref, wq_ref, k_ref, v_ref, wo_ref, y_ref, a_ref, wy_ref, wa_ref, o_ref = refs
        h = h_ref[...] + _dot(y_ref[...], wy_ref[...]) + _dot(a_ref[...], wa_ref[...])
    else:
        h_ref, nw_ref, wq_ref, k_ref, v_ref, wo_ref, o_ref = refs
        h = h_ref[...]
    hn = _rms(h, nw_ref[...]).astype(BF)
    dh = h.shape[-1] // heads
    q = (_dot(hn, wq_ref[...]) * (math.log2(math.e) * dh ** -0.5)).astype(BF)
    outs = []
    for i in range(heads):
        sl = slice(i * dh, (i + 1) * dh)
        s = lax.dot_general(q[:, sl], k_ref[:, sl], NT_DIMS, preferred_element_type=F32)
        p = jnp.exp2(s - jnp.max(s, axis=-1, keepdims=True))
        l = jnp.sum(p, axis=-1, keepdims=True)
        outs.append((_dot(p.astype(BF), v_ref[:, sl]) / l).astype(BF))
    o = jnp.concatenate(outs, axis=-1)
    o_ref[...] = h + _dot(o, wo_ref[...])


def _xattn(h, norm_w, wq, kv, wo, layer, batch, mix=None):
    t, d = h.shape
    s = t // batch
    nm = kv.shape[2]
    tm = _tile(s, 1024)
    ns = s // tm
    row = lambda b, i: (b * ns + i, 0)
    in_specs = [
        pl.BlockSpec((tm, d), row),
        pl.BlockSpec((None, 1, d), lambda b, i: (layer, 0, 0)),
        _resident((None, d, d), lambda b, i: (layer, 0, 0)),
        pl.BlockSpec((None, None, nm, d), lambda b, i: (layer, b, 0, 0)),
        pl.BlockSpec((None, None, nm, d), lambda b, i: (layer, b, 0, 1)),
        _resident((None, d, d), lambda b, i: (layer, 0, 0)),
    ]
    args = [h, norm_w, wq, kv, kv, wo]
    if mix is not None:
        y_ssd, o_mla, w_out, e = mix
        ny, na = y_ssd.shape[1], o_mla.shape[1]
        assert ny % na == 0
        in_specs += [
            pl.BlockSpec((tm, ny), row),
            pl.BlockSpec((tm, na), row),
            _resident((None, ny, d), lambda b, i: (e, 0, 0)),
            _resident((None, na, d), lambda b, i: (e, ny // na, 0)),
        ]
        args += [y_ssd, o_mla, w_out, w_out]
    return pl.pallas_call(
        functools.partial(_xa_body, heads=XA_HEADS, mixed=mix is not None),
        grid=(batch, ns),
        in_specs=in_specs,
        out_specs=pl.BlockSpec((tm, d), row),
        out_shape=jax.ShapeDtypeStruct((t, d), F32),
        compiler_params=_params("parallel", "parallel"),
        name="xattn",
    )(*args)


def _inproj_body(h_ref, nw_ref, w_ref, qn_ref, kvn_ref, wqa_ref, wqb_ref, wkn_ref, wv_ref,
                 cos_ref, sin_ref, z_ref, xbc_ref, dt_ref, q_ref, k_ref, v_ref, *, cols, scale):
    c_z, c_xbc, c_dt, c_cq, c_ckv, c_kr = cols
    u = _rms(h_ref[...], nw_ref[...]).astype(BF)
    z_ref[...] = _dot(u, w_ref[:, c_z[0]:c_z[1]]).astype(BF)
    xbc_ref[...] = _dot(u, w_ref[:, c_xbc[0]:c_xbc[1]])
    dt_ref[...] = _dot(u, w_ref[:, c_dt[0]:c_dt[1]])
    cqn = _rms(_dot(u, w_ref[:, c_cq[0]:c_cq[1]]), qn_ref[...]).astype(BF)
    ckvn = _rms(_dot(u, w_ref[:, c_ckv[0]:c_ckv[1]]), kvn_ref[...]).astype(BF)
    kr = _dot(u, w_ref[:, c_kr[0]:c_kr[1]])
    cos_t = cos_ref[...]
    sin_t = sin_ref[...]
    kp = kr[:, :LANES] * cos_t + kr[:, LANES:] * sin_t
    qa = _dot(cqn, wqa_ref[...])
    qb = _dot(cqn, wqb_ref[...])
    kn = _dot(ckvn, wkn_ref[...])
    for i in range(qa.shape[-1] // LANES):
        sl = slice(i * LANES, (i + 1) * LANES)
        q_ref[:, sl] = ((qa[:, sl] * cos_t + qb[:, sl] * sin_t) * scale).astype(BF)
        k_ref[:, sl] = (kn[:, sl] + kp).astype(BF)
    col = lax.broadcasted_iota(jnp.int32, (1, v_ref.shape[-1]), 1) % (2 * LANES)
    ones = jnp.where((col == MLA_V) | (col == LANES), 1.0, 0.0)
    v_ref[...] = (_dot(ckvn, wv_ref[...]) + ones).astype(BF)


def _inproj(h, norm_w, w_all, cols, q_norm, kv_norm, wqa, wqb, wkn, wv, cos_t, sin_t, layer, e):
    t, d = h.shape
    tm = _tile(t, 1024)
    wc = w_all.shape[-1]
    n_z = cols[0][1] - cols[0][0]
    n_xbc = cols[1][1] - cols[1][0]
    hq = wqa.shape[-1]
    row = lambda i: (i, 0)
    return pl.pallas_call(
        functools.partial(_inproj_body, cols=cols, scale=math.log2(math.e) * (MLA_NOPE + MLA_ROPE) ** -0.5),
        grid=(t // tm,),
        in_specs=[
            pl.BlockSpec((tm, d), row),
            pl.BlockSpec((None, 1, d), lambda i: (layer, 0, 0)),
            _resident((None, d, wc), lambda i: (e, 0, 0)),
            pl.BlockSpec((None, 1, MLA_Q_RANK), lambda i: (e, 0, 0)),
            pl.BlockSpec((None, 1, MLA_KV_RANK), lambda i: (e, 0, 0)),
            _resident((None, MLA_Q_RANK, hq), lambda i: (e, 0, 0)),
            _resident((None, MLA_Q_RANK, hq), lambda i: (e, 0, 0)),
            _resident((None, MLA_KV_RANK, hq), lambda i: (e, 0, 0)),
            _resident((None, MLA_KV_RANK, hq), lambda i: (e, 0, 0)),
            pl.BlockSpec((tm, LANES), row),
            pl.BlockSpec((tm, LANES), row),
        ],
        out_specs=[
            pl.BlockSpec((tm, n_z), row),
            pl.BlockSpec((tm, n_xbc), row),
            pl.BlockSpec((tm, LANES), row),
            pl.BlockSpec((tm, hq), row),
            pl.BlockSpec((tm, hq), row),
            pl.BlockSpec((tm, hq), row),
        ],
        out_shape=[
            jax.ShapeDtypeStruct((t, n_z), BF),
            jax.ShapeDtypeStruct((t, n_xbc), F32),
            jax.ShapeDtypeStruct((t, LANES), F32),
            jax.ShapeDtypeStruct((t, hq), BF),
            jax.ShapeDtypeStruct((t, hq), BF),
            jax.ShapeDtypeStruct((t, hq), BF),
        ],
        compiler_params=_params("parallel"),
        name="mix_inproj",
    )(h, norm_w, w_all, q_norm, kv_norm, wqa, wqb, wkn, wv, cos_t, sin_t)


def _split3_dot(a_bf, x):
    hi = x.astype(BF)
    r1 = x - hi.astype(F32)
    mid = r1.astype(BF)
    low = (r1 - mid.astype(F32)).astype(BF)
    return _dot(a_bf, hi) + _dot(a_bf, mid) + _dot(a_bf, low)


def _ssd_states_phase(z, j, first, last, xc_ref, xp_ref, xn_ref, dt_ref, cw_ref, cb_ref, bias_ref, a_ref, e_ref,
                      sm_ref, st_ref, x_ref, b_ref, c_ref, sel_ref, rows_ref, dec_ref, carry_ref):
    q = SSD_CHUNK
    nh, n = SSD_HEADS, SSD_STATE
    inner = nh * SSD_HEAD_DIM
    gw = inner // SSD_GROUPS
    half = SSD_CONV // 2
    edge = xp_ref.shape[0]
    nsub = xc_ref.shape[0] // q
    r0 = pl.multiple_of(j * q, q)
    cur = xc_ref[pl.ds(r0, q), :]
    before = xc_ref[pl.ds(pl.multiple_of(jnp.maximum(r0 - edge, 0), edge), edge), :]
    after = xc_ref[pl.ds(pl.multiple_of(jnp.minimum(r0 + q, (nsub - 1) * q + q - edge), edge), edge), :]
    prev = jnp.where(first, 0.0, jnp.where(j == 0, xp_ref[...], before))
    nxt = jnp.where(last, 0.0, jnp.where(j == nsub - 1, xn_ref[...], after))
    fill = jnp.zeros((sm_ref.shape[1] - q - 2 * edge, cur.shape[1]), F32)
    window = jnp.concatenate([prev, cur, nxt, fill], axis=0).astype(BF)
    shifted = _dot(sm_ref[...], window)
    acc = cur * cw_ref[half:half + 1, :] + cb_ref[...]
    blk = 0
    for k in range(SSD_CONV):
        if k != half:
            acc = acc + shifted[blk * q:(blk + 1) * q] * cw_ref[k:k + 1, :]
            blk += 1
    xbc = jax.nn.silu(acc)
    x = xbc[:, :inner]
    bm = xbc[:, inner:inner + SSD_GROUPS * n].astype(BF)
    x_ref[z] = x.astype(BF)
    b_ref[z] = bm
    c_ref[z] = xbc[:, inner + SSD_GROUPS * n:].astype(BF)
    dt = jax.nn.softplus(dt_ref[pl.ds(r0, q), :] + bias_ref[...])
    la = dt * (a_ref[...] * math.log2(math.e))
    row = lax.broadcasted_iota(jnp.int32, (q, q), 0)
    col = lax.broadcasted_iota(jnp.int32, (q, q), 1)
    tril = jnp.where(row >= col, 1.0, 0.0).astype(BF)
    cum = _split3_dot(tril, la)
    tot = cum[q - 1:q, :]
    rev = tot - cum + la
    fwd_lane = lax.broadcasted_iota(jnp.int32, (q, LANES), 1) < nh
    sel = jnp.where(fwd_lane, cum, rev)
    sel_ref[z] = sel
    dt_t = dt.T
    rows_ref[z, 0:2 * nh, :] = (sel - jnp.log2(dt)).T[0:2 * nh]
    rows_ref[z, 2 * nh:3 * nh, :] = jnp.log2(dt_t[0:nh] + dt_t[nh:2 * nh])
    w_all = (jnp.exp2(tot - sel) * dt).astype(BF)
    wexp = _dot(w_all, e_ref[...])
    etot = jnp.exp2(tot)
    lo1 = lax.broadcasted_iota(jnp.int32, (1, LANES), 1) < SSD_HEAD_DIM
    ppg = gw // LANES
    for d in range(2):
        xw = (x * wexp[:, d * inner:(d + 1) * inner]).astype(BF)
        for g in range(SSD_GROUPS):
            upd = lax.dot_general(bm[:, g * n:(g + 1) * n], xw[:, g * gw:(g + 1) * gw], TN_DIMS,
                                  preferred_element_type=F32)
            for j in range(ppg):
                p = g * ppg + j
                h0 = d * nh + 2 * p
                dec = jnp.where(lo1, etot[:, h0:h0 + 1], etot[:, h0 + 1:h0 + 2])
                contrib = upd[:, j * LANES:(j + 1) * LANES]
                if d == 0:
                    state = carry_ref[0, p]
                    st_ref[z, 0, p] = state
                    carry_ref[0, p] = state * dec + contrib
                else:
                    st_ref[z, 1, p] = contrib
                    dec_ref[z, p] = jnp.broadcast_to(dec, dec_ref.shape[2:])


def _ssd_output_phase(z, j, z_ref, d_ref, nw_ref, o_ref, st_ref, x_ref, b_ref, c_ref, sel_ref, rows_ref,
                      dec_ref, carry_ref):
    q = SSD_CHUNK
    r0 = pl.multiple_of(j * q, q)
    nh, n, hpg = SSD_HEADS, SSD_STATE, SSD_HEADS // SSD_GROUPS
    sel = sel_ref[z]
    rows = rows_ref[z]
    row = lax.broadcasted_iota(jnp.int32, (q, q), 0)
    col = lax.broadcasted_iota(jnp.int32, (q, q), 1)
    lower = row > col
    diag = row == col
    lo = lax.broadcasted_iota(jnp.int32, (q, LANES), 1) < SSD_HEAD_DIM
    ys = []
    for g in range(SSD_GROUPS):
        bg = b_ref[z, :, g * n:(g + 1) * n]
        cg = c_ref[z, :, g * n:(g + 1) * n]
        cb = lax.dot_general(cg, bg, NT_DIMS, preferred_element_type=F32)
        cg32 = cg.astype(F32)
        for j in range(hpg // 2):
            p = g * (hpg // 2) + j
            lhs = []
            for h in (2 * p, 2 * p + 1):
                hb = nh + h
                a_f = jnp.broadcast_to(sel[:, h:h + 1], (q, q))
                a_b = jnp.broadcast_to(sel[:, hb:hb + 1], (q, q))
                seg = jnp.where(lower, a_f - rows[h:h + 1, :],
                                jnp.where(diag, rows[2 * nh + h:2 * nh + h + 1, :], a_b - rows[hb:hb + 1, :]))
                m = (cb * jnp.exp2(seg)).astype(BF)
                cef = (cg32 * jnp.exp2(a_f)).astype(BF)
                ceb = (cg32 * jnp.exp2(a_b)).astype(BF)
                lhs.append(jnp.concatenate([m, cef, ceb], axis=1))
            xp = x_ref[z, :, p * LANES:(p + 1) * LANES]
            back = carry_ref[1, p]
            carry_ref[1, p] = back * dec_ref[z, p][0:1] + st_ref[z, 1, p]
            rhs = jnp.concatenate([xp, st_ref[z, 0, p].astype(BF), back.astype(BF)], axis=0)
            out = _dot(jnp.concatenate(lhs, axis=0), rhs)
            ys.append(jnp.where(lo, out[:q], out[q:]) + xp.astype(F32) * d_ref[:, p * LANES:(p + 1) * LANES])
    y = jnp.concatenate(ys, axis=-1)
    gated = y * jax.nn.silu(z_ref[pl.ds(r0, q), :].astype(F32))
    o_ref[pl.ds(r0, q), :] = _rms(gated, nw_ref[...]).astype(BF)


def _ssd_body(xc_ref, xp_ref, xn_ref, dt_ref, z_ref, cw_ref, cb_ref, bias_ref, a_ref, d_ref, nw_ref, e_ref, sm_ref,
              o_ref, st_ref, x_ref, b_ref, c_ref, sel_ref, rows_ref, dec_ref, carry_ref):
    t = pl.program_id(1)
    nc = st_ref.shape[0]
    nsub = xc_ref.shape[0] // SSD_CHUNK
    nb = nc // nsub

    @pl.when(t == 0)
    def _():
        carry_ref[0] = jnp.zeros(carry_ref.shape[1:], F32)

    @pl.when(t == nb)
    def _():
        carry_ref[1] = jnp.zeros(carry_ref.shape[1:], F32)

    @pl.when(t < nb)
    def _():
        def sub(j, carry):
            z = t * nsub + j
            _ssd_states_phase(z, j, z == 0, z == nc - 1, xc_ref, xp_ref, xn_ref, dt_ref, cw_ref, cb_ref,
                              bias_ref, a_ref, e_ref, sm_ref, st_ref, x_ref, b_ref, c_ref, sel_ref, rows_ref,
                              dec_ref, carry_ref)
            return carry
        lax.fori_loop(0, nsub, sub, 0)

    @pl.when(t >= nb)
    def _():
        def sub(i, carry):
            j = nsub - 1 - i
            z = (2 * nb - 1 - t) * nsub + j
            _ssd_output_phase(z, j, z_ref, d_ref, nw_ref, o_ref, st_ref, x_ref, b_ref, c_ref, sel_ref,
                              rows_ref, dec_ref, carry_ref)
            return carry
        lax.fori_loop(0, nsub, sub, 0)


def _conv_shift_table(q, edge, rows):
    half = SSD_CONV // 2
    m = np.zeros(((SSD_CONV - 1) * q, rows), np.float32)
    blk = 0
    for k in range(SSD_CONV):
        if k != half:
            m[blk * q + np.arange(q), edge + np.arange(q) + k - half] = 1.0
            blk += 1
    return jnp.asarray(m, BF)


def _head_expand_table():
    inner = SSD_HEADS * SSD_HEAD_DIM
    e = np.zeros((LANES, 2 * inner), np.float32)
    for h in range(2 * SSD_HEADS):
        e[h, h * SSD_HEAD_DIM:(h + 1) * SSD_HEAD_DIM] = 1.0
    return jnp.asarray(e, BF)


def _ssd(xbc, dt_raw, z, conv_w, conv_b, dt_bias, a_neg, d_exp, ssd_norm, e, batch):
    t, c = xbc.shape
    s = t // batch
    inner = SSD_HEADS * SSD_HEAD_DIM
    gn = SSD_GROUPS * SSD_STATE
    q = SSD_CHUNK
    nc = s // q
    nsub = next(n for n in (8, 4, 2, 1) if nc % n == 0)
    nb = nc // nsub
    rows = nsub * q
    edge = 8
    epb = rows // edge
    blk1 = lambda i: jnp.minimum(i, nb - 1)
    blk3 = lambda i: nb - 1 - jnp.maximum(i - nb, 0)
    par = lambda b, i: (e, 0, 0)
    window_rows = 2 * q
    return pl.pallas_call(
        _ssd_body,
        grid=(batch, 2 * nb),
        in_specs=[
            pl.BlockSpec((rows, c), lambda b, i: (b * nb + blk1(i), 0)),
            pl.BlockSpec((edge, c), lambda b, i: (b * nb * epb + jnp.maximum(blk1(i) * epb - 1, 0), 0)),
            pl.BlockSpec((edge, c), lambda b, i: (b * nb * epb + jnp.minimum((blk1(i) + 1) * epb, nb * epb - 1), 0)),
            pl.BlockSpec((rows, LANES), lambda b, i: (b * nb + blk1(i), 0)),
            pl.BlockSpec((rows, inner), lambda b, i: (b * nb + blk3(i), 0)),
            pl.BlockSpec((None, SSD_CONV, c), par),
            pl.BlockSpec((None, 1, c), par),
            pl.BlockSpec((None, 1, LANES), par),
            pl.BlockSpec((None, 1, LANES), par),
            pl.BlockSpec((None, 1, inner), par),
            pl.BlockSpec((None, 1, inner), par),
            _resident((LANES, 2 * inner), lambda b, i: (0, 0)),
            _resident(((SSD_CONV - 1) * q, window_rows), lambda b, i: (0, 0)),
        ],
        out_specs=pl.BlockSpec((rows, inner), lambda b, i: (b * nb + blk3(i), 0)),
        out_shape=jax.ShapeDtypeStruct((t, inner), BF),
        scratch_shapes=[
            pltpu.VMEM((nc, 2, SSD_HEADS // 2, SSD_STATE, LANES), F32),
            pltpu.VMEM((nc, q, inner), BF),
            pltpu.VMEM((nc, q, gn), BF),
            pltpu.VMEM((nc, q, gn), BF),
            pltpu.VMEM((nc, q, LANES), F32),
            pltpu.VMEM((nc, 3 * SSD_HEADS, q), F32),
            pltpu.VMEM((nc, SSD_HEADS // 2, edge, LANES), F32),
            pltpu.VMEM((2, SSD_HEADS // 2, SSD_STATE, LANES), F32),
        ],
        compiler_params=_params("parallel", "arbitrary"),
        name="ssd",
    )(xbc, xbc, xbc, dt_raw, z, conv_w, conv_b, dt_bias, a_neg, d_exp, ssd_norm, _head_expand_table(),
      _conv_shift_table(q, edge, window_rows))


def _lane_fold(x, op):
    out = x[:, :LANES]
    for j in range(1, x.shape[-1] // LANES):
        out = op(out, x[:, j * LANES:(j + 1) * LANES])
    return out


def _mla_body(q_ref, k_ref, v_ref, o_ref, s_ref, m_ref, *, kt):
    nk = k_ref.shape[0]
    _, rb, _ = s_ref.shape
    nrb = q_ref.shape[0] // rb
    nheads = q_ref.shape[1] // LANES

    def score_pass(r, head):
        rows = pl.ds(pl.multiple_of(r * rb, rb), rb)
        sl = slice(head * LANES, (head + 1) * LANES)
        slot = head % 2
        q = q_ref[rows, sl]
        m_acc = None
        for c in range(0, nk, kt):
            s = lax.dot_general(q, k_ref[c:c + kt, sl], NT_DIMS, preferred_element_type=F32)
            s_ref[slot, :, c:c + kt] = s
            mc = _lane_fold(s, jnp.maximum)
            m_acc = mc if m_acc is None else jnp.maximum(m_acc, mc)
        m_ref[slot] = jnp.max(m_acc, axis=-1, keepdims=True)

    def value_pass(head):
        sl = slice(head * LANES, (head + 1) * LANES)
        slot = head % 2
        m = m_ref[slot]
        o = jnp.zeros((rb, LANES), F32)
        for c in range(0, nk, kt):
            p = jnp.exp2(s_ref[slot, :, c:c + kt] - m)
            o = o + _dot(p.astype(BF), v_ref[c:c + kt, sl])
        return o

    def row_block(r, carry):
        lo = lax.broadcasted_iota(jnp.int32, (rb, LANES), 1) < MLA_V
        outs = []
        for head in range(nheads):
            if head + 1 < nheads:
                score_pass(r, head + 1)
            else:
                score_pass(jnp.minimum(r + 1, nrb - 1), 0)
            outs.append(value_pass(head))
        pairs = [jnp.where(lo, o0 / o0[:, MLA_V:MLA_V + 1], o1 / o1[:, 0:1])
                 for o0, o1 in zip(outs[0::2], outs[1::2])]
        o_ref[pl.ds(pl.multiple_of(r * rb, rb), rb), :] = jnp.concatenate(pairs, axis=-1).astype(BF)
        return carry

    score_pass(0, 0)
    lax.fori_loop(0, nrb, row_block, 0)


def _mla(q, k, v, batch):
    t, hq = q.shape
    s = t // batch
    rb = _tile(s, 256)
    nv = MLA_HEADS * MLA_V
    q3, k3, v3 = (a.reshape(batch, s, hq) for a in (q, k, v))
    blk = pl.BlockSpec((None, s, hq), lambda b: (b, 0, 0))
    o = pl.pallas_call(
        functools.partial(_mla_body, kt=_tile(s, 256)),
        grid=(batch,),
        in_specs=[blk, blk, blk],
        out_specs=pl.BlockSpec((None, s, nv), lambda b: (b, 0, 0)),
        out_shape=jax.ShapeDtypeStruct((batch, s, nv), BF),
        scratch_shapes=[pltpu.VMEM((2, rb, s), F32), pltpu.VMEM((2, rb, 1), F32)],
        compiler_params=_params("parallel"),
        name="mla_attn",
    )(q3, k3, v3)
    return o.reshape(t, nv)


def _fnet_fold_body(ha_ref, hm_ref, hx_ref, nw_ref, cc_ref, sc_ref, ec_ref, es_ref):
    tm = ha_ref.shape[0]
    nw = nw_ref[...]
    u_a = _rms(ha_ref[...], nw)
    u_m = _rms(hm_ref[...], nw)
    u_x = _rms(hx_ref[...], nw)[0:1]
    r = lax.broadcasted_iota(jnp.int32, (tm, tm), 0)
    c = lax.broadcasted_iota(jnp.int32, (tm, tm), 1)
    perm = jnp.where(r + c == tm, 1.0, 0.0).astype(BF)
    hi = u_m.astype(BF)
    low = (u_m - hi.astype(F32)).astype(BF)
    mirror = _dot(perm, hi) + _dot(perm, low)
    first = lax.broadcasted_iota(jnp.int32, u_a.shape, 0) == 0
    mirror = jnp.where(first, u_x, mirror)
    ue = (u_a + mirror).astype(BF)
    uo = (u_a - mirror).astype(BF)
    gc = cc_ref.shape[0]
    for g in range(ue.shape[-1] // gc):
        sl = slice(g * gc, (g + 1) * gc)
        ec_ref[:, sl] = _dot(ue[:, sl], cc_ref[...]).astype(BF)
        es_ref[:, sl] = _dot(uo[:, sl], sc_ref[...]).astype(BF)


def _fnet_seq_body(h_ref, hh_ref, nw_ref, cc_ref, cs_ref, ss_ref, ec_ref, es_ref, w_ref, o_ref, *, scale):
    ts = h_ref.shape[0]
    rows = pl.ds(pl.multiple_of(pl.program_id(1) * ts, ts), ts)
    y = _dot(cs_ref[rows, :], ec_ref[...]) - _dot(ss_ref[rows, :], es_ref[...])
    u_h = _rms(hh_ref[...], nw_ref[...]).astype(BF)
    gc = cc_ref.shape[0]
    x_h = jnp.concatenate([_dot(u_h[:, g * gc:(g + 1) * gc], cc_ref[...])
                           for g in range(u_h.shape[-1] // gc)], axis=-1)[0:1]
    odd = lax.broadcasted_iota(jnp.int32, (ts, 1), 0) % 2 == 1
    y = y + jnp.where(odd, -1.0, 1.0) * x_h
    o_ref[...] = h_ref[...] + _dot((y * scale).astype(BF), w_ref[...])


def _dft_tables(n, fold):
    j = np.arange(n)[:, None]
    k = np.arange(n // 2 if fold else n)[None, :]
    ang = ((j * k) % n) * (2.0 * np.pi / n)
    cos, sin = np.cos(ang), np.sin(ang)
    if fold:
        cos[:, 0] = 0.5
    return jnp.asarray(cos, BF), jnp.asarray(sin, BF)


def _fnet(h, norm_w, w_out, layer, o, batch, tables):
    t, d = h.shape
    s = t // batch
    cc, sc, cs, ss = tables
    gc = cc.shape[0]
    sub = 8
    tm = _tile(s // 2, 256)
    nt, nf = s // tm, s // 2 // tm
    nwspec = pl.BlockSpec((None, 1, d), lambda b, i: (layer, 0, 0))
    table = _resident((gc, gc), lambda b, i: (0, 0))
    ec, es = pl.pallas_call(
        _fnet_fold_body,
        grid=(batch, nf),
        in_specs=[
            pl.BlockSpec((tm, d), lambda b, i: (b * nt + i, 0)),
            pl.BlockSpec((tm, d), lambda b, i: (b * nt + nt - 1 - i, 0)),
            pl.BlockSpec((sub, d), lambda b, i: (b * (s // sub) + ((nt - i) % nt) * (tm // sub), 0)),
            nwspec, table, table,
        ],
        out_specs=[pl.BlockSpec((tm, d), lambda b, i: (b * nf + i, 0))] * 2,
        out_shape=[jax.ShapeDtypeStruct((t // 2, d), BF)] * 2,
        compiler_params=_params("parallel", "parallel"),
        name="fnet_fold_channel_dft",
    )(h, h, h, norm_w, cc, sc)
    ts = _tile(s, 512)
    ns = s // ts
    return pl.pallas_call(
        functools.partial(_fnet_seq_body, scale=(s * gc) ** -0.5),
        grid=(batch, ns),
        in_specs=[
            pl.BlockSpec((ts, d), lambda b, i: (b * ns + i, 0)),
            pl.BlockSpec((sub, d), lambda b, i: (b * (s // sub) + s // 2 // sub, 0)),
            nwspec, table,
            _resident((s, s // 2), lambda b, i: (0, 0)),
            _resident((s, s // 2), lambda b, i: (0, 0)),
            pl.BlockSpec((s // 2, d), lambda b, i: (b, 0)),
            pl.BlockSpec((s // 2, d), lambda b, i: (b, 0)),
            _resident((None, d, d), lambda b, i: (o, 0, 0)),
        ],
        out_specs=pl.BlockSpec((ts, d), lambda b, i: (b * ns + i, 0)),
        out_shape=jax.ShapeDtypeStruct((t, d), F32),
        compiler_params=_params("parallel", "arbitrary"),
        name="fnet_seq_dft",
    )(h, h, norm_w, cc, cs, ss, ec, es, w_out)


def _mixer_weights(w_in, w_uq, w_ukv, dt_bias, a_log, ssd_d):
    w_in, w_uq, w_ukv = w_in.astype(BF), w_uq.astype(BF), w_ukv.astype(BF)
    ne, d, _ = w_in.shape
    inner = SSD_HEADS * SSD_HEAD_DIM
    conv_ch = inner + 2 * SSD_GROUPS * SSD_STATE
    o_z, o_xbc = 0, inner
    o_dt = o_xbc + conv_ch
    o_cq = o_dt + 2 * SSD_HEADS
    o_ckv = o_cq + MLA_Q_RANK
    o_kr = o_ckv + MLA_KV_RANK
    half = MLA_ROPE // 2
    pad = LANES - MLA_NOPE - MLA_ROPE
    zeros = lambda *s: jnp.zeros(s, w_in.dtype)
    w_dt = jnp.concatenate([w_in[:, :, o_dt:o_cq], zeros(ne, d, LANES - 2 * SSD_HEADS)], axis=-1)
    kr1 = w_in[:, :, o_kr:o_kr + half]
    kr2 = w_in[:, :, o_kr + half:o_kr + MLA_ROPE]
    kr_a = jnp.concatenate([zeros(ne, d, MLA_NOPE), kr1, kr2, zeros(ne, d, pad)], axis=-1)
    kr_b = jnp.concatenate([zeros(ne, d, MLA_NOPE), -kr2, kr1, zeros(ne, d, pad)], axis=-1)
    pieces = [w_in[:, :, o_z:o_xbc], w_in[:, :, o_xbc:o_dt], w_dt, w_in[:, :, o_cq:o_ckv],
              w_in[:, :, o_ckv:o_kr], jnp.concatenate([kr_a, kr_b], axis=-1)]
    cols, c = [], 0
    for p in pieces:
        cols.append((c, c + p.shape[-1]))
        c += p.shape[-1]
    w_all = jnp.concatenate(pieces, axis=-1).astype(BF)

    uq = w_uq.reshape(ne, MLA_Q_RANK, MLA_HEADS, MLA_NOPE + MLA_ROPE)
    q_nope, q1, q2 = uq[..., :MLA_NOPE], uq[..., MLA_NOPE:MLA_NOPE + half], uq[..., MLA_NOPE + half:]
    zq = lambda n: jnp.zeros((ne, MLA_Q_RANK, MLA_HEADS, n), w_uq.dtype)
    hq = MLA_HEADS * LANES
    wqa = jnp.concatenate([q_nope, q1, q2, zq(pad)], axis=-1).reshape(ne, MLA_Q_RANK, hq).astype(BF)
    wqb = jnp.concatenate([zq(MLA_NOPE), -q2, q1, zq(pad)], axis=-1).reshape(ne, MLA_Q_RANK, hq).astype(BF)

    ukv = w_ukv.reshape(ne, MLA_KV_RANK, MLA_HEADS, MLA_NOPE + MLA_V)
    zkv = lambda *s: jnp.zeros((ne, MLA_KV_RANK) + s, w_ukv.dtype)
    wkn = jnp.concatenate([ukv[..., :MLA_NOPE], zkv(MLA_HEADS, LANES - MLA_NOPE)], axis=-1)
    wkn = wkn.reshape(ne, MLA_KV_RANK, hq).astype(BF)
    vv = ukv[..., MLA_NOPE:].reshape(ne, MLA_KV_RANK, MLA_HEADS // 2, 2, MLA_V)
    zv = zkv(MLA_HEADS // 2, MLA_V)
    wv = jnp.stack([jnp.concatenate([vv[:, :, :, 0], zv], axis=-1),
                    jnp.concatenate([zv, vv[:, :, :, 1]], axis=-1)], axis=3)
    wv = wv.reshape(ne, MLA_KV_RANK, hq).astype(BF)

    padl = lambda a: jnp.concatenate([a, jnp.zeros((ne, LANES - a.shape[-1]), a.dtype)], axis=-1)[:, None, :]
    bias = padl(dt_bias.reshape(ne, 2 * SSD_HEADS))
    a_neg = padl(-jnp.exp(a_log.reshape(ne, 2 * SSD_HEADS)))
    d_exp = jnp.repeat(ssd_d, SSD_HEAD_DIM, axis=-1)[:, None, :]
    return w_all, tuple(cols), wqa, wqb, wkn, wv, bias, a_neg, d_exp


def _rope_tables(positions):
    inv = 1.0 / (ROPE_THETA ** (jnp.arange(0, MLA_ROPE, 2, dtype=F32) / MLA_ROPE))
    ang = inv[:, None] * positions.astype(F32).reshape(1, -1)
    cos, sin = jnp.cos(ang), jnp.sin(ang)
    t = cos.shape[1]
    pad = LANES - MLA_NOPE - MLA_ROPE
    cos_t = jnp.concatenate([jnp.ones((MLA_NOPE, t), F32), cos, cos, jnp.zeros((pad, t), F32)], axis=0).T
    sin_t = jnp.concatenate([jnp.zeros((MLA_NOPE, t), F32), sin, sin, jnp.zeros((pad, t), F32)], axis=0).T
    return cos_t, sin_t


def kernel(x, mem, positions, mem_norm, final_norm, ffn1_norm, ffn1_w_gu, ffn1_w_down, mix_norm, xa_norm,
           xa_wq, xa_wkv, xa_wo, ffn2_norm, ffn2_w_gu, ffn2_w_down, w_in, conv_w, conv_b, dt_bias, a_log,
           ssd_d, ssd_norm, q_norm, w_uq, kv_norm, w_ukv, w_out, fnet_w_out):
    batch, seq, d = x.shape
    depth = ffn1_norm.shape[0]
    t = batch * seq
    bf = lambda a: a.astype(BF)
    row3 = lambda a: a[:, None, :]

    kv = _kvproj(mem.reshape(-1, d), mem_norm[None, :], bf(xa_wkv)).reshape(depth, batch, mem.shape[1], 2 * d)
    w_all, cols, wqa, wqb, wkn, wv, bias, a_neg, d_exp = _mixer_weights(w_in, w_uq, w_ukv, dt_bias, a_log, ssd_d)
    cos_t, sin_t = _rope_tables(positions)
    gc = d // FNET_GROUPS
    tables = _dft_tables(gc, fold=False) + _dft_tables(seq, fold=True)
    f1n, f2n, mxn, xan = row3(ffn1_norm), row3(ffn2_norm), row3(mix_norm), row3(xa_norm)
    f1gu, f1d, f2gu, f2d = bf(ffn1_w_gu), bf(ffn1_w_down), bf(ffn2_w_gu), bf(ffn2_w_down)
    wq, wo, w_mix_out, w_fnet = bf(xa_wq), bf(xa_wo), bf(w_out), bf(fnet_w_out)
    ssd_nw, qn, kvn = row3(ssd_norm), row3(q_norm), row3(kv_norm)

    h = x.reshape(t, d)
    for layer in range(depth):
        h = _ffn(h, f1n, f1gu, f1d, layer)
        if layer % 2 == 0:
            e = layer // 2
            z, xbc, dt_raw, q, k, v = _inproj(h, mxn, w_all, cols, qn, kvn, wqa, wqb, wkn, wv,
                                              cos_t, sin_t, layer, e)
            y_ssd = _ssd(xbc, dt_raw, z, conv_w, row3(conv_b), bias, a_neg, d_exp, ssd_nw, e, batch)
            o_mla = _mla(q, k, v, batch)
            mix = (y_ssd, o_mla, w_mix_out, e)
        else:
            h = _fnet(h, mxn, w_fnet, layer, layer // 2, batch, tables)
            mix = None
        h = _xattn(h, xan, wq, kv, wo, layer, batch, mix=mix)
        h = _ffn(h, f2n, f2gu, f2d, layer, final_w=final_norm[None, :] if layer == depth - 1 else None)
    return h.reshape(batch, seq, d)
```

```python
import functools
import math

import numpy as np
import jax
import jax.numpy as jnp
from jax import lax
from jax.experimental import pallas as pl
from jax.experimental.pallas import tpu as pltpu

EPS = 1e-6
BF = jnp.bfloat16
F32 = jnp.float32

V7X_VMEM_BYTES = 64 * 1024 * 1024
VMEM_LIMIT = V7X_VMEM_BYTES - 8 * 1024 * 1024
LANES = 128

SSD_HEADS = 16
SSD_HEAD_DIM = 64
SSD_GROUPS = 2
SSD_STATE = 128
SSD_CONV = 5
SSD_CHUNK = 128
MLA_HEADS = 8
MLA_Q_RANK = 512
MLA_KV_RANK = 256
MLA_NOPE = 64
MLA_ROPE = 32
MLA_V = 64
ROPE_THETA = 10000.0
FNET_GROUPS = 4
XA_HEADS = 4

NT_DIMS = (((1,), (1,)), ((), ()))
TN_DIMS = (((0,), (0,)), ((), ()))


def _params(*sem):
    return pltpu.CompilerParams(dimension_semantics=sem, vmem_limit_bytes=VMEM_LIMIT)


def _resident(shape, index_map):
    return pl.BlockSpec(shape, index_map, pipeline_mode=pl.Buffered(1))


def _rms(x, w):
    return x * lax.rsqrt(jnp.mean(x * x, axis=-1, keepdims=True) + EPS) * w


def _dot(a, b):
    return jnp.dot(a, b, preferred_element_type=F32)


def _tile(n, pref):
    t = min(n, pref)
    assert n % t == 0, (n, t)
    return t


def _ffn_body(*refs, chunks, final):
    if final:
        h_ref, nw_ref, wg_ref, wu_ref, wd_ref, fw_ref, o_ref = refs
    else:
        h_ref, nw_ref, wg_ref, wu_ref, wd_ref, o_ref = refs
    h = h_ref[...]
    xn = _rms(h, nw_ref[...]).astype(BF)
    acc = jnp.zeros(h.shape, F32)
    for a, b in chunks:
        g = _dot(xn, wg_ref[:, a:b])
        u = _dot(xn, wu_ref[:, a:b])
        act = (jax.nn.silu(g) * u).astype(BF)
        acc = acc + _dot(act, wd_ref[a:b, :])
    out = h + 0.5 * acc
    if final:
        out = _rms(out, fw_ref[...])
    o_ref[...] = out


def _ffn(h, norm_w, w_gu, w_down, layer, final_w=None):
    t, d = h.shape
    f = w_down.shape[1]
    tm = _tile(t, 1024)
    step = 768
    chunks = tuple((a, min(a + step, f)) for a in range(0, f, step))
    in_specs = [
        pl.BlockSpec((tm, d), lambda i: (i, 0)),
        pl.BlockSpec((None, 1, d), lambda i: (layer, 0, 0)),
        _resident((None, d, f), lambda i: (layer, 0, 0)),
        _resident((None, d, f), lambda i: (layer, 0, 1)),
        _resident((None, f, d), lambda i: (layer, 0, 0)),
    ]
    args = [h, norm_w, w_gu, w_gu, w_down]
    if final_w is not None:
        in_specs.append(pl.BlockSpec((1, d), lambda i: (0, 0)))
        args.append(final_w)
    return pl.pallas_call(
        functools.partial(_ffn_body, chunks=chunks, final=final_w is not None),
        grid=(t // tm,),
        in_specs=in_specs,
        out_specs=pl.BlockSpec((tm, d), lambda i: (i, 0)),
        out_shape=jax.ShapeDtypeStruct((t, d), F32),
        compiler_params=_params("parallel"),
        name="ffn",
    )(*args)


def _kvproj_body(m_ref, nw_ref, w_ref, o_ref):
    mn = _rms(m_ref[...], nw_ref[...]).astype(BF)
    o_ref[...] = _dot(mn, w_ref[...]).astype(BF)


def _kvproj(mem2d, mem_norm, wkv):
    n, d = mem2d.shape
    nl, _, d2 = wkv.shape
    tm = _tile(n, 512)
    return pl.pallas_call(
        _kvproj_body,
        grid=(nl, n // tm),
        in_specs=[
            pl.BlockSpec((tm, d), lambda l, i: (i, 0)),
            pl.BlockSpec((1, d), lambda l, i: (0, 0)),
            pl.BlockSpec((None, d, d2), lambda l, i: (l, 0, 0)),
        ],
        out_specs=pl.BlockSpec((None, tm, d2), lambda l, i: (l, i, 0)),
        out_shape=jax.ShapeDtypeStruct((nl, n, d2), BF),
        compiler_params=_params("parallel", "parallel"),
        name="xa_kvproj",
    )(mem2d, mem_norm, wkv)


def _xa_body(*refs, heads, mixed):
    if mixed:
        h_ref, nw_ref, wq_ref, k_ref, v_ref, wo_ref, y_ref, a_ref, wy_ref, wa_ref, o_ref = refs
        h = h_ref[...] + _dot(y_ref[...], wy_ref[...]) + _dot(a_ref[...], wa_ref[...])
    else:
        h_ref, nw_ref, wq_ref, k_ref, v_ref, wo_ref, o_ref = refs
        h = h_ref[...]
    hn = _rms(h, nw_ref[...]).astype(BF)
    dh = h.shape[-1] // heads
    q = (_dot(hn, wq_ref[...]) * (math.log2(math.e) * dh ** -0.5)).astype(BF)
    outs = []
    for i in range(heads):
        sl = slice(i * dh, (i + 1) * dh)
        s = lax.dot_general(q[:, sl], k_ref[:, sl], NT_DIMS, preferred_element_type=F32)
        p = jnp.exp2(s - jnp.max(s, axis=-1, keepdims=True))
        l = jnp.sum(p, axis=-1, keepdims=True)
        outs.append((_dot(p.astype(BF), v_ref[:, sl]) / l).astype(BF))
    o = jnp.concatenate(outs, axis=-1)
    o_ref[...] = h + _dot(o, wo_ref[...])


def _xattn(h, norm_w, wq, kv, wo, layer, batch, mix=None):
    t, d = h.shape
    s = t // batch
    nm = kv.shape[2]
    tm = _tile(s, 1024)
    ns = s // tm
    row = lambda b, i: (b * ns + i, 0)
    in_specs = [
        pl.BlockSpec((tm, d), row),
        pl.BlockSpec((None, 1, d), lambda b, i: (layer, 0, 0)),
        _resident((None, d, d), lambda b, i: (layer, 0, 0)),
        pl.BlockSpec((None, None, nm, d), lambda b, i: (layer, b, 0, 0)),
        pl.BlockSpec((None, None, nm, d), lambda b, i: (layer, b, 0, 1)),
        _resident((None, d, d), lambda b, i: (layer, 0, 0)),
    ]
    args = [h, norm_w, wq, kv, kv, wo]
    if mix is not None:
        y_ssd, o_mla, w_out, e = mix
        ny, na = y_ssd.shape[1], o_mla.shape[1]
        assert ny % na == 0
        in_specs += [
            pl.BlockSpec((tm, ny), row),
            pl.BlockSpec((tm, na), row),
            _resident((None, ny, d), lambda b, i: (e, 0, 0)),
            _resident((None, na, d), lambda b, i: (e, ny // na, 0)),
        ]
        args += [y_ssd, o_mla, w_out, w_out]
    return pl.pallas_call(
        functools.partial(_xa_body, heads=XA_HEADS, mixed=mix is not None),
        grid=(batch, ns),
        in_specs=in_specs,
        out_specs=pl.BlockSpec((tm, d), row),
        out_shape=jax.ShapeDtypeStruct((t, d), F32),
        compiler_params=_params("parallel", "parallel"),
        name="xattn",
    )(*args)


def _inproj_body(h_ref, nw_ref, w_ref, qn_ref, kvn_ref, wqa_ref, wqb_ref, wkn_ref, wvt_ref,
                 cos_ref, sin_ref, z_ref, xbc_ref, dt_ref, q_ref, k_ref, vt_ref, *, cols, scale):
    c_z, c_xbc, c_dt, c_cq, c_ckv, c_kr = cols
    u = _rms(h_ref[...], nw_ref[...]).astype(BF)
    z_ref[...] = _dot(u, w_ref[:, c_z[0]:c_z[1]]).astype(BF)
    xbc_ref[...] = _dot(u, w_ref[:, c_xbc[0]:c_xbc[1]])
    dt_ref[...] = _dot(u, w_ref[:, c_dt[0]:c_dt[1]])
    cqn = _rms(_dot(u, w_ref[:, c_cq[0]:c_cq[1]]), qn_ref[...]).astype(BF)
    ckvn = _rms(_dot(u, w_ref[:, c_ckv[0]:c_ckv[1]]), kvn_ref[...]).astype(BF)
    kr = _dot(u, w_ref[:, c_kr[0]:c_kr[1]])
    cos_t = cos_ref[...]
    sin_t = sin_ref[...]
    kp = kr[:, :LANES] * cos_t + kr[:, LANES:] * sin_t
    qa = _dot(cqn, wqa_ref[...])
    qb = _dot(cqn, wqb_ref[...])
    kn = _dot(ckvn, wkn_ref[...])
    for i in range(qa.shape[-1] // LANES):
        sl = slice(i * LANES, (i + 1) * LANES)
        q_ref[:, sl] = ((qa[:, sl] * cos_t + qb[:, sl] * sin_t) * scale).astype(BF)
        k_ref[:, sl] = (kn[:, sl] + kp).astype(BF)
    vrow = lax.broadcasted_iota(jnp.int32, (vt_ref.shape[0], 1), 0) % (2 * LANES)
    ones = jnp.where((vrow == MLA_V) | (vrow == LANES), 1.0, 0.0)
    vt = lax.dot_general(wvt_ref[...], ckvn, NT_DIMS, preferred_element_type=F32)
    vt_ref[...] = (vt + ones).astype(BF)


def _inproj(h, norm_w, w_all, cols, q_norm, kv_norm, wqa, wqb, wkn, wvt, cos_t, sin_t, layer, e):
    t, d = h.shape
    tm = _tile(t, 1024)
    wc = w_all.shape[-1]
    n_z = cols[0][1] - cols[0][0]
    n_xbc = cols[1][1] - cols[1][0]
    hq = wqa.shape[-1]
    row = lambda i: (i, 0)
    return pl.pallas_call(
        functools.partial(_inproj_body, cols=cols, scale=math.log2(math.e) * (MLA_NOPE + MLA_ROPE) ** -0.5),
        grid=(t // tm,),
        in_specs=[
            pl.BlockSpec((tm, d), row),
            pl.BlockSpec((None, 1, d), lambda i: (layer, 0, 0)),
            _resident((None, d, wc), lambda i: (e, 0, 0)),
            pl.BlockSpec((None, 1, MLA_Q_RANK), lambda i: (e, 0, 0)),
            pl.BlockSpec((None, 1, MLA_KV_RANK), lambda i: (e, 0, 0)),
            _resident((None, MLA_Q_RANK, hq), lambda i: (e, 0, 0)),
            _resident((None, MLA_Q_RANK, hq), lambda i: (e, 0, 0)),
            _resident((None, MLA_KV_RANK, hq), lambda i: (e, 0, 0)),
            _resident((None, hq, MLA_KV_RANK), lambda i: (e, 0, 0)),
            pl.BlockSpec((tm, LANES), row),
            pl.BlockSpec((tm, LANES), row),
        ],
        out_specs=[
            pl.BlockSpec((tm, n_z), row),
            pl.BlockSpec((tm, n_xbc), row),
            pl.BlockSpec((tm, LANES), row),
            pl.BlockSpec((tm, hq), row),
            pl.BlockSpec((tm, hq), row),
            pl.BlockSpec((hq, tm), lambda i: (0, i)),
        ],
        out_shape=[
            jax.ShapeDtypeStruct((t, n_z), BF),
            jax.ShapeDtypeStruct((t, n_xbc), F32),
            jax.ShapeDtypeStruct((t, LANES), F32),
            jax.ShapeDtypeStruct((t, hq), BF),
            jax.ShapeDtypeStruct((t, hq), BF),
            jax.ShapeDtypeStruct((hq, t), BF),
        ],
        compiler_params=_params("parallel"),
        name="mix_inproj",
    )(h, norm_w, w_all, q_norm, kv_norm, wqa, wqb, wkn, wvt, cos_t, sin_t)


def _split3_dot(a_bf, x):
    hi = x.astype(BF)
    r1 = x - hi.astype(F32)
    mid = r1.astype(BF)
    low = (r1 - mid.astype(F32)).astype(BF)
    return _dot(a_bf, hi) + _dot(a_bf, mid) + _dot(a_bf, low)


def _ssd_decay_stage(z, r0, dt_ref, bias_ref, a_ref, sel_ref, rows_ref, dec_ref, wall_ref):
    q = SSD_CHUNK
    nh = SSD_HEADS
    dt = jax.nn.softplus(dt_ref[pl.ds(r0, q), :] + bias_ref[...])
    la = dt * (a_ref[...] * math.log2(math.e))
    row = lax.broadcasted_iota(jnp.int32, (q, q), 0)
    col = lax.broadcasted_iota(jnp.int32, (q, q), 1)
    tril = jnp.where(row >= col, 1.0, 0.0).astype(BF)
    cum = _split3_dot(tril, la)
    tot = cum[q - 1:q, :]
    rev = tot - cum + la
    fwd_lane = lax.broadcasted_iota(jnp.int32, (q, LANES), 1) < nh
    sel = jnp.where(fwd_lane, cum, rev)
    sel_ref[z] = sel
    dt_t = dt.T
    rows_ref[z, 0:2 * nh, :] = (sel - jnp.log2(dt)).T[0:2 * nh]
    rows_ref[z, 2 * nh:3 * nh, :] = jnp.log2(dt_t[0:nh] + dt_t[nh:2 * nh])
    wall_ref[z] = (jnp.exp2(tot - sel) * dt).astype(BF)
    etot = jnp.exp2(tot)
    lo1 = lax.broadcasted_iota(jnp.int32, (1, LANES), 1) < SSD_HEAD_DIM
    for d in range(2):
        for p in range(nh // 2):
            h0 = d * nh + 2 * p
            dec = jnp.where(lo1, etot[:, h0:h0 + 1], etot[:, h0 + 1:h0 + 2])
            dec_ref[z, d, p] = jnp.broadcast_to(dec, dec_ref.shape[3:])


def _ssd_conv_stage(z, j, first, last, xc_ref, xp_ref, xn_ref, cw_ref, cb_ref, sm_ref, x_ref, b_ref, c_ref):
    q = SSD_CHUNK
    inner = SSD_HEADS * SSD_HEAD_DIM
    gn = SSD_GROUPS * SSD_STATE
    half = SSD_CONV // 2
    edge = xp_ref.shape[0]
    nsub = xc_ref.shape[0] // q
    r0 = pl.multiple_of(j * q, q)
    cur = xc_ref[pl.ds(r0, q), :]
    before = xc_ref[pl.ds(pl.multiple_of(jnp.maximum(r0 - edge, 0), edge), edge), :]
    after = xc_ref[pl.ds(pl.multiple_of(jnp.minimum(r0 + q, (nsub - 1) * q + q - edge), edge), edge), :]
    prev = jnp.where(first, 0.0, jnp.where(j == 0, xp_ref[...], before))
    nxt = jnp.where(last, 0.0, jnp.where(j == nsub - 1, xn_ref[...], after))
    fill = jnp.zeros((sm_ref.shape[1] - q - 2 * edge, cur.shape[1]), F32)
    window = jnp.concatenate([prev, cur, nxt, fill], axis=0).astype(BF)
    shifted = _dot(sm_ref[...], window)
    acc = cur * cw_ref[half:half + 1, :] + cb_ref[...]
    blk = 0
    for k in range(SSD_CONV):
        if k != half:
            acc = acc + shifted[blk * q:(blk + 1) * q] * cw_ref[k:k + 1, :]
            blk += 1
    xbc = jax.nn.silu(acc)
    x_ref[z] = xbc[:, :inner].astype(BF)
    for g in range(SSD_GROUPS):
        b_ref[z, g] = xbc[:, inner + g * SSD_STATE:inner + (g + 1) * SSD_STATE].T.astype(BF)
    c_ref[z] = xbc[:, inner + gn:].astype(BF)


def _ssd_state_stage(z, e_ref, st_ref, x_ref, b_ref, wall_ref, dec_ref, carry_ref):
    nh, n = SSD_HEADS, SSD_STATE
    inner = nh * SSD_HEAD_DIM
    gw = inner // SSD_GROUPS
    ppg = gw // LANES
    x = x_ref[z].astype(F32)
    wexp = _dot(wall_ref[z], e_ref[...])
    for d in range(2):
        xw = (x * wexp[:, d * inner:(d + 1) * inner]).astype(BF)
        for g in range(SSD_GROUPS):
            upd = _dot(b_ref[z, g], xw[:, g * gw:(g + 1) * gw])
            for i in range(ppg):
                p = g * ppg + i
                contrib = upd[:, i * LANES:(i + 1) * LANES]
                if d == 0:
                    state = carry_ref[0, p]
                    st_ref[z, 0, p] = state
                    carry_ref[0, p] = state * dec_ref[z, 0, p][0:1] + contrib
                else:
                    st_ref[z, 1, p] = contrib


def _ssd_output_phase(z, j, z_ref, d_ref, nw_ref, o_ref, st_ref, x_ref, b_ref, c_ref, sel_ref, rows_ref,
                      dec_ref, carry_ref):
    q = SSD_CHUNK
    r0 = pl.multiple_of(j * q, q)
    nh, n, hpg = SSD_HEADS, SSD_STATE, SSD_HEADS // SSD_GROUPS
    sel = sel_ref[z]
    rows = rows_ref[z]
    row = lax.broadcasted_iota(jnp.int32, (q, q), 0)
    col = lax.broadcasted_iota(jnp.int32, (q, q), 1)
    lower = row > col
    diag = row == col
    lo = lax.broadcasted_iota(jnp.int32, (q, LANES), 1) < SSD_HEAD_DIM
    ys = []
    for g in range(SSD_GROUPS):
        cg = c_ref[z, :, g * n:(g + 1) * n]
        cb = _dot(cg, b_ref[z, g])
        cg32 = cg.astype(F32)
        for j in range(hpg // 2):
            p = g * (hpg // 2) + j
            lhs = []
            for h in (2 * p, 2 * p + 1):
                hb = nh + h
                a_f = jnp.broadcast_to(sel[:, h:h + 1], (q, q))
                a_b = jnp.broadcast_to(sel[:, hb:hb + 1], (q, q))
                seg = jnp.where(lower, a_f - rows[h:h + 1, :],
                                jnp.where(diag, rows[2 * nh + h:2 * nh + h + 1, :], a_b - rows[hb:hb + 1, :]))
                m = (cb * jnp.exp2(seg)).astype(BF)
                cef = (cg32 * jnp.exp2(a_f)).astype(BF)
                ceb = (cg32 * jnp.exp2(a_b)).astype(BF)
                lhs.append(jnp.concatenate([m, cef, ceb], axis=1))
            xp = x_ref[z, :, p * LANES:(p + 1) * LANES]
            back = carry_ref[1, p]
            carry_ref[1, p] = back * dec_ref[z, 1, p][0:1] + st_ref[z, 1, p]
            rhs = jnp.concatenate([xp, st_ref[z, 0, p].astype(BF), back.astype(BF)], axis=0)
            out = _dot(jnp.concatenate(lhs, axis=0), rhs)
            ys.append(jnp.where(lo, out[:q], out[q:]) + xp.astype(F32) * d_ref[:, p * LANES:(p + 1) * LANES])
    y = jnp.concatenate(ys, axis=-1)
    gated = y * jax.nn.silu(z_ref[pl.ds(r0, q), :].astype(F32))
    o_ref[pl.ds(r0, q), :] = _rms(gated, nw_ref[...]).astype(BF)


def _ssd_body(xc_ref, xp_ref, xn_ref, dt_ref, z_ref, cw_ref, cb_ref, bias_ref, a_ref, d_ref, nw_ref, e_ref, sm_ref,
              o_ref, st_ref, x_ref, b_ref, c_ref, sel_ref, rows_ref, dec_ref, wall_ref, carry_ref):
    t = pl.program_id(1)
    nc = st_ref.shape[0]
    nsub = xc_ref.shape[0] // SSD_CHUNK
    nb = nc // nsub

    @pl.when(t == 0)
    def _():
        carry_ref[0] = jnp.zeros(carry_ref.shape[1:], F32)

    @pl.when(t == nb)
    def _():
        carry_ref[1] = jnp.zeros(carry_ref.shape[1:], F32)

    @pl.when(t < nb)
    def _():
        base = t * nsub
        for j in range(nsub):
            _ssd_decay_stage(base + j, j * SSD_CHUNK, dt_ref, bias_ref, a_ref, sel_ref, rows_ref, dec_ref, wall_ref)

        def conv(j):
            z = base + j
            _ssd_conv_stage(z, j, z == 0, z == nc - 1, xc_ref, xp_ref, xn_ref, cw_ref, cb_ref, sm_ref,
                            x_ref, b_ref, c_ref)

        def state(j):
            _ssd_state_stage(base + j, e_ref, st_ref, x_ref, b_ref, wall_ref, dec_ref, carry_ref)

        def sub(j, carry):
            conv(j)
            state(j - 1)
            return carry

        conv(0)
        lax.fori_loop(1, nsub, sub, 0)
        state(nsub - 1)

    @pl.when(t >= nb)
    def _():
        def sub(i, carry):
            j = nsub - 1 - i
            z = (2 * nb - 1 - t) * nsub + j
            _ssd_output_phase(z, j, z_ref, d_ref, nw_ref, o_ref, st_ref, x_ref, b_ref, c_ref, sel_ref,
                              rows_ref, dec_ref, carry_ref)
            return carry
        lax.fori_loop(0, nsub, sub, 0)


def _conv_shift_table(q, edge, rows):
    half = SSD_CONV // 2
    m = np.zeros(((SSD_CONV - 1) * q, rows), np.float32)
    blk = 0
    for k in range(SSD_CONV):
        if k != half:
            m[blk * q + np.arange(q), edge + np.arange(q) + k - half] = 1.0
            blk += 1
    return jnp.asarray(m, BF)


def _head_expand_table():
    inner = SSD_HEADS * SSD_HEAD_DIM
    e = np.zeros((LANES, 2 * inner), np.float32)
    for h in range(2 * SSD_HEADS):
        e[h, h * SSD_HEAD_DIM:(h + 1) * SSD_HEAD_DIM] = 1.0
    return jnp.asarray(e, BF)


def _ssd(xbc, dt_raw, z, conv_w, conv_b, dt_bias, a_neg, d_exp, ssd_norm, e, batch):
    t, c = xbc.shape
    s = t // batch
    inner = SSD_HEADS * SSD_HEAD_DIM
    gn = SSD_GROUPS * SSD_STATE
    q = SSD_CHUNK
    nc = s // q
    nsub = next(n for n in (8, 4, 2, 1) if nc % n == 0)
    nb = nc // nsub
    rows = nsub * q
    edge = 8
    epb = rows // edge
    blk1 = lambda i: jnp.minimum(i, nb - 1)
    blk3 = lambda i: nb - 1 - jnp.maximum(i - nb, 0)
    par = lambda b, i: (e, 0, 0)
    window_rows = 2 * q
    return pl.pallas_call(
        _ssd_body,
        grid=(batch, 2 * nb),
        in_specs=[
            pl.BlockSpec((rows, c), lambda b, i: (b * nb + blk1(i), 0)),
            pl.BlockSpec((edge, c), lambda b, i: (b * nb * epb + jnp.maximum(blk1(i) * epb - 1, 0), 0)),
            pl.BlockSpec((edge, c), lambda b, i: (b * nb * epb + jnp.minimum((blk1(i) + 1) * epb, nb * epb - 1), 0)),
            pl.BlockSpec((rows, LANES), lambda b, i: (b * nb + blk1(i), 0)),
            pl.BlockSpec((rows, inner), lambda b, i: (b * nb + blk3(i), 0)),
            pl.BlockSpec((None, SSD_CONV, c), par),
            pl.BlockSpec((None, 1, c), par),
            pl.BlockSpec((None, 1, LANES), par),
            pl.BlockSpec((None, 1, LANES), par),
            pl.BlockSpec((None, 1, inner), par),
            pl.BlockSpec((None, 1, inner), par),
            _resident((LANES, 2 * inner), lambda b, i: (0, 0)),
            _resident(((SSD_CONV - 1) * q, window_rows), lambda b, i: (0, 0)),
        ],
        out_specs=pl.BlockSpec((rows, inner), lambda b, i: (b * nb + blk3(i), 0)),
        out_shape=jax.ShapeDtypeStruct((t, inner), BF),
        scratch_shapes=[
            pltpu.VMEM((nc, 2, SSD_HEADS // 2, SSD_STATE, LANES), F32),
            pltpu.VMEM((nc, q, inner), BF),
            pltpu.VMEM((nc, SSD_GROUPS, SSD_STATE, q), BF),
            pltpu.VMEM((nc, q, gn), BF),
            pltpu.VMEM((nc, q, LANES), F32),
            pltpu.VMEM((nc, 3 * SSD_HEADS, q), F32),
            pltpu.VMEM((nc, 2, SSD_HEADS // 2, edge, LANES), F32),
            pltpu.VMEM((nc, q, LANES), BF),
            pltpu.VMEM((2, SSD_HEADS // 2, SSD_STATE, LANES), F32),
        ],
        compiler_params=_params("parallel", "arbitrary"),
        name="ssd",
    )(xbc, xbc, xbc, dt_raw, z, conv_w, conv_b, dt_bias, a_neg, d_exp, ssd_norm, _head_expand_table(),
      _conv_shift_table(q, edge, window_rows))


def _lane_fold(x, op):
    out = x[:, :LANES]
    for j in range(1, x.shape[-1] // LANES):
        out = op(out, x[:, j * LANES:(j + 1) * LANES])
    return out


def _mla_body(q_ref, k_ref, vt_ref, o_ref, s_ref, m_ref, *, kt):
    nk = k_ref.shape[0]
    rb = s_ref.shape[2]
    nrb = q_ref.shape[0] // rb
    nheads = q_ref.shape[1] // LANES

    def score_pass(r, head):
        rows = pl.ds(pl.multiple_of(r * rb, rb), rb)
        sl = slice(head * LANES, (head + 1) * LANES)
        slot = head % 2
        s = lax.dot_general(k_ref[:, sl], q_ref[rows, sl], NT_DIMS, preferred_element_type=F32)
        s_ref[slot] = s
        m_ref[slot] = jnp.max(s, axis=0, keepdims=True)

    def value_pass(head):
        hs = slice(head * LANES, (head + 1) * LANES)
        slot = head % 2
        m = m_ref[slot]
        o = jnp.zeros((LANES, rb), F32)
        for c in range(0, nk, kt):
            p = jnp.exp2(s_ref[slot, c:c + kt, :] - m)
            o = o + _dot(vt_ref[hs, c:c + kt], p.astype(BF))
        return o

    def row_block(r, carry):
        top = lax.broadcasted_iota(jnp.int32, (LANES, rb), 0) < MLA_V
        outs = []
        for head in range(nheads):
            if head + 1 < nheads:
                score_pass(r, head + 1)
            else:
                score_pass(jnp.minimum(r + 1, nrb - 1), 0)
            outs.append(value_pass(head))
        pairs = [jnp.where(top, o0 / o0[MLA_V:MLA_V + 1, :], o1 / o1[0:1, :]).T
                 for o0, o1 in zip(outs[0::2], outs[1::2])]
        o_ref[pl.ds(pl.multiple_of(r * rb, rb), rb), :] = jnp.concatenate(pairs, axis=-1).astype(BF)
        return carry

    score_pass(0, 0)
    lax.fori_loop(0, nrb, row_block, 0)


def _mla(q, k, vt, batch):
    t, hq = q.shape
    s = t // batch
    rb = _tile(s, 256)
    nv = MLA_HEADS * MLA_V
    q3, k3 = (a.reshape(batch, s, hq) for a in (q, k))
    blk = pl.BlockSpec((None, s, hq), lambda b: (b, 0, 0))
    o = pl.pallas_call(
        functools.partial(_mla_body, kt=_tile(s, 256)),
        grid=(batch,),
        in_specs=[blk, blk, pl.BlockSpec((hq, s), lambda b: (0, b))],
        out_specs=pl.BlockSpec((None, s, nv), lambda b: (b, 0, 0)),
        out_shape=jax.ShapeDtypeStruct((batch, s, nv), BF),
        scratch_shapes=[pltpu.VMEM((2, s, rb), F32), pltpu.VMEM((2, 1, rb), F32)],
        compiler_params=_params("parallel"),
        name="mla_attn",
    )(q3, k3, vt)
    return o.reshape(t, nv)


def _fnet_fold_body(ha_ref, hm_ref, hx_ref, nw_ref, cc_ref, sc_ref, ec_ref, es_ref):
    tm = ha_ref.shape[0]
    nw = nw_ref[...]
    u_a = _rms(ha_ref[...], nw)
    u_m = _rms(hm_ref[...], nw)
    u_x = _rms(hx_ref[...], nw)[0:1]
    r = lax.broadcasted_iota(jnp.int32, (tm, tm), 0)
    c = lax.broadcasted_iota(jnp.int32, (tm, tm), 1)
    perm = jnp.where(r + c == tm, 1.0, 0.0).astype(BF)
    hi = u_m.astype(BF)
    low = (u_m - hi.astype(F32)).astype(BF)
    mirror = _dot(perm, hi) + _dot(perm, low)
    first = lax.broadcasted_iota(jnp.int32, u_a.shape, 0) == 0
    mirror = jnp.where(first, u_x, mirror)
    ue = (u_a + mirror).astype(BF)
    uo = (u_a - mirror).astype(BF)
    gc = cc_ref.shape[0]
    for g in range(ue.shape[-1] // gc):
        sl = slice(g * gc, (g + 1) * gc)
        ec_ref[:, sl] = _dot(ue[:, sl], cc_ref[...]).astype(BF)
        es_ref[:, sl] = _dot(uo[:, sl], sc_ref[...]).astype(BF)


def _fnet_seq_body(h_ref, hh_ref, nw_ref, cc_ref, cs_ref, ss_ref, ec_ref, es_ref, w_ref, o_ref, *, scale):
    ts = h_ref.shape[0]
    rows = pl.ds(pl.multiple_of(pl.program_id(1) * ts, ts), ts)
    y = _dot(cs_ref[rows, :], ec_ref[...]) - _dot(ss_ref[rows, :], es_ref[...])
    u_h = _rms(hh_ref[...], nw_ref[...]).astype(BF)
    gc = cc_ref.shape[0]
    x_h = jnp.concatenate([_dot(u_h[:, g * gc:(g + 1) * gc], cc_ref[...])
                           for g in range(u_h.shape[-1] // gc)], axis=-1)[0:1]
    odd = lax.broadcasted_iota(jnp.int32, (ts, 1), 0) % 2 == 1
    y = y + jnp.where(odd, -1.0, 1.0) * x_h
    o_ref[...] = h_ref[...] + _dot((y * scale).astype(BF), w_ref[...])


def _dft_tables(n, fold):
    j = np.arange(n)[:, None]
    k = np.arange(n // 2 if fold else n)[None, :]
    ang = ((j * k) % n) * (2.0 * np.pi / n)
    cos, sin = np.cos(ang), np.sin(ang)
    if fold:
        cos[:, 0] = 0.5
    return jnp.asarray(cos, BF), jnp.asarray(sin, BF)


def _fnet(h, norm_w, w_out, layer, o, batch, tables):
    t, d = h.shape
    s = t // batch
    cc, sc, cs, ss = tables
    gc = cc.shape[0]
    sub = 8
    tm = _tile(s // 2, 256)
    nt, nf = s // tm, s // 2 // tm
    nwspec = pl.BlockSpec((None, 1, d), lambda b, i: (layer, 0, 0))
    table = _resident((gc, gc), lambda b, i: (0, 0))
    ec, es = pl.pallas_call(
        _fnet_fold_body,
        grid=(batch, nf),
        in_specs=[
            pl.BlockSpec((tm, d), lambda b, i: (b * nt + i, 0)),
            pl.BlockSpec((tm, d), lambda b, i: (b * nt + nt - 1 - i, 0)),
            pl.BlockSpec((sub, d), lambda b, i: (b * (s // sub) + ((nt - i) % nt) * (tm // sub), 0)),
            nwspec, table, table,
        ],
        out_specs=[pl.BlockSpec((tm, d), lambda b, i: (b * nf + i, 0))] * 2,
        out_shape=[jax.ShapeDtypeStruct((t // 2, d), BF)] * 2,
        compiler_params=_params("parallel", "parallel"),
        name="fnet_fold_channel_dft",
    )(h, h, h, norm_w, cc, sc)
    ts = _tile(s, 512)
    ns = s // ts
    return pl.pallas_call(
        functools.partial(_fnet_seq_body, scale=(s * gc) ** -0.5),
        grid=(batch, ns),
        in_specs=[
            pl.BlockSpec((ts, d), lambda b, i: (b * ns + i, 0)),
            pl.BlockSpec((sub, d), lambda b, i: (b * (s // sub) + s // 2 // sub, 0)),
            nwspec, table,
            _resident((s, s // 2), lambda b, i: (0, 0)),
            _resident((s, s // 2), lambda b, i: (0, 0)),
            pl.BlockSpec((s // 2, d), lambda b, i: (b, 0)),
            pl.BlockSpec((s // 2, d), lambda b, i: (b, 0)),
            _resident((None, d, d), lambda b, i: (o, 0, 0)),
        ],
        out_specs=pl.BlockSpec((ts, d), lambda b, i: (b * ns + i, 0)),
        out_shape=jax.ShapeDtypeStruct((t, d), F32),
        compiler_params=_params("parallel", "arbitrary"),
        name="fnet_seq_dft",
    )(h, h, norm_w, cc, cs, ss, ec, es, w_out)


def _mixer_weights(w_in, w_uq, w_ukv, dt_bias, a_log, ssd_d):
    w_in, w_uq, w_ukv = w_in.astype(BF), w_uq.astype(BF), w_ukv.astype(BF)
    ne, d, _ = w_in.shape
    inner = SSD_HEADS * SSD_HEAD_DIM
    conv_ch = inner + 2 * SSD_GROUPS * SSD_STATE
    o_z, o_xbc = 0, inner
    o_dt = o_xbc + conv_ch
    o_cq = o_dt + 2 * SSD_HEADS
    o_ckv = o_cq + MLA_Q_RANK
    o_kr = o_ckv + MLA_KV_RANK
    half = MLA_ROPE // 2
    pad = LANES - MLA_NOPE - MLA_ROPE
    zeros = lambda *s: jnp.zeros(s, w_in.dtype)
    w_dt = jnp.concatenate([w_in[:, :, o_dt:o_cq], zeros(ne, d, LANES - 2 * SSD_HEADS)], axis=-1)
    kr1 = w_in[:, :, o_kr:o_kr + half]
    kr2 = w_in[:, :, o_kr + half:o_kr + MLA_ROPE]
    kr_a = jnp.concatenate([zeros(ne, d, MLA_NOPE), kr1, kr2, zeros(ne, d, pad)], axis=-1)
    kr_b = jnp.concatenate([zeros(ne, d, MLA_NOPE), -kr2, kr1, zeros(ne, d, pad)], axis=-1)
    pieces = [w_in[:, :, o_z:o_xbc], w_in[:, :, o_xbc:o_dt], w_dt, w_in[:, :, o_cq:o_ckv],
              w_in[:, :, o_ckv:o_kr], jnp.concatenate([kr_a, kr_b], axis=-1)]
    cols, c = [], 0
    for p in pieces:
        cols.append((c, c + p.shape[-1]))
        c += p.shape[-1]
    w_all = jnp.concatenate(pieces, axis=-1).astype(BF)

    uq = w_uq.reshape(ne, MLA_Q_RANK, MLA_HEADS, MLA_NOPE + MLA_ROPE)
    q_nope, q1, q2 = uq[..., :MLA_NOPE], uq[..., MLA_NOPE:MLA_NOPE + half], uq[..., MLA_NOPE + half:]
    zq = lambda n: jnp.zeros((ne, MLA_Q_RANK, MLA_HEADS, n), w_uq.dtype)
    hq = MLA_HEADS * LANES
    wqa = jnp.concatenate([q_nope, q1, q2, zq(pad)], axis=-1).reshape(ne, MLA_Q_RANK, hq).astype(BF)
    wqb = jnp.concatenate([zq(MLA_NOPE), -q2, q1, zq(pad)], axis=-1).reshape(ne, MLA_Q_RANK, hq).astype(BF)

    ukv = w_ukv.reshape(ne, MLA_KV_RANK, MLA_HEADS, MLA_NOPE + MLA_V)
    zkv = lambda *s: jnp.zeros((ne, MLA_KV_RANK) + s, w_ukv.dtype)
    wkn = jnp.concatenate([ukv[..., :MLA_NOPE], zkv(MLA_HEADS, LANES - MLA_NOPE)], axis=-1)
    wkn = wkn.reshape(ne, MLA_KV_RANK, hq).astype(BF)
    vv = ukv[..., MLA_NOPE:].reshape(ne, MLA_KV_RANK, MLA_HEADS // 2, 2, MLA_V)
    zv = zkv(MLA_HEADS // 2, MLA_V)
    wv = jnp.stack([jnp.concatenate([vv[:, :, :, 0], zv], axis=-1),
                    jnp.concatenate([zv, vv[:, :, :, 1]], axis=-1)], axis=3)
    wvt = jnp.swapaxes(wv.reshape(ne, MLA_KV_RANK, hq), 1, 2).astype(BF)

    padl = lambda a: jnp.concatenate([a, jnp.zeros((ne, LANES - a.shape[-1]), a.dtype)], axis=-1)[:, None, :]
    bias = padl(dt_bias.reshape(ne, 2 * SSD_HEADS))
    a_neg = padl(-jnp.exp(a_log.reshape(ne, 2 * SSD_HEADS)))
    d_exp = jnp.repeat(ssd_d, SSD_HEAD_DIM, axis=-1)[:, None, :]
    return w_all, tuple(cols), wqa, wqb, wkn, wvt, bias, a_neg, d_exp


def _rope_tables(positions):
    inv = 1.0 / (ROPE_THETA ** (jnp.arange(0, MLA_ROPE, 2, dtype=F32) / MLA_ROPE))
    ang = inv[:, None] * positions.astype(F32).reshape(1, -1)
    cos, sin = jnp.cos(ang), jnp.sin(ang)
    t = cos.shape[1]
    pad = LANES - MLA_NOPE - MLA_ROPE
    cos_t = jnp.concatenate([jnp.ones((MLA_NOPE, t), F32), cos, cos, jnp.zeros((pad, t), F32)], axis=0).T
    sin_t = jnp.concatenate([jnp.zeros((MLA_NOPE, t), F32), sin, sin, jnp.zeros((pad, t), F32)], axis=0).T
    return cos_t, sin_t


def kernel(x, mem, positions, mem_norm, final_norm, ffn1_norm, ffn1_w_gu, ffn1_w_down, mix_norm, xa_norm,
           xa_wq, xa_wkv, xa_wo, ffn2_norm, ffn2_w_gu, ffn2_w_down, w_in, conv_w, conv_b, dt_bias, a_log,
           ssd_d, ssd_norm, q_norm, w_uq, kv_norm, w_ukv, w_out, fnet_w_out):
    batch, seq, d = x.shape
    depth = ffn1_norm.shape[0]
    t = batch * seq
    bf = lambda a: a.astype(BF)
    row3 = lambda a: a[:, None, :]

    kv = _kvproj(mem.reshape(-1, d), mem_norm[None, :], bf(xa_wkv)).reshape(depth, batch, mem.shape[1], 2 * d)
    w_all, cols, wqa, wqb, wkn, wvt, bias, a_neg, d_exp = _mixer_weights(w_in, w_uq, w_ukv, dt_bias, a_log, ssd_d)
    cos_t, sin_t = _rope_tables(positions)
    gc = d // FNET_GROUPS
    tables = _dft_tables(gc, fold=False) + _dft_tables(seq, fold=True)
    f1n, f2n, mxn, xan = row3(ffn1_norm), row3(ffn2_norm), row3(mix_norm), row3(xa_norm)
    f1gu, f1d, f2gu, f2d = bf(ffn1_w_gu), bf(ffn1_w_down), bf(ffn2_w_gu), bf(ffn2_w_down)
    wq, wo, w_mix_out, w_fnet = bf(xa_wq), bf(xa_wo), bf(w_out), bf(fnet_w_out)
    ssd_nw, qn, kvn = row3(ssd_norm), row3(q_norm), row3(kv_norm)

    h = x.reshape(t, d)
    for layer in range(depth):
        h = _ffn(h, f1n, f1gu, f1d, layer)
        if layer % 2 == 0:
            e = layer // 2
            z, xbc, dt_raw, q, k, vt = _inproj(h, mxn, w_all, cols, qn, kvn, wqa, wqb, wkn, wvt,
                                              cos_t, sin_t, layer, e)
            y_ssd = _ssd(xbc, dt_raw, z, conv_w, row3(conv_b), bias, a_neg, d_exp, ssd_nw, e, batch)
            o_mla = _mla(q, k, vt, batch)
            mix = (y_ssd, o_mla, w_mix_out, e)
        else:
            h = _fnet(h, mxn, w_fnet, layer, layer // 2, batch, tables)
            mix = None
        h = _xattn(h, xan, wq, kv, wo, layer, batch, mix=mix)
        h = _ffn(h, f2n, f2gu, f2d, layer, final_w=final_norm[None, :] if layer == depth - 1 else None)
    return h.reshape(batch, seq, d)
```

```python
import functools
import math

import numpy as np
import jax
import jax.numpy as jnp
from jax import lax
from jax.experimental import pallas as pl
from jax.experimental.pallas import tpu as pltpu

EPS = 1e-6
BF = jnp.bfloat16
F32 = jnp.float32

V7X_VMEM_BYTES = 64 * 1024 * 1024
VMEM_LIMIT = V7X_VMEM_BYTES - 8 * 1024 * 1024
LANES = 128

SSD_HEADS = 16
SSD_HEAD_DIM = 64
SSD_GROUPS = 2
SSD_STATE = 128
SSD_CONV = 5
SSD_CHUNK = 128
MLA_HEADS = 8
MLA_Q_RANK = 512
MLA_KV_RANK = 256
MLA_NOPE = 64
MLA_ROPE = 32
MLA_V = 64
ROPE_THETA = 10000.0
FNET_GROUPS = 4
XA_HEADS = 4

NT_DIMS = (((1,), (1,)), ((), ()))
TN_DIMS = (((0,), (0,)), ((), ()))


def _params(*sem):
    return pltpu.CompilerParams(dimension_semantics=sem, vmem_limit_bytes=VMEM_LIMIT)


def _resident(shape, index_map):
    return pl.BlockSpec(shape, index_map, pipeline_mode=pl.Buffered(1))


def _rms(x, w):
    return x * lax.rsqrt(jnp.mean(x * x, axis=-1, keepdims=True) + EPS) * w


def _dot(a, b):
    return jnp.dot(a, b, preferred_element_type=F32)


def _tile(n, pref):
    t = min(n, pref)
    assert n % t == 0, (n, t)
    return t


def _ffn_body(*refs, chunks, final, cast):
    refs = list(refs)
    h_ref, nw_ref, wg_ref, wu_ref, wd_ref = refs[:5]
    del refs[:5]
    fw_ref = refs.pop(0) if final else None
    if cast:
        gu_src, dn_src = refs.pop(0), refs.pop(0)
        o_ref, gu_dst, dn_dst = refs
        gu_dst[...] = gu_src[...].astype(BF)
        dn_dst[...] = dn_src[...].astype(BF)
    else:
        (o_ref,) = refs
    h = h_ref[...]
    xn = _rms(h, nw_ref[...]).astype(BF)
    acc = jnp.zeros(h.shape, F32)
    for a, b in chunks:
        g = _dot(xn, wg_ref[:, a:b])
        u = _dot(xn, wu_ref[:, a:b])
        act = (jax.nn.silu(g) * u).astype(BF)
        acc = acc + _dot(act, wd_ref[a:b, :])
    out = h + 0.5 * acc
    if final:
        out = _rms(out, fw_ref[...])
    o_ref[...] = out


def _ffn(h, norm_w, w_gu, w_down, layer, final_w=None, cast_next=None):
    t, d = h.shape
    f = w_down.shape[0]
    tm = _tile(t, 1024)
    steps = t // tm
    step = 768
    chunks = tuple((a, min(a + step, f)) for a in range(0, f, step))
    row = lambda i: (i, 0)
    in_specs = [
        pl.BlockSpec((tm, d), row),
        pl.BlockSpec((None, 1, d), lambda i: (layer, 0, 0)),
        _resident((d, f), lambda i: (0, 0)),
        _resident((d, f), lambda i: (0, 1)),
        _resident((f, d), lambda i: (0, 0)),
    ]
    args = [h, norm_w, w_gu, w_gu, w_down]
    out_specs = [pl.BlockSpec((tm, d), row)]
    out_shape = [jax.ShapeDtypeStruct((t, d), F32)]
    if final_w is not None:
        in_specs.append(pl.BlockSpec((1, d), lambda i: (0, 0)))
        args.append(final_w)
    if cast_next is not None:
        src_gu, src_dn, nxt = cast_next
        nblk = next(n for n in (16, 8, 4, 2, 1) if steps % n == 0)
        rep = steps // nblk
        assert d % (16 * nblk) == 0 and f % (16 * nblk) == 0
        in_specs += [pl.BlockSpec((None, d // nblk, 2 * f), lambda i: (nxt, i // rep, 0)),
                     pl.BlockSpec((None, f // nblk, d), lambda i: (nxt, i // rep, 0))]
        args += [src_gu, src_dn]
        out_specs += [pl.BlockSpec((d // nblk, 2 * f), lambda i: (i // rep, 0)),
                      pl.BlockSpec((f // nblk, d), lambda i: (i // rep, 0))]
        out_shape += [jax.ShapeDtypeStruct((d, 2 * f), BF), jax.ShapeDtypeStruct((f, d), BF)]
    outs = pl.pallas_call(
        functools.partial(_ffn_body, chunks=chunks, final=final_w is not None, cast=cast_next is not None),
        grid=(steps,),
        in_specs=in_specs,
        out_specs=out_specs,
        out_shape=out_shape,
        compiler_params=_params("arbitrary" if cast_next is not None else "parallel"),
        name="ffn",
    )(*args)
    return outs if cast_next is not None else outs[0]


def _kvproj_body(m_ref, nw_ref, w_ref, o_ref):
    mn = _rms(m_ref[...], nw_ref[...]).astype(BF)
    o_ref[...] = _dot(mn, w_ref[...]).astype(BF)


def _kvproj(mem2d, mem_norm, wkv):
    n, d = mem2d.shape
    nl, _, d2 = wkv.shape
    tm = _tile(n, 512)
    return pl.pallas_call(
        _kvproj_body,
        grid=(nl, n // tm),
        in_specs=[
            pl.BlockSpec((tm, d), lambda l, i: (i, 0)),
            pl.BlockSpec((1, d), lambda l, i: (0, 0)),
            pl.BlockSpec((None, d, d2), lambda l, i: (l, 0, 0)),
        ],
        out_specs=pl.BlockSpec((None, tm, d2), lambda l, i: (l, i, 0)),
        out_shape=jax.ShapeDtypeStruct((nl, n, d2), BF),
        compiler_params=_params("parallel", "parallel"),
        name="xa_kvproj",
    )(mem2d, mem_norm, wkv)


def _xa_body(*refs, heads, mixed):
    if mixed:
        h_ref, nw_ref, wq_ref, k_ref, v_ref, wo_ref, y_ref, a_ref, wy_ref, wa_ref, o_ref = refs
        h = h_ref[...] + _dot(y_ref[...], wy_ref[...]) + _dot(a_ref[...], wa_ref[...])
    else:
        h_ref, nw_ref, wq_ref, k_ref, v_ref, wo_ref, o_ref = refs
        h = h_ref[...]
    hn = _rms(h, nw_ref[...]).astype(BF)
    dh = h.shape[-1] // heads
    q = (_dot(hn, wq_ref[...]) * (math.log2(math.e) * dh ** -0.5)).astype(BF)
    outs = []
    for i in range(heads):
        sl = slice(i * dh, (i + 1) * dh)
        s = lax.dot_general(q[:, sl], k_ref[:, sl], NT_DIMS, preferred_element_type=F32)
        p = jnp.exp2(s - jnp.max(s, axis=-1, keepdims=True))
        l = jnp.sum(p, axis=-1, keepdims=True)
        outs.append((_dot(p.astype(BF), v_ref[:, sl]) / l).astype(BF))
    o = jnp.concatenate(outs, axis=-1)
    o_ref[...] = h + _dot(o, wo_ref[...])


def _xattn(h, norm_w, wq, kv, wo, layer, batch, mix=None):
    t, d = h.shape
    s = t // batch
    nm = kv.shape[2]
    tm = _tile(s, 1024)
    ns = s // tm
    row = lambda b, i: (b * ns + i, 0)
    in_specs = [
        pl.BlockSpec((tm, d), row),
        pl.BlockSpec((None, 1, d), lambda b, i: (layer, 0, 0)),
        _resident((None, d, d), lambda b, i: (layer, 0, 0)),
        pl.BlockSpec((None, None, nm, d), lambda b, i: (layer, b, 0, 0)),
        pl.BlockSpec((None, None, nm, d), lambda b, i: (layer, b, 0, 1)),
        _resident((None, d, d), lambda b, i: (layer, 0, 0)),
    ]
    args = [h, norm_w, wq, kv, kv, wo]
    if mix is not None:
        y_ssd, o_mla, w_out, e = mix
        ny, na = y_ssd.shape[1], o_mla.shape[1]
        assert ny % na == 0
        in_specs += [
            pl.BlockSpec((tm, ny), row),
            pl.BlockSpec((tm, na), row),
            _resident((None, ny, d), lambda b, i: (e, 0, 0)),
            _resident((None, na, d), lambda b, i: (e, ny // na, 0)),
        ]
        args += [y_ssd, o_mla, w_out, w_out]
    return pl.pallas_call(
        functools.partial(_xa_body, heads=XA_HEADS, mixed=mix is not None),
        grid=(batch, ns),
        in_specs=in_specs,
        out_specs=pl.BlockSpec((tm, d), row),
        out_shape=jax.ShapeDtypeStruct((t, d), F32),
        compiler_params=_params("parallel", "parallel"),
        name="xattn",
    )(*args)


def _inproj_body(h_ref, nw_ref, w_ref, qn_ref, kvn_ref, wqa_ref, wqb_ref, wkn_ref, wvt_ref,
                 cos_ref, sin_ref, z_ref, xbc_ref, dt_ref, q_ref, k_ref, vt_ref, *, cols, scale):
    c_z, c_xbc, c_dt, c_cq, c_ckv, c_kr = cols
    u = _rms(h_ref[...], nw_ref[...]).astype(BF)
    z_ref[...] = _dot(u, w_ref[:, c_z[0]:c_z[1]]).astype(BF)
    xbc_ref[...] = _dot(u, w_ref[:, c_xbc[0]:c_xbc[1]])
    dt_ref[...] = _dot(u, w_ref[:, c_dt[0]:c_dt[1]])
    cqn = _rms(_dot(u, w_ref[:, c_cq[0]:c_cq[1]]), qn_ref[...]).astype(BF)
    ckvn = _rms(_dot(u, w_ref[:, c_ckv[0]:c_ckv[1]]), kvn_ref[...]).astype(BF)
    kr = _dot(u, w_ref[:, c_kr[0]:c_kr[1]])
    cos_t = cos_ref[...]
    sin_t = sin_ref[...]
    kp = kr[:, :LANES] * cos_t + kr[:, LANES:] * sin_t
    qa = _dot(cqn, wqa_ref[...])
    qb = _dot(cqn, wqb_ref[...])
    kn = _dot(ckvn, wkn_ref[...])
    for i in range(qa.shape[-1] // LANES):
        sl = slice(i * LANES, (i + 1) * LANES)
        q_ref[:, sl] = ((qa[:, sl] * cos_t + qb[:, sl] * sin_t) * scale).astype(BF)
        k_ref[:, sl] = (kn[:, sl] + kp).astype(BF)
    vrow = lax.broadcasted_iota(jnp.int32, (vt_ref.shape[0], 1), 0) % (2 * LANES)
    ones = jnp.where((vrow == MLA_V) | (vrow == LANES), 1.0, 0.0)
    vt = lax.dot_general(wvt_ref[...], ckvn, NT_DIMS, preferred_element_type=F32)
    vt_ref[...] = (vt + ones).astype(BF)


def _inproj(h, norm_w, w_all, cols, q_norm, kv_norm, wqa, wqb, wkn, wvt, cos_t, sin_t, layer, e):
    t, d = h.shape
    tm = _tile(t, 1024)
    wc = w_all.shape[-1]
    n_z = cols[0][1] - cols[0][0]
    n_xbc = cols[1][1] - cols[1][0]
    hq = wqa.shape[-1]
    row = lambda i: (i, 0)
    return pl.pallas_call(
        functools.partial(_inproj_body, cols=cols, scale=math.log2(math.e) * (MLA_NOPE + MLA_ROPE) ** -0.5),
        grid=(t // tm,),
        in_specs=[
            pl.BlockSpec((tm, d), row),
            pl.BlockSpec((None, 1, d), lambda i: (layer, 0, 0)),
            _resident((None, d, wc), lambda i: (e, 0, 0)),
            pl.BlockSpec((None, 1, MLA_Q_RANK), lambda i: (e, 0, 0)),
            pl.BlockSpec((None, 1, MLA_KV_RANK), lambda i: (e, 0, 0)),
            _resident((None, MLA_Q_RANK, hq), lambda i: (e, 0, 0)),
            _resident((None, MLA_Q_RANK, hq), lambda i: (e, 0, 0)),
            _resident((None, MLA_KV_RANK, hq), lambda i: (e, 0, 0)),
            _resident((None, hq, MLA_KV_RANK), lambda i: (e, 0, 0)),
            pl.BlockSpec((tm, LANES), row),
            pl.BlockSpec((tm, LANES), row),
        ],
        out_specs=[
            pl.BlockSpec((tm, n_z), row),
            pl.BlockSpec((tm, n_xbc), row),
            pl.BlockSpec((tm, LANES), row),
            pl.BlockSpec((tm, hq), row),
            pl.BlockSpec((tm, hq), row),
            pl.BlockSpec((hq, tm), lambda i: (0, i)),
        ],
        out_shape=[
            jax.ShapeDtypeStruct((t, n_z), BF),
            jax.ShapeDtypeStruct((t, n_xbc), F32),
            jax.ShapeDtypeStruct((t, LANES), F32),
            jax.ShapeDtypeStruct((t, hq), BF),
            jax.ShapeDtypeStruct((t, hq), BF),
            jax.ShapeDtypeStruct((hq, t), BF),
        ],
        compiler_params=_params("parallel"),
        name="mix_inproj",
    )(h, norm_w, w_all, q_norm, kv_norm, wqa, wqb, wkn, wvt, cos_t, sin_t)


def _split3_dot(a_bf, x):
    hi = x.astype(BF)
    r1 = x - hi.astype(F32)
    mid = r1.astype(BF)
    low = (r1 - mid.astype(F32)).astype(BF)
    return _dot(a_bf, hi) + _dot(a_bf, mid) + _dot(a_bf, low)


def _ssd_decay_stage(z, r0, dt_ref, bias_ref, a_ref, sel_ref, rows_ref, dec_ref, wall_ref):
    q = SSD_CHUNK
    nh = SSD_HEADS
    dt = jax.nn.softplus(dt_ref[pl.ds(r0, q), :] + bias_ref[...])
    la = dt * (a_ref[...] * math.log2(math.e))
    row = lax.broadcasted_iota(jnp.int32, (q, q), 0)
    col = lax.broadcasted_iota(jnp.int32, (q, q), 1)
    tril = jnp.where(row >= col, 1.0, 0.0).astype(BF)
    cum = _split3_dot(tril, la)
    tot = cum[q - 1:q, :]
    rev = tot - cum + la
    fwd_lane = lax.broadcasted_iota(jnp.int32, (q, LANES), 1) < nh
    sel = jnp.where(fwd_lane, cum, rev)
    sel_ref[z] = sel
    dt_t = dt.T
    rows_ref[z, 0:2 * nh, :] = (sel - jnp.log2(dt)).T[0:2 * nh]
    rows_ref[z, 2 * nh:3 * nh, :] = jnp.log2(dt_t[0:nh] + dt_t[nh:2 * nh])
    wall_ref[z] = (jnp.exp2(tot - sel) * dt).astype(BF)
    etot = jnp.exp2(tot)
    lo1 = lax.broadcasted_iota(jnp.int32, (1, LANES), 1) < SSD_HEAD_DIM
    for d in range(2):
        for p in range(nh // 2):
            h0 = d * nh + 2 * p
            dec = jnp.where(lo1, etot[:, h0:h0 + 1], etot[:, h0 + 1:h0 + 2])
            dec_ref[z, d, p] = jnp.broadcast_to(dec, dec_ref.shape[3:])


def _ssd_conv_stage(z, j, first, last, xc_ref, xp_ref, xn_ref, cw_ref, cb_ref, sm_ref, x_ref, b_ref, c_ref):
    q = SSD_CHUNK
    inner = SSD_HEADS * SSD_HEAD_DIM
    gn = SSD_GROUPS * SSD_STATE
    half = SSD_CONV // 2
    edge = xp_ref.shape[0]
    nsub = xc_ref.shape[0] // q
    r0 = pl.multiple_of(j * q, q)
    cur = xc_ref[pl.ds(r0, q), :]
    before = xc_ref[pl.ds(pl.multiple_of(jnp.maximum(r0 - edge, 0), edge), edge), :]
    after = xc_ref[pl.ds(pl.multiple_of(jnp.minimum(r0 + q, (nsub - 1) * q + q - edge), edge), edge), :]
    prev = jnp.where(first, 0.0, jnp.where(j == 0, xp_ref[...], before))
    nxt = jnp.where(last, 0.0, jnp.where(j == nsub - 1, xn_ref[...], after))
    fill = jnp.zeros((sm_ref.shape[1] - q - 2 * edge, cur.shape[1]), F32)
    window = jnp.concatenate([prev, cur, nxt, fill], axis=0).astype(BF)
    shifted = _dot(sm_ref[...], window)
    acc = cur * cw_ref[half:half + 1, :] + cb_ref[...]
    blk = 0
    for k in range(SSD_CONV):
        if k != half:
            acc = acc + shifted[blk * q:(blk + 1) * q] * cw_ref[k:k + 1, :]
            blk += 1
    xbc = jax.nn.silu(acc)
    x_ref[z] = xbc[:, :inner].astype(BF)
    for g in range(SSD_GROUPS):
        b_ref[z, g] = xbc[:, inner + g * SSD_STATE:inner + (g + 1) * SSD_STATE].T.astype(BF)
    c_ref[z] = xbc[:, inner + gn:].astype(BF)


def _ssd_state_stage(z, e_ref, st_ref, x_ref, b_ref, wall_ref, dec_ref, carry_ref):
    nh, n = SSD_HEADS, SSD_STATE
    inner = nh * SSD_HEAD_DIM
    gw = inner // SSD_GROUPS
    ppg = gw // LANES
    x = x_ref[z].astype(F32)
    wexp = _dot(wall_ref[z], e_ref[...])
    for d in range(2):
        xw = (x * wexp[:, d * inner:(d + 1) * inner]).astype(BF)
        for g in range(SSD_GROUPS):
            upd = _dot(b_ref[z, g], xw[:, g * gw:(g + 1) * gw])
            for i in range(ppg):
                p = g * ppg + i
                contrib = upd[:, i * LANES:(i + 1) * LANES]
                if d == 0:
                    state = carry_ref[0, p]
                    st_ref[z, 0, p] = state
                    carry_ref[0, p] = state * dec_ref[z, 0, p][0:1] + contrib
                else:
                    st_ref[z, 1, p] = contrib


def _ssd_output_phase(z, j, z_ref, d_ref, nw_ref, o_ref, st_ref, x_ref, b_ref, c_ref, sel_ref, rows_ref,
                      dec_ref, carry_ref):
    q = SSD_CHUNK
    r0 = pl.multiple_of(j * q, q)
    nh, n, hpg = SSD_HEADS, SSD_STATE, SSD_HEADS // SSD_GROUPS
    sel = sel_ref[z]
    rows = rows_ref[z]
    row = lax.broadcasted_iota(jnp.int32, (q, q), 0)
    col = lax.broadcasted_iota(jnp.int32, (q, q), 1)
    lower = row > col
    diag = row == col
    lo = lax.broadcasted_iota(jnp.int32, (q, LANES), 1) < SSD_HEAD_DIM
    ys = []
    for g in range(SSD_GROUPS):
        cg = c_ref[z, :, g * n:(g + 1) * n]
        cb = _dot(cg, b_ref[z, g])
        cg32 = cg.astype(F32)
        for j in range(hpg // 2):
            p = g * (hpg // 2) + j
            lhs = []
            for h in (2 * p, 2 * p + 1):
                hb = nh + h
                a_f = jnp.broadcast_to(sel[:, h:h + 1], (q, q))
                a_b = jnp.broadcast_to(sel[:, hb:hb + 1], (q, q))
                seg = jnp.where(lower, a_f - rows[h:h + 1, :],
                                jnp.where(diag, rows[2 * nh + h:2 * nh + h + 1, :], a_b - rows[hb:hb + 1, :]))
                m = (cb * jnp.exp2(seg)).astype(BF)
                cef = (cg32 * jnp.exp2(a_f)).astype(BF)
                ceb = (cg32 * jnp.exp2(a_b)).astype(BF)
                lhs.append(jnp.concatenate([m, cef, ceb], axis=1))
            xp = x_ref[z, :, p * LANES:(p + 1) * LANES]
            back = carry_ref[1, p]
            carry_ref[1, p] = back * dec_ref[z, 1, p][0:1] + st_ref[z, 1, p]
            rhs = jnp.concatenate([xp, st_ref[z, 0, p].astype(BF), back.astype(BF)], axis=0)
            out = _dot(jnp.concatenate(lhs, axis=0), rhs)
            ys.append(jnp.where(lo, out[:q], out[q:]) + xp.astype(F32) * d_ref[:, p * LANES:(p + 1) * LANES])
    y = jnp.concatenate(ys, axis=-1)
    gated = y * jax.nn.silu(z_ref[pl.ds(r0, q), :].astype(F32))
    o_ref[pl.ds(r0, q), :] = _rms(gated, nw_ref[...]).astype(BF)


def _ssd_body(xc_ref, xp_ref, xn_ref, dt_ref, z_ref, cw_ref, cb_ref, bias_ref, a_ref, d_ref, nw_ref, e_ref, sm_ref,
              o_ref, st_ref, x_ref, b_ref, c_ref, sel_ref, rows_ref, dec_ref, wall_ref, carry_ref):
    t = pl.program_id(1)
    nc = st_ref.shape[0]
    nsub = xc_ref.shape[0] // SSD_CHUNK
    nb = nc // nsub

    @pl.when(t == 0)
    def _():
        carry_ref[0] = jnp.zeros(carry_ref.shape[1:], F32)

    @pl.when(t == nb)
    def _():
        carry_ref[1] = jnp.zeros(carry_ref.shape[1:], F32)

    @pl.when(t < nb)
    def _():
        base = t * nsub
        for j in range(nsub):
            _ssd_decay_stage(base + j, j * SSD_CHUNK, dt_ref, bias_ref, a_ref, sel_ref, rows_ref, dec_ref, wall_ref)

        def conv(j):
            z = base + j
            _ssd_conv_stage(z, j, z == 0, z == nc - 1, xc_ref, xp_ref, xn_ref, cw_ref, cb_ref, sm_ref,
                            x_ref, b_ref, c_ref)

        def state(j):
            _ssd_state_stage(base + j, e_ref, st_ref, x_ref, b_ref, wall_ref, dec_ref, carry_ref)

        def sub(j, carry):
            conv(j)
            state(j - 1)
            return carry

        conv(0)
        lax.fori_loop(1, nsub, sub, 0)
        state(nsub - 1)

    @pl.when(t >= nb)
    def _():
        def sub(i, carry):
            j = nsub - 1 - i
            z = (2 * nb - 1 - t) * nsub + j
            _ssd_output_phase(z, j, z_ref, d_ref, nw_ref, o_ref, st_ref, x_ref, b_ref, c_ref, sel_ref,
                              rows_ref, dec_ref, carry_ref)
            return carry
        lax.fori_loop(0, nsub, sub, 0)


def _conv_shift_table(q, edge, rows):
    half = SSD_CONV // 2
    m = np.zeros(((SSD_CONV - 1) * q, rows), np.float32)
    blk = 0
    for k in range(SSD_CONV):
        if k != half:
            m[blk * q + np.arange(q), edge + np.arange(q) + k - half] = 1.0
            blk += 1
    return jnp.asarray(m, BF)


def _head_expand_table():
    inner = SSD_HEADS * SSD_HEAD_DIM
    e = np.zeros((LANES, 2 * inner), np.float32)
    for h in range(2 * SSD_HEADS):
        e[h, h * SSD_HEAD_DIM:(h + 1) * SSD_HEAD_DIM] = 1.0
    return jnp.asarray(e, BF)


def _ssd(xbc, dt_raw, z, conv_w, conv_b, dt_bias, a_neg, d_exp, ssd_norm, e, batch):
    t, c = xbc.shape
    s = t // batch
    inner = SSD_HEADS * SSD_HEAD_DIM
    gn = SSD_GROUPS * SSD_STATE
    q = SSD_CHUNK
    nc = s // q
    nsub = next(n for n in (8, 4, 2, 1) if nc % n == 0)
    nb = nc // nsub
    rows = nsub * q
    edge = 8
    epb = rows // edge
    blk1 = lambda i: jnp.minimum(i, nb - 1)
    blk3 = lambda i: nb - 1 - jnp.maximum(i - nb, 0)
    par = lambda b, i: (e, 0, 0)
    window_rows = 2 * q
    return pl.pallas_call(
        _ssd_body,
        grid=(batch, 2 * nb),
        in_specs=[
            pl.BlockSpec((rows, c), lambda b, i: (b * nb + blk1(i), 0)),
            pl.BlockSpec((edge, c), lambda b, i: (b * nb * epb + jnp.maximum(blk1(i) * epb - 1, 0), 0)),
            pl.BlockSpec((edge, c), lambda b, i: (b * nb * epb + jnp.minimum((blk1(i) + 1) * epb, nb * epb - 1), 0)),
            pl.BlockSpec((rows, LANES), lambda b, i: (b * nb + blk1(i), 0)),
            pl.BlockSpec((rows, inner), lambda b, i: (b * nb + blk3(i), 0)),
            pl.BlockSpec((None, SSD_CONV, c), par),
            pl.BlockSpec((None, 1, c), par),
            pl.BlockSpec((None, 1, LANES), par),
            pl.BlockSpec((None, 1, LANES), par),
            pl.BlockSpec((None, 1, inner), par),
            pl.BlockSpec((None, 1, inner), par),
            _resident((LANES, 2 * inner), lambda b, i: (0, 0)),
            _resident(((SSD_CONV - 1) * q, window_rows), lambda b, i: (0, 0)),
        ],
        out_specs=pl.BlockSpec((rows, inner), lambda b, i: (b * nb + blk3(i), 0)),
        out_shape=jax.ShapeDtypeStruct((t, inner), BF),
        scratch_shapes=[
            pltpu.VMEM((nc, 2, SSD_HEADS // 2, SSD_STATE, LANES), F32),
            pltpu.VMEM((nc, q, inner), BF),
            pltpu.VMEM((nc, SSD_GROUPS, SSD_STATE, q), BF),
            pltpu.VMEM((nc, q, gn), BF),
            pltpu.VMEM((nc, q, LANES), F32),
            pltpu.VMEM((nc, 3 * SSD_HEADS, q), F32),
            pltpu.VMEM((nc, 2, SSD_HEADS // 2, edge, LANES), F32),
            pltpu.VMEM((nc, q, LANES), BF),
            pltpu.VMEM((2, SSD_HEADS // 2, SSD_STATE, LANES), F32),
        ],
        compiler_params=_params("parallel", "arbitrary"),
        name="ssd",
    )(xbc, xbc, xbc, dt_raw, z, conv_w, conv_b, dt_bias, a_neg, d_exp, ssd_norm, _head_expand_table(),
      _conv_shift_table(q, edge, window_rows))


def _lane_fold(x, op):
    out = x[:, :LANES]
    for j in range(1, x.shape[-1] // LANES):
        out = op(out, x[:, j * LANES:(j + 1) * LANES])
    return out


def _mla_body(q_ref, k_ref, vt_ref, o_ref, s_ref, m_ref, *, kt):
    nk = k_ref.shape[0]
    rb = s_ref.shape[2]
    nrb = q_ref.shape[0] // rb
    nheads = q_ref.shape[1] // LANES

    def score_pass(r, head):
        rows = pl.ds(pl.multiple_of(r * rb, rb), rb)
        sl = slice(head * LANES, (head + 1) * LANES)
        slot = head % 2
        s = lax.dot_general(k_ref[:, sl], q_ref[rows, sl], NT_DIMS, preferred_element_type=F32)
        s_ref[slot] = s
        m_ref[slot] = jnp.max(s, axis=0, keepdims=True)

    def value_pass(head):
        hs = slice(head * LANES, (head + 1) * LANES)
        slot = head % 2
        m = m_ref[slot]
        o = jnp.zeros((LANES, rb), F32)
        for c in range(0, nk, kt):
            p = jnp.exp2(s_ref[slot, c:c + kt, :] - m)
            o = o + _dot(vt_ref[hs, c:c + kt], p.astype(BF))
        return o

    def row_block(r, carry):
        top = lax.broadcasted_iota(jnp.int32, (LANES, rb), 0) < MLA_V
        outs = []
        for head in range(nheads):
            if head + 1 < nheads:
                score_pass(r, head + 1)
            else:
                score_pass(jnp.minimum(r + 1, nrb - 1), 0)
            outs.append(value_pass(head))
        pairs = [jnp.where(top, o0 / o0[MLA_V:MLA_V + 1, :], o1 / o1[0:1, :]).T
                 for o0, o1 in zip(outs[0::2], outs[1::2])]
        o_ref[pl.ds(pl.multiple_of(r * rb, rb), rb), :] = jnp.concatenate(pairs, axis=-1).astype(BF)
        return carry

    score_pass(0, 0)
    lax.fori_loop(0, nrb, row_block, 0)


def _mla(q, k, vt, batch):
    t, hq = q.shape
    s = t // batch
    rb = _tile(s, 256)
    nv = MLA_HEADS * MLA_V
    q3, k3 = (a.reshape(batch, s, hq) for a in (q, k))
    blk = pl.BlockSpec((None, s, hq), lambda b: (b, 0, 0))
    o = pl.pallas_call(
        functools.partial(_mla_body, kt=_tile(s, 256)),
        grid=(batch,),
        in_specs=[blk, blk, pl.BlockSpec((hq, s), lambda b: (0, b))],
        out_specs=pl.BlockSpec((None, s, nv), lambda b: (b, 0, 0)),
        out_shape=jax.ShapeDtypeStruct((batch, s, nv), BF),
        scratch_shapes=[pltpu.VMEM((2, s, rb), F32), pltpu.VMEM((2, 1, rb), F32)],
        compiler_params=_params("parallel"),
        name="mla_attn",
    )(q3, k3, vt)
    return o.reshape(t, nv)


def _fnet_fold_body(ha_ref, hm_ref, hx_ref, nw_ref, cc_ref, sc_ref, ec_ref, es_ref):
    tm = ha_ref.shape[0]
    nw = nw_ref[...]
    u_a = _rms(ha_ref[...], nw)
    u_m = _rms(hm_ref[...], nw)
    u_x = _rms(hx_ref[...], nw)[0:1]
    r = lax.broadcasted_iota(jnp.int32, (tm, tm), 0)
    c = lax.broadcasted_iota(jnp.int32, (tm, tm), 1)
    perm = jnp.where(r + c == tm, 1.0, 0.0).astype(BF)
    hi = u_m.astype(BF)
    low = (u_m - hi.astype(F32)).astype(BF)
    mirror = _dot(perm, hi) + _dot(perm, low)
    first = lax.broadcasted_iota(jnp.int32, u_a.shape, 0) == 0
    mirror = jnp.where(first, u_x, mirror)
    ue = (u_a + mirror).astype(BF)
    uo = (u_a - mirror).astype(BF)
    gc = cc_ref.shape[0]
    for g in range(ue.shape[-1] // gc):
        sl = slice(g * gc, (g + 1) * gc)
        ec_ref[:, sl] = _dot(ue[:, sl], cc_ref[...]).astype(BF)
        es_ref[:, sl] = _dot(uo[:, sl], sc_ref[...]).astype(BF)


def _fnet_seq_body(h_ref, hh_ref, nw_ref, cc_ref, cs_ref, ss_ref, ec_ref, es_ref, w_ref, o_ref, *, scale):
    ts = h_ref.shape[0]
    rows = pl.ds(pl.multiple_of(pl.program_id(1) * ts, ts), ts)
    y = _dot(cs_ref[rows, :], ec_ref[...]) - _dot(ss_ref[rows, :], es_ref[...])
    u_h = _rms(hh_ref[...], nw_ref[...]).astype(BF)
    gc = cc_ref.shape[0]
    x_h = jnp.concatenate([_dot(u_h[:, g * gc:(g + 1) * gc], cc_ref[...])
                           for g in range(u_h.shape[-1] // gc)], axis=-1)[0:1]
    odd = lax.broadcasted_iota(jnp.int32, (ts, 1), 0) % 2 == 1
    y = y + jnp.where(odd, -1.0, 1.0) * x_h
    o_ref[...] = h_ref[...] + _dot((y * scale).astype(BF), w_ref[...])


def _dft_tables(n, fold):
    j = np.arange(n)[:, None]
    k = np.arange(n // 2 if fold else n)[None, :]
    ang = ((j * k) % n) * (2.0 * np.pi / n)
    cos, sin = np.cos(ang), np.sin(ang)
    if fold:
        cos[:, 0] = 0.5
    return jnp.asarray(cos, BF), jnp.asarray(sin, BF)


def _fnet(h, norm_w, w_out, layer, o, batch, tables):
    t, d = h.shape
    s = t // batch
    cc, sc, cs, ss = tables
    gc = cc.shape[0]
    sub = 8
    tm = _tile(s // 2, 256)
    nt, nf = s // tm, s // 2 // tm
    nwspec = pl.BlockSpec((None, 1, d), lambda b, i: (layer, 0, 0))
    table = _resident((gc, gc), lambda b, i: (0, 0))
    ec, es = pl.pallas_call(
        _fnet_fold_body,
        grid=(batch, nf),
        in_specs=[
            pl.BlockSpec((tm, d), lambda b, i: (b * nt + i, 0)),
            pl.BlockSpec((tm, d), lambda b, i: (b * nt + nt - 1 - i, 0)),
            pl.BlockSpec((sub, d), lambda b, i: (b * (s // sub) + ((nt - i) % nt) * (tm // sub), 0)),
            nwspec, table, table,
        ],
        out_specs=[pl.BlockSpec((tm, d), lambda b, i: (b * nf + i, 0))] * 2,
        out_shape=[jax.ShapeDtypeStruct((t // 2, d), BF)] * 2,
        compiler_params=_params("parallel", "parallel"),
        name="fnet_fold_channel_dft",
    )(h, h, h, norm_w, cc, sc)
    ts = _tile(s, 512)
    ns = s // ts
    return pl.pallas_call(
        functools.partial(_fnet_seq_body, scale=(s * gc) ** -0.5),
        grid=(batch, ns),
        in_specs=[
            pl.BlockSpec((ts, d), lambda b, i: (b * ns + i, 0)),
            pl.BlockSpec((sub, d), lambda b, i: (b * (s // sub) + s // 2 // sub, 0)),
            nwspec, table,
            _resident((s, s // 2), lambda b, i: (0, 0)),
            _resident((s, s // 2), lambda b, i: (0, 0)),
            pl.BlockSpec((s // 2, d), lambda b, i: (b, 0)),
            pl.BlockSpec((s // 2, d), lambda b, i: (b, 0)),
            _resident((None, d, d), lambda b, i: (o, 0, 0)),
        ],
        out_specs=pl.BlockSpec((ts, d), lambda b, i: (b * ns + i, 0)),
        out_shape=jax.ShapeDtypeStruct((t, d), F32),
        compiler_params=_params("parallel", "arbitrary"),
        name="fnet_seq_dft",
    )(h, h, norm_w, cc, cs, ss, ec, es, w_out)


def _mixer_weights(w_in, w_uq, w_ukv, dt_bias, a_log, ssd_d):
    w_in, w_uq, w_ukv = w_in.astype(BF), w_uq.astype(BF), w_ukv.astype(BF)
    ne, d, _ = w_in.shape
    inner = SSD_HEADS * SSD_HEAD_DIM
    conv_ch = inner + 2 * SSD_GROUPS * SSD_STATE
    o_z, o_xbc = 0, inner
    o_dt = o_xbc + conv_ch
    o_cq = o_dt + 2 * SSD_HEADS
    o_ckv = o_cq + MLA_Q_RANK
    o_kr = o_ckv + MLA_KV_RANK
    half = MLA_ROPE // 2
    pad = LANES - MLA_NOPE - MLA_ROPE
    zeros = lambda *s: jnp.zeros(s, w_in.dtype)
    w_dt = jnp.concatenate([w_in[:, :, o_dt:o_cq], zeros(ne, d, LANES - 2 * SSD_HEADS)], axis=-1)
    kr1 = w_in[:, :, o_kr:o_kr + half]
    kr2 = w_in[:, :, o_kr + half:o_kr + MLA_ROPE]
    kr_a = jnp.concatenate([zeros(ne, d, MLA_NOPE), kr1, kr2, zeros(ne, d, pad)], axis=-1)
    kr_b = jnp.concatenate([zeros(ne, d, MLA_NOPE), -kr2, kr1, zeros(ne, d, pad)], axis=-1)
    pieces = [w_in[:, :, o_z:o_xbc], w_in[:, :, o_xbc:o_dt], w_dt, w_in[:, :, o_cq:o_ckv],
              w_in[:, :, o_ckv:o_kr], jnp.concatenate([kr_a, kr_b], axis=-1)]
    cols, c = [], 0
    for p in pieces:
        cols.append((c, c + p.shape[-1]))
        c += p.shape[-1]
    w_all = jnp.concatenate(pieces, axis=-1).astype(BF)

    uq = w_uq.reshape(ne, MLA_Q_RANK, MLA_HEADS, MLA_NOPE + MLA_ROPE)
    q_nope, q1, q2 = uq[..., :MLA_NOPE], uq[..., MLA_NOPE:MLA_NOPE + half], uq[..., MLA_NOPE + half:]
    zq = lambda n: jnp.zeros((ne, MLA_Q_RANK, MLA_HEADS, n), w_uq.dtype)
    hq = MLA_HEADS * LANES
    wqa = jnp.concatenate([q_nope, q1, q2, zq(pad)], axis=-1).reshape(ne, MLA_Q_RANK, hq).astype(BF)
    wqb = jnp.concatenate([zq(MLA_NOPE), -q2, q1, zq(pad)], axis=-1).reshape(ne, MLA_Q_RANK, hq).astype(BF)

    ukv = w_ukv.reshape(ne, MLA_KV_RANK, MLA_HEADS, MLA_NOPE + MLA_V)
    zkv = lambda *s: jnp.zeros((ne, MLA_KV_RANK) + s, w_ukv.dtype)
    wkn = jnp.concatenate([ukv[..., :MLA_NOPE], zkv(MLA_HEADS, LANES - MLA_NOPE)], axis=-1)
    wkn = wkn.reshape(ne, MLA_KV_RANK, hq).astype(BF)
    vv = ukv[..., MLA_NOPE:].reshape(ne, MLA_KV_RANK, MLA_HEADS // 2, 2, MLA_V)
    zv = zkv(MLA_HEADS // 2, MLA_V)
    wv = jnp.stack([jnp.concatenate([vv[:, :, :, 0], zv], axis=-1),
                    jnp.concatenate([zv, vv[:, :, :, 1]], axis=-1)], axis=3)
    wvt = jnp.swapaxes(wv.reshape(ne, MLA_KV_RANK, hq), 1, 2).astype(BF)

    padl = lambda a: jnp.concatenate([a, jnp.zeros((ne, LANES - a.shape[-1]), a.dtype)], axis=-1)[:, None, :]
    bias = padl(dt_bias.reshape(ne, 2 * SSD_HEADS))
    a_neg = padl(-jnp.exp(a_log.reshape(ne, 2 * SSD_HEADS)))
    d_exp = jnp.repeat(ssd_d, SSD_HEAD_DIM, axis=-1)[:, None, :]
    return w_all, tuple(cols), wqa, wqb, wkn, wvt, bias, a_neg, d_exp


def _rope_tables(positions):
    inv = 1.0 / (ROPE_THETA ** (jnp.arange(0, MLA_ROPE, 2, dtype=F32) / MLA_ROPE))
    ang = inv[:, None] * positions.astype(F32).reshape(1, -1)
    cos, sin = jnp.cos(ang), jnp.sin(ang)
    t = cos.shape[1]
    pad = LANES - MLA_NOPE - MLA_ROPE
    cos_t = jnp.concatenate([jnp.ones((MLA_NOPE, t), F32), cos, cos, jnp.zeros((pad, t), F32)], axis=0).T
    sin_t = jnp.concatenate([jnp.zeros((MLA_NOPE, t), F32), sin, sin, jnp.zeros((pad, t), F32)], axis=0).T
    return cos_t, sin_t


def kernel(x, mem, positions, mem_norm, final_norm, ffn1_norm, ffn1_w_gu, ffn1_w_down, mix_norm, xa_norm,
           xa_wq, xa_wkv, xa_wo, ffn2_norm, ffn2_w_gu, ffn2_w_down, w_in, conv_w, conv_b, dt_bias, a_log,
           ssd_d, ssd_norm, q_norm, w_uq, kv_norm, w_ukv, w_out, fnet_w_out):
    batch, seq, d = x.shape
    depth = ffn1_norm.shape[0]
    t = batch * seq
    bf = lambda a: a.astype(BF)
    row3 = lambda a: a[:, None, :]

    kv = _kvproj(mem.reshape(-1, d), mem_norm[None, :], bf(xa_wkv)).reshape(depth, batch, mem.shape[1], 2 * d)
    w_all, cols, wqa, wqb, wkn, wvt, bias, a_neg, d_exp = _mixer_weights(w_in, w_uq, w_ukv, dt_bias, a_log, ssd_d)
    cos_t, sin_t = _rope_tables(positions)
    gc = d // FNET_GROUPS
    tables = _dft_tables(gc, fold=False) + _dft_tables(seq, fold=True)
    f1n, f2n, mxn, xan = row3(ffn1_norm), row3(ffn2_norm), row3(mix_norm), row3(xa_norm)
    gu, dn = bf(ffn1_w_gu[0]), bf(ffn1_w_down[0])
    wq, wo, w_mix_out, w_fnet = bf(xa_wq), bf(xa_wo), bf(w_out), bf(fnet_w_out)
    ssd_nw, qn, kvn = row3(ssd_norm), row3(q_norm), row3(kv_norm)

    h = x.reshape(t, d)
    for layer in range(depth):
        h, gu, dn = _ffn(h, f1n, gu, dn, layer, cast_next=(ffn2_w_gu, ffn2_w_down, layer))
        if layer % 2 == 0:
            e = layer // 2
            z, xbc, dt_raw, q, k, vt = _inproj(h, mxn, w_all, cols, qn, kvn, wqa, wqb, wkn, wvt,
                                              cos_t, sin_t, layer, e)
            y_ssd = _ssd(xbc, dt_raw, z, conv_w, row3(conv_b), bias, a_neg, d_exp, ssd_nw, e, batch)
            o_mla = _mla(q, k, vt, batch)
            mix = (y_ssd, o_mla, w_mix_out, e)
        else:
            h = _fnet(h, mxn, w_fnet, layer, layer // 2, batch, tables)
            mix = None
        h = _xattn(h, xan, wq, kv, wo, layer, batch, mix=mix)
        if layer + 1 < depth:
            h, gu, dn = _ffn(h, f2n, gu, dn, layer, cast_next=(ffn1_w_gu, ffn1_w_down, layer + 1))
        else:
            h = _ffn(h, f2n, gu, dn, layer, final_w=final_norm[None, :])
    return h.reshape(batch, seq, d)
```

```python
import functools
import math

import numpy as np
import jax
import jax.numpy as jnp
from jax import lax
from jax.experimental import pallas as pl
from jax.experimental.pallas import tpu as pltpu

EPS = 1e-6
BF = jnp.bfloat16
F32 = jnp.float32

V7X_VMEM_BYTES = 64 * 1024 * 1024
VMEM_LIMIT = V7X_VMEM_BYTES - 8 * 1024 * 1024
LANES = 128

SSD_HEADS = 16
SSD_HEAD_DIM = 64
SSD_GROUPS = 2
SSD_STATE = 128
SSD_CONV = 5
SSD_CHUNK = 128
MLA_HEADS = 8
MLA_Q_RANK = 512
MLA_KV_RANK = 256
MLA_NOPE = 64
MLA_ROPE = 32
MLA_V = 64
ROPE_THETA = 10000.0
FNET_GROUPS = 4
XA_HEADS = 4

NT_DIMS = (((1,), (1,)), ((), ()))
TN_DIMS = (((0,), (0,)), ((), ()))


def _params(*sem):
    return pltpu.CompilerParams(dimension_semantics=sem, vmem_limit_bytes=VMEM_LIMIT)


def _resident(shape, index_map):
    return pl.BlockSpec(shape, index_map, pipeline_mode=pl.Buffered(1))


def _rms(x, w):
    return x * lax.rsqrt(jnp.mean(x * x, axis=-1, keepdims=True) + EPS) * w


def _dot(a, b):
    return jnp.dot(a, b, preferred_element_type=F32)


def _tile(n, pref):
    t = min(n, pref)
    assert n % t == 0, (n, t)
    return t


def _ffn_body(*refs, chunks, final, cast):
    refs = list(refs)
    h_ref, nw_ref, wg_ref, wu_ref, wd_ref = refs[:5]
    del refs[:5]
    fw_ref = refs.pop(0) if final else None
    if cast:
        gu_src, dn_src = refs.pop(0), refs.pop(0)
        o_ref, gu_dst, dn_dst = refs
        gu_dst[...] = gu_src[...].astype(BF)
        dn_dst[...] = dn_src[...].astype(BF)
    else:
        (o_ref,) = refs
    h = h_ref[...]
    xn = _rms(h, nw_ref[...]).astype(BF)
    acc = jnp.zeros(h.shape, F32)
    for a, b in chunks:
        g = _dot(xn, wg_ref[:, a:b])
        u = _dot(xn, wu_ref[:, a:b])
        act = (jax.nn.silu(g) * u).astype(BF)
        acc = acc + _dot(act, wd_ref[a:b, :])
    out = h + 0.5 * acc
    if final:
        out = _rms(out, fw_ref[...])
    o_ref[...] = out


def _ffn(h, norm_w, w_gu, w_down, layer, final_w=None, cast_next=None):
    t, d = h.shape
    f = w_down.shape[0]
    tm = _tile(t, 1024)
    steps = t // tm
    step = 768
    chunks = tuple((a, min(a + step, f)) for a in range(0, f, step))
    row = lambda i: (i, 0)
    in_specs = [
        pl.BlockSpec((tm, d), row),
        pl.BlockSpec((None, 1, d), lambda i: (layer, 0, 0)),
        _resident((d, f), lambda i: (0, 0)),
        _resident((d, f), lambda i: (0, 1)),
        _resident((f, d), lambda i: (0, 0)),
    ]
    args = [h, norm_w, w_gu, w_gu, w_down]
    out_specs = [pl.BlockSpec((tm, d), row)]
    out_shape = [jax.ShapeDtypeStruct((t, d), F32)]
    if final_w is not None:
        in_specs.append(pl.BlockSpec((1, d), lambda i: (0, 0)))
        args.append(final_w)
    if cast_next is not None:
        src_gu, src_dn, nxt = cast_next
        nblk = next(n for n in (16, 8, 4, 2, 1) if steps % n == 0)
        rep = steps // nblk
        assert d % (16 * nblk) == 0 and f % (16 * nblk) == 0
        in_specs += [pl.BlockSpec((None, d // nblk, 2 * f), lambda i: (nxt, i // rep, 0)),
                     pl.BlockSpec((None, f // nblk, d), lambda i: (nxt, i // rep, 0))]
        args += [src_gu, src_dn]
        out_specs += [pl.BlockSpec((d // nblk, 2 * f), lambda i: (i // rep, 0)),
                      pl.BlockSpec((f // nblk, d), lambda i: (i // rep, 0))]
        out_shape += [jax.ShapeDtypeStruct((d, 2 * f), BF), jax.ShapeDtypeStruct((f, d), BF)]
    outs = pl.pallas_call(
        functools.partial(_ffn_body, chunks=chunks, final=final_w is not None, cast=cast_next is not None),
        grid=(steps,),
        in_specs=in_specs,
        out_specs=out_specs,
        out_shape=out_shape,
        compiler_params=_params("arbitrary" if cast_next is not None else "parallel"),
        name="ffn",
    )(*args)
    return outs if cast_next is not None else outs[0]


def _kvproj_body(m_ref, nw_ref, w_ref, o_ref):
    mn = _rms(m_ref[...], nw_ref[...]).astype(BF)
    o_ref[...] = _dot(mn, w_ref[...]).astype(BF)


def _kvproj(mem2d, mem_norm, wkv):
    n, d = mem2d.shape
    nl, _, d2 = wkv.shape
    tm = _tile(n, 512)
    return pl.pallas_call(
        _kvproj_body,
        grid=(nl, n // tm),
        in_specs=[
            pl.BlockSpec((tm, d), lambda l, i: (i, 0)),
            pl.BlockSpec((1, d), lambda l, i: (0, 0)),
            pl.BlockSpec((None, d, d2), lambda l, i: (l, 0, 0)),
        ],
        out_specs=pl.BlockSpec((None, tm, d2), lambda l, i: (l, i, 0)),
        out_shape=jax.ShapeDtypeStruct((nl, n, d2), BF),
        compiler_params=_params("parallel", "parallel"),
        name="xa_kvproj",
    )(mem2d, mem_norm, wkv)


def _xa_body(*refs, heads, mixed):
    if mixed:
        h_ref, nw_ref, wq_ref, k_ref, v_ref, wo_ref, y_ref, a_ref, wy_ref, wa_ref, o_ref = refs
        h = h_ref[...] + _dot(y_ref[...], wy_ref[...]) + _dot(a_ref[...], wa_ref[...])
    else:
        h_ref, nw_ref, wq_ref, k_ref, v_ref, wo_ref, o_ref = refs
        h = h_ref[...]
    hn = _rms(h, nw_ref[...]).astype(BF)
    dh = h.shape[-1] // heads
    q = (_dot(hn, wq_ref[...]) * (math.log2(math.e) * dh ** -0.5)).astype(BF)
    outs = []
    for i in range(heads):
        sl = slice(i * dh, (i + 1) * dh)
        s = lax.dot_general(q[:, sl], k_ref[:, sl], NT_DIMS, preferred_element_type=F32)
        p = jnp.exp2(s - jnp.max(s, axis=-1, keepdims=True))
        l = jnp.sum(p, axis=-1, keepdims=True)
        outs.append((_dot(p.astype(BF), v_ref[:, sl]) / l).astype(BF))
    o = jnp.concatenate(outs, axis=-1)
    o_ref[...] = h + _dot(o, wo_ref[...])


def _xattn(h, norm_w, wq, kv, wo, layer, batch, mix=None):
    t, d = h.shape
    s = t // batch
    nm = kv.shape[2]
    tm = _tile(s, 1024)
    ns = s // tm
    row = lambda b, i: (b * ns + i, 0)
    in_specs = [
        pl.BlockSpec((tm, d), row),
        pl.BlockSpec((None, 1, d), lambda b, i: (layer, 0, 0)),
        _resident((None, d, d), lambda b, i: (layer, 0, 0)),
        pl.BlockSpec((None, None, nm, d), lambda b, i: (layer, b, 0, 0)),
        pl.BlockSpec((None, None, nm, d), lambda b, i: (layer, b, 0, 1)),
        _resident((None, d, d), lambda b, i: (layer, 0, 0)),
    ]
    args = [h, norm_w, wq, kv, kv, wo]
    if mix is not None:
        y_ssd, o_mla, w_out, e = mix
        ny, na = y_ssd.shape[1], o_mla.shape[1]
        assert ny % na == 0
        in_specs += [
            pl.BlockSpec((tm, ny), row),
            pl.BlockSpec((tm, na), row),
            _resident((None, ny, d), lambda b, i: (e, 0, 0)),
            _resident((None, na, d), lambda b, i: (e, ny // na, 0)),
        ]
        args += [y_ssd, o_mla, w_out, w_out]
    return pl.pallas_call(
        functools.partial(_xa_body, heads=XA_HEADS, mixed=mix is not None),
        grid=(batch, ns),
        in_specs=in_specs,
        out_specs=pl.BlockSpec((tm, d), row),
        out_shape=jax.ShapeDtypeStruct((t, d), F32),
        compiler_params=_params("parallel", "parallel"),
        name="xattn",
    )(*args)


def _inproj_body(h_ref, nw_ref, w_ref, qn_ref, kvn_ref, wqa_ref, wqb_ref, wkn_ref, wvt_ref,
                 cos_ref, sin_ref, z_ref, xbc_ref, dt_ref, q_ref, k_ref, vt_ref, *, cols, scale):
    c_z, c_xbc, c_dt, c_cq, c_ckv, c_kr = cols
    u = _rms(h_ref[...], nw_ref[...]).astype(BF)
    z_ref[...] = _dot(u, w_ref[:, c_z[0]:c_z[1]]).astype(BF)
    xbc_ref[...] = _dot(u, w_ref[:, c_xbc[0]:c_xbc[1]])
    dt_ref[...] = _dot(u, w_ref[:, c_dt[0]:c_dt[1]])
    cqn = _rms(_dot(u, w_ref[:, c_cq[0]:c_cq[1]]), qn_ref[...]).astype(BF)
    ckvn = _rms(_dot(u, w_ref[:, c_ckv[0]:c_ckv[1]]), kvn_ref[...]).astype(BF)
    kr = _dot(u, w_ref[:, c_kr[0]:c_kr[1]])
    cos_t = cos_ref[...]
    sin_t = sin_ref[...]
    kp = kr[:, :LANES] * cos_t + kr[:, LANES:] * sin_t
    qa = _dot(cqn, wqa_ref[...])
    qb = _dot(cqn, wqb_ref[...])
    kn = _dot(ckvn, wkn_ref[...])
    for i in range(qa.shape[-1] // LANES):
        sl = slice(i * LANES, (i + 1) * LANES)
        q_ref[:, sl] = ((qa[:, sl] * cos_t + qb[:, sl] * sin_t) * scale).astype(BF)
        k_ref[:, sl] = (kn[:, sl] + kp).astype(BF)
    vrow = lax.broadcasted_iota(jnp.int32, (vt_ref.shape[0], 1), 0) % (2 * LANES)
    ones = jnp.where((vrow == MLA_V) | (vrow == LANES), 1.0, 0.0)
    vt = lax.dot_general(wvt_ref[...], ckvn, NT_DIMS, preferred_element_type=F32)
    vt_ref[...] = (vt + ones).astype(BF)


def _inproj(h, norm_w, w_all, cols, q_norm, kv_norm, wqa, wqb, wkn, wvt, cos_t, sin_t, layer, e):
    t, d = h.shape
    tm = _tile(t, 1024)
    wc = w_all.shape[-1]
    n_z = cols[0][1] - cols[0][0]
    n_xbc = cols[1][1] - cols[1][0]
    hq = wqa.shape[-1]
    row = lambda i: (i, 0)
    return pl.pallas_call(
        functools.partial(_inproj_body, cols=cols, scale=math.log2(math.e) * (MLA_NOPE + MLA_ROPE) ** -0.5),
        grid=(t // tm,),
        in_specs=[
            pl.BlockSpec((tm, d), row),
            pl.BlockSpec((None, 1, d), lambda i: (layer, 0, 0)),
            _resident((None, d, wc), lambda i: (e, 0, 0)),
            pl.BlockSpec((None, 1, MLA_Q_RANK), lambda i: (e, 0, 0)),
            pl.BlockSpec((None, 1, MLA_KV_RANK), lambda i: (e, 0, 0)),
            _resident((None, MLA_Q_RANK, hq), lambda i: (e, 0, 0)),
            _resident((None, MLA_Q_RANK, hq), lambda i: (e, 0, 0)),
            _resident((None, MLA_KV_RANK, hq), lambda i: (e, 0, 0)),
            _resident((None, hq, MLA_KV_RANK), lambda i: (e, 0, 0)),
            pl.BlockSpec((tm, LANES), row),
            pl.BlockSpec((tm, LANES), row),
        ],
        out_specs=[
            pl.BlockSpec((tm, n_z), row),
            pl.BlockSpec((tm, n_xbc), row),
            pl.BlockSpec((tm, LANES), row),
            pl.BlockSpec((tm, hq), row),
            pl.BlockSpec((tm, hq), row),
            pl.BlockSpec((hq, tm), lambda i: (0, i)),
        ],
        out_shape=[
            jax.ShapeDtypeStruct((t, n_z), BF),
            jax.ShapeDtypeStruct((t, n_xbc), F32),
            jax.ShapeDtypeStruct((t, LANES), F32),
            jax.ShapeDtypeStruct((t, hq), BF),
            jax.ShapeDtypeStruct((t, hq), BF),
            jax.ShapeDtypeStruct((hq, t), BF),
        ],
        compiler_params=_params("parallel"),
        name="mix_inproj",
    )(h, norm_w, w_all, q_norm, kv_norm, wqa, wqb, wkn, wvt, cos_t, sin_t)


def _split3_dot(a_bf, x):
    hi = x.astype(BF)
    r1 = x - hi.astype(F32)
    mid = r1.astype(BF)
    low = (r1 - mid.astype(F32)).astype(BF)
    return _dot(a_bf, hi) + _dot(a_bf, mid) + _dot(a_bf, low)


def _ssd_decay_stage(z, r0, dt_ref, bias_ref, a_ref, sel_ref, rows_ref, dec_ref, wall_ref):
    q = SSD_CHUNK
    nh = SSD_HEADS
    dt = jax.nn.softplus(dt_ref[pl.ds(r0, q), :] + bias_ref[...])
    la = dt * (a_ref[...] * math.log2(math.e))
    row = lax.broadcasted_iota(jnp.int32, (q, q), 0)
    col = lax.broadcasted_iota(jnp.int32, (q, q), 1)
    tril = jnp.where(row >= col, 1.0, 0.0).astype(BF)
    cum = _split3_dot(tril, la)
    tot = cum[q - 1:q, :]
    rev = tot - cum + la
    fwd_lane = lax.broadcasted_iota(jnp.int32, (q, LANES), 1) < nh
    sel = jnp.where(fwd_lane, cum, rev)
    sel_ref[z] = sel
    dt_t = dt.T
    rows_ref[z, 0:2 * nh, :] = (sel - jnp.log2(dt)).T[0:2 * nh]
    rows_ref[z, 2 * nh:3 * nh, :] = jnp.log2(dt_t[0:nh] + dt_t[nh:2 * nh])
    wall_ref[z] = (jnp.exp2(tot - sel) * dt).astype(BF)
    etot = jnp.exp2(tot)
    lo1 = lax.broadcasted_iota(jnp.int32, (1, LANES), 1) < SSD_HEAD_DIM
    for d in range(2):
        for p in range(nh // 2):
            h0 = d * nh + 2 * p
            dec = jnp.where(lo1, etot[:, h0:h0 + 1], etot[:, h0 + 1:h0 + 2])
            dec_ref[z, d, p] = jnp.broadcast_to(dec, dec_ref.shape[3:])


def _ssd_conv_stage(z, j, first, last, xc_ref, xp_ref, xn_ref, cw_ref, cb_ref, sm_ref, x_ref, b_ref, c_ref):
    q = SSD_CHUNK
    inner = SSD_HEADS * SSD_HEAD_DIM
    gn = SSD_GROUPS * SSD_STATE
    half = SSD_CONV // 2
    edge = xp_ref.shape[0]
    nsub = xc_ref.shape[0] // q
    r0 = pl.multiple_of(j * q, q)
    cur = xc_ref[pl.ds(r0, q), :]
    before = xc_ref[pl.ds(pl.multiple_of(jnp.maximum(r0 - edge, 0), edge), edge), :]
    after = xc_ref[pl.ds(pl.multiple_of(jnp.minimum(r0 + q, (nsub - 1) * q + q - edge), edge), edge), :]
    prev = jnp.where(first, 0.0, jnp.where(j == 0, xp_ref[...], before))
    nxt = jnp.where(last, 0.0, jnp.where(j == nsub - 1, xn_ref[...], after))
    fill = jnp.zeros((sm_ref.shape[1] - q - 2 * edge, cur.shape[1]), F32)
    window = jnp.concatenate([prev, cur, nxt, fill], axis=0).astype(BF)
    shifted = _dot(sm_ref[...], window)
    acc = cur * cw_ref[half:half + 1, :] + cb_ref[...]
    blk = 0
    for k in range(SSD_CONV):
        if k != half:
            acc = acc + shifted[blk * q:(blk + 1) * q] * cw_ref[k:k + 1, :]
            blk += 1
    xbc = jax.nn.silu(acc)
    x_ref[z] = xbc[:, :inner].astype(BF)
    for g in range(SSD_GROUPS):
        b_ref[z, g] = xbc[:, inner + g * SSD_STATE:inner + (g + 1) * SSD_STATE].T.astype(BF)
    c_ref[z] = xbc[:, inner + gn:].astype(BF)


def _ssd_state_stage(z, e_ref, st_ref, x_ref, b_ref, wall_ref, dec_ref, carry_ref):
    nh, n = SSD_HEADS, SSD_STATE
    inner = nh * SSD_HEAD_DIM
    gw = inner // SSD_GROUPS
    ppg = gw // LANES
    x = x_ref[z].astype(F32)
    wexp = _dot(wall_ref[z], e_ref[...])
    for d in range(2):
        xw = (x * wexp[:, d * inner:(d + 1) * inner]).astype(BF)
        for g in range(SSD_GROUPS):
            upd = _dot(b_ref[z, g], xw[:, g * gw:(g + 1) * gw])
            for i in range(ppg):
                p = g * ppg + i
                contrib = upd[:, i * LANES:(i + 1) * LANES]
                if d == 0:
                    state = carry_ref[0, p]
                    st_ref[z, 0, p] = state
                    carry_ref[0, p] = state * dec_ref[z, 0, p][0:1] + contrib
                else:
                    st_ref[z, 1, p] = contrib


def _ssd_output_phase(z, j, z_ref, d_ref, nw_ref, o_ref, st_ref, x_ref, b_ref, c_ref, sel_ref, rows_ref,
                      dec_ref, carry_ref):
    q = SSD_CHUNK
    r0 = pl.multiple_of(j * q, q)
    nh, n, hpg = SSD_HEADS, SSD_STATE, SSD_HEADS // SSD_GROUPS
    sel = sel_ref[z]
    rows = rows_ref[z]
    row = lax.broadcasted_iota(jnp.int32, (q, q), 0)
    col = lax.broadcasted_iota(jnp.int32, (q, q), 1)
    lower = row > col
    diag = row == col
    lo = lax.broadcasted_iota(jnp.int32, (q, LANES), 1) < SSD_HEAD_DIM
    ys = []
    for g in range(SSD_GROUPS):
        cg = c_ref[z, :, g * n:(g + 1) * n]
        cb = _dot(cg, b_ref[z, g])
        cg32 = cg.astype(F32)
        for j in range(hpg // 2):
            p = g * (hpg // 2) + j
            lhs = []
            for h in (2 * p, 2 * p + 1):
                hb = nh + h
                a_f = jnp.broadcast_to(sel[:, h:h + 1], (q, q))
                a_b = jnp.broadcast_to(sel[:, hb:hb + 1], (q, q))
                seg = jnp.where(lower, a_f - rows[h:h + 1, :],
                                jnp.where(diag, rows[2 * nh + h:2 * nh + h + 1, :], a_b - rows[hb:hb + 1, :]))
                m = (cb * jnp.exp2(seg)).astype(BF)
                cef = (cg32 * jnp.exp2(a_f)).astype(BF)
                ceb = (cg32 * jnp.exp2(a_b)).astype(BF)
                lhs.append(jnp.concatenate([m, cef, ceb], axis=1))
            xp = x_ref[z, :, p * LANES:(p + 1) * LANES]
            back = carry_ref[1, p]
            carry_ref[1, p] = back * dec_ref[z, 1, p][0:1] + st_ref[z, 1, p]
            rhs = jnp.concatenate([xp, st_ref[z, 0, p].astype(BF), back.astype(BF)], axis=0)
            out = _dot(jnp.concatenate(lhs, axis=0), rhs)
            ys.append(jnp.where(lo, out[:q], out[q:]) + xp.astype(F32) * d_ref[:, p * LANES:(p + 1) * LANES])
    y = jnp.concatenate(ys, axis=-1)
    gated = y * jax.nn.silu(z_ref[pl.ds(r0, q), :].astype(F32))
    o_ref[pl.ds(r0, q), :] = _rms(gated, nw_ref[...]).astype(BF)


def _ssd_body(xc_ref, xp_ref, xn_ref, dt_ref, z_ref, cw_ref, cb_ref, bias_ref, a_ref, d_ref, nw_ref, e_ref, sm_ref,
              o_ref, st_ref, x_ref, b_ref, c_ref, sel_ref, rows_ref, dec_ref, wall_ref, carry_ref):
    t = pl.program_id(1)
    nc = st_ref.shape[0]
    nsub = xc_ref.shape[0] // SSD_CHUNK
    nb = nc // nsub

    @pl.when(t == 0)
    def _():
        carry_ref[0] = jnp.zeros(carry_ref.shape[1:], F32)

    @pl.when(t == nb)
    def _():
        carry_ref[1] = jnp.zeros(carry_ref.shape[1:], F32)

    @pl.when(t < nb)
    def _():
        base = t * nsub
        for j in range(nsub):
            _ssd_decay_stage(base + j, j * SSD_CHUNK, dt_ref, bias_ref, a_ref, sel_ref, rows_ref, dec_ref, wall_ref)

        def conv(j):
            z = base + j
            _ssd_conv_stage(z, j, z == 0, z == nc - 1, xc_ref, xp_ref, xn_ref, cw_ref, cb_ref, sm_ref,
                            x_ref, b_ref, c_ref)

        def state(j):
            _ssd_state_stage(base + j, e_ref, st_ref, x_ref, b_ref, wall_ref, dec_ref, carry_ref)

        def sub(j, carry):
            conv(j)
            state(j - 1)
            return carry

        conv(0)
        lax.fori_loop(1, nsub, sub, 0)
        state(nsub - 1)

    @pl.when(t >= nb)
    def _():
        def sub(i, carry):
            j = nsub - 1 - i
            z = (2 * nb - 1 - t) * nsub + j
            _ssd_output_phase(z, j, z_ref, d_ref, nw_ref, o_ref, st_ref, x_ref, b_ref, c_ref, sel_ref,
                              rows_ref, dec_ref, carry_ref)
            return carry
        lax.fori_loop(0, nsub, sub, 0)


def _conv_shift_table(q, edge, rows):
    half = SSD_CONV // 2
    m = np.zeros(((SSD_CONV - 1) * q, rows), np.float32)
    blk = 0
    for k in range(SSD_CONV):
        if k != half:
            m[blk * q + np.arange(q), edge + np.arange(q) + k - half] = 1.0
            blk += 1
    return jnp.asarray(m, BF)


def _head_expand_table():
    inner = SSD_HEADS * SSD_HEAD_DIM
    e = np.zeros((LANES, 2 * inner), np.float32)
    for h in range(2 * SSD_HEADS):
        e[h, h * SSD_HEAD_DIM:(h + 1) * SSD_HEAD_DIM] = 1.0
    return jnp.asarray(e, BF)


def _ssd(xbc, dt_raw, z, conv_w, conv_b, dt_bias, a_neg, d_exp, ssd_norm, e, batch):
    t, c = xbc.shape
    s = t // batch
    inner = SSD_HEADS * SSD_HEAD_DIM
    gn = SSD_GROUPS * SSD_STATE
    q = SSD_CHUNK
    nc = s // q
    nsub = next(n for n in (8, 4, 2, 1) if nc % n == 0)
    nb = nc // nsub
    rows = nsub * q
    edge = 8
    epb = rows // edge
    blk1 = lambda i: jnp.minimum(i, nb - 1)
    blk3 = lambda i: nb - 1 - jnp.maximum(i - nb, 0)
    par = lambda b, i: (e, 0, 0)
    window_rows = 2 * q
    return pl.pallas_call(
        _ssd_body,
        grid=(batch, 2 * nb),
        in_specs=[
            pl.BlockSpec((rows, c), lambda b, i: (b * nb + blk1(i), 0)),
            pl.BlockSpec((edge, c), lambda b, i: (b * nb * epb + jnp.maximum(blk1(i) * epb - 1, 0), 0)),
            pl.BlockSpec((edge, c), lambda b, i: (b * nb * epb + jnp.minimum((blk1(i) + 1) * epb, nb * epb - 1), 0)),
            pl.BlockSpec((rows, LANES), lambda b, i: (b * nb + blk1(i), 0)),
            pl.BlockSpec((rows, inner), lambda b, i: (b * nb + blk3(i), 0)),
            pl.BlockSpec((None, SSD_CONV, c), par),
            pl.BlockSpec((None, 1, c), par),
            pl.BlockSpec((None, 1, LANES), par),
            pl.BlockSpec((None, 1, LANES), par),
            pl.BlockSpec((None, 1, inner), par),
            pl.BlockSpec((None, 1, inner), par),
            _resident((LANES, 2 * inner), lambda b, i: (0, 0)),
            _resident(((SSD_CONV - 1) * q, window_rows), lambda b, i: (0, 0)),
        ],
        out_specs=pl.BlockSpec((rows, inner), lambda b, i: (b * nb + blk3(i), 0)),
        out_shape=jax.ShapeDtypeStruct((t, inner), BF),
        scratch_shapes=[
            pltpu.VMEM((nc, 2, SSD_HEADS // 2, SSD_STATE, LANES), F32),
            pltpu.VMEM((nc, q, inner), BF),
            pltpu.VMEM((nc, SSD_GROUPS, SSD_STATE, q), BF),
            pltpu.VMEM((nc, q, gn), BF),
            pltpu.VMEM((nc, q, LANES), F32),
            pltpu.VMEM((nc, 3 * SSD_HEADS, q), F32),
            pltpu.VMEM((nc, 2, SSD_HEADS // 2, edge, LANES), F32),
            pltpu.VMEM((nc, q, LANES), BF),
            pltpu.VMEM((2, SSD_HEADS // 2, SSD_STATE, LANES), F32),
        ],
        compiler_params=_params("parallel", "arbitrary"),
        name="ssd",
    )(xbc, xbc, xbc, dt_raw, z, conv_w, conv_b, dt_bias, a_neg, d_exp, ssd_norm, _head_expand_table(),
      _conv_shift_table(q, edge, window_rows))


def _lane_fold(x, op):
    out = x[:, :LANES]
    for j in range(1, x.shape[-1] // LANES):
        out = op(out, x[:, j * LANES:(j + 1) * LANES])
    return out


def _mla_body(q_ref, k_ref, vt_ref, o_ref, s_ref, m_ref, *, kt):
    nk = k_ref.shape[0]
    rb = s_ref.shape[2]
    nrb = q_ref.shape[0] // rb
    nheads = q_ref.shape[1] // LANES

    def score_pass(r, head):
        rows = pl.ds(pl.multiple_of(r * rb, rb), rb)
        sl = slice(head * LANES, (head + 1) * LANES)
        slot = head % 2
        s = lax.dot_general(k_ref[:, sl], q_ref[rows, sl], NT_DIMS, preferred_element_type=F32)
        s_ref[slot] = s
        m_ref[slot] = jnp.max(s, axis=0, keepdims=True)

    def value_pass(head):
        hs = slice(head * LANES, (head + 1) * LANES)
        slot = head % 2
        m = m_ref[slot]
        o = jnp.zeros((LANES, rb), F32)
        for c in range(0, nk, kt):
            p = jnp.exp2(s_ref[slot, c:c + kt, :] - m)
            o = o + _dot(vt_ref[hs, c:c + kt], p.astype(BF))
        return o

    def row_block(r, carry):
        top = lax.broadcasted_iota(jnp.int32, (LANES, rb), 0) < MLA_V
        outs = []
        for head in range(nheads):
            if head + 1 < nheads:
                score_pass(r, head + 1)
            else:
                score_pass(jnp.minimum(r + 1, nrb - 1), 0)
            outs.append(value_pass(head))
        pairs = [jnp.where(top, o0 / o0[MLA_V:MLA_V + 1, :], o1 / o1[0:1, :]).T
                 for o0, o1 in zip(outs[0::2], outs[1::2])]
        o_ref[pl.ds(pl.multiple_of(r * rb, rb), rb), :] = jnp.concatenate(pairs, axis=-1).astype(BF)
        return carry

    score_pass(0, 0)
    lax.fori_loop(0, nrb, row_block, 0)


def _mla(q, k, vt, batch):
    t, hq = q.shape
    s = t // batch
    rb = _tile(s, 512)
    nv = MLA_HEADS * MLA_V
    q3, k3 = (a.reshape(batch, s, hq) for a in (q, k))
    blk = pl.BlockSpec((None, s, hq), lambda b: (b, 0, 0))
    o = pl.pallas_call(
        functools.partial(_mla_body, kt=_tile(s, 256)),
        grid=(batch,),
        in_specs=[blk, blk, pl.BlockSpec((hq, s), lambda b: (0, b))],
        out_specs=pl.BlockSpec((None, s, nv), lambda b: (b, 0, 0)),
        out_shape=jax.ShapeDtypeStruct((batch, s, nv), BF),
        scratch_shapes=[pltpu.VMEM((2, s, rb), F32), pltpu.VMEM((2, 1, rb), F32)],
        compiler_params=_params("parallel"),
        name="mla_attn",
    )(q3, k3, vt)
    return o.reshape(t, nv)


def _fnet_fold_body(ha_ref, hm_ref, hx_ref, nw_ref, cc_ref, sc_ref, ec_ref, es_ref):
    tm = ha_ref.shape[0]
    nw = nw_ref[...]
    u_a = _rms(ha_ref[...], nw)
    u_m = _rms(hm_ref[...], nw)
    u_x = _rms(hx_ref[...], nw)[0:1]
    r = lax.broadcasted_iota(jnp.int32, (tm, tm), 0)
    c = lax.broadcasted_iota(jnp.int32, (tm, tm), 1)
    perm = jnp.where(r + c == tm, 1.0, 0.0).astype(BF)
    hi = u_m.astype(BF)
    low = (u_m - hi.astype(F32)).astype(BF)
    mirror = _dot(perm, hi) + _dot(perm, low)
    first = lax.broadcasted_iota(jnp.int32, u_a.shape, 0) == 0
    mirror = jnp.where(first, u_x, mirror)
    ue = (u_a + mirror).astype(BF)
    uo = (u_a - mirror).astype(BF)
    gc = cc_ref.shape[0]
    for g in range(ue.shape[-1] // gc):
        sl = slice(g * gc, (g + 1) * gc)
        ec_ref[:, sl] = _dot(ue[:, sl], cc_ref[...]).astype(BF)
        es_ref[:, sl] = _dot(uo[:, sl], sc_ref[...]).astype(BF)


def _fnet_seq_body(h_ref, hh_ref, nw_ref, cc_ref, cs_ref, ss_ref, ec_ref, es_ref, w_ref, o_ref, *, scale):
    ts = h_ref.shape[0]
    u_h = _rms(hh_ref[...], nw_ref[...]).astype(BF)
    gc = cc_ref.shape[0]
    x_h = jnp.concatenate([_dot(u_h[:, g * gc:(g + 1) * gc], cc_ref[...])
                           for g in range(u_h.shape[-1] // gc)], axis=-1)[0:1]
    hr = ts // 2 if ts % 16 == 0 else ts
    odd = lax.broadcasted_iota(jnp.int32, (hr, 1), 0) % 2 == 1
    for r0 in range(0, ts, hr):
        rows = pl.ds(pl.multiple_of(pl.program_id(1) * ts + r0, hr), hr)
        y = _dot(cs_ref[rows, :], ec_ref[...]) - _dot(ss_ref[rows, :], es_ref[...])
        y = y + jnp.where(odd, -1.0, 1.0) * x_h
        o_ref[r0:r0 + hr, :] = h_ref[r0:r0 + hr, :] + _dot((y * scale).astype(BF), w_ref[...])


def _dft_tables(n, fold):
    j = np.arange(n)[:, None]
    k = np.arange(n // 2 if fold else n)[None, :]
    ang = ((j * k) % n) * (2.0 * np.pi / n)
    cos, sin = np.cos(ang), np.sin(ang)
    if fold:
        cos[:, 0] = 0.5
    return jnp.asarray(cos, BF), jnp.asarray(sin, BF)


def _fnet(h, norm_w, w_out, layer, o, batch, tables):
    t, d = h.shape
    s = t // batch
    cc, sc, cs, ss = tables
    gc = cc.shape[0]
    sub = 8
    tm = _tile(s // 2, 256)
    nt, nf = s // tm, s // 2 // tm
    nwspec = pl.BlockSpec((None, 1, d), lambda b, i: (layer, 0, 0))
    table = _resident((gc, gc), lambda b, i: (0, 0))
    ec, es = pl.pallas_call(
        _fnet_fold_body,
        grid=(batch, nf),
        in_specs=[
            pl.BlockSpec((tm, d), lambda b, i: (b * nt + i, 0)),
            pl.BlockSpec((tm, d), lambda b, i: (b * nt + nt - 1 - i, 0)),
            pl.BlockSpec((sub, d), lambda b, i: (b * (s // sub) + ((nt - i) % nt) * (tm // sub), 0)),
            nwspec, table, table,
        ],
        out_specs=[pl.BlockSpec((tm, d), lambda b, i: (b * nf + i, 0))] * 2,
        out_shape=[jax.ShapeDtypeStruct((t // 2, d), BF)] * 2,
        compiler_params=_params("parallel", "parallel"),
        name="fnet_fold_channel_dft",
    )(h, h, h, norm_w, cc, sc)
    ts = _tile(s, 512)
    ns = s // ts
    return pl.pallas_call(
        functools.partial(_fnet_seq_body, scale=(s * gc) ** -0.5),
        grid=(batch, ns),
        in_specs=[
            pl.BlockSpec((ts, d), lambda b, i: (b * ns + i, 0)),
            pl.BlockSpec((sub, d), lambda b, i: (b * (s // sub) + s // 2 // sub, 0)),
            nwspec, table,
            _resident((s, s // 2), lambda b, i: (0, 0)),
            _resident((s, s // 2), lambda b, i: (0, 0)),
            pl.BlockSpec((s // 2, d), lambda b, i: (b, 0)),
            pl.BlockSpec((s // 2, d), lambda b, i: (b, 0)),
            _resident((None, d, d), lambda b, i: (o, 0, 0)),
        ],
        out_specs=pl.BlockSpec((ts, d), lambda b, i: (b * ns + i, 0)),
        out_shape=jax.ShapeDtypeStruct((t, d), F32),
        compiler_params=_params("parallel", "arbitrary"),
        name="fnet_seq_dft",
    )(h, h, norm_w, cc, cs, ss, ec, es, w_out)


def _mixer_weights(w_in, w_uq, w_ukv, dt_bias, a_log, ssd_d):
    w_in, w_uq, w_ukv = w_in.astype(BF), w_uq.astype(BF), w_ukv.astype(BF)
    ne, d, _ = w_in.shape
    inner = SSD_HEADS * SSD_HEAD_DIM
    conv_ch = inner + 2 * SSD_GROUPS * SSD_STATE
    o_z, o_xbc = 0, inner
    o_dt = o_xbc + conv_ch
    o_cq = o_dt + 2 * SSD_HEADS
    o_ckv = o_cq + MLA_Q_RANK
    o_kr = o_ckv + MLA_KV_RANK
    half = MLA_ROPE // 2
    pad = LANES - MLA_NOPE - MLA_ROPE
    zeros = lambda *s: jnp.zeros(s, w_in.dtype)
    w_dt = jnp.concatenate([w_in[:, :, o_dt:o_cq], zeros(ne, d, LANES - 2 * SSD_HEADS)], axis=-1)
    kr1 = w_in[:, :, o_kr:o_kr + half]
    kr2 = w_in[:, :, o_kr + half:o_kr + MLA_ROPE]
    kr_a = jnp.concatenate([zeros(ne, d, MLA_NOPE), kr1, kr2, zeros(ne, d, pad)], axis=-1)
    kr_b = jnp.concatenate([zeros(ne, d, MLA_NOPE), -kr2, kr1, zeros(ne, d, pad)], axis=-1)
    pieces = [w_in[:, :, o_z:o_xbc], w_in[:, :, o_xbc:o_dt], w_dt, w_in[:, :, o_cq:o_ckv],
              w_in[:, :, o_ckv:o_kr], jnp.concatenate([kr_a, kr_b], axis=-1)]
    cols, c = [], 0
    for p in pieces:
        cols.append((c, c + p.shape[-1]))
        c += p.shape[-1]
    w_all = jnp.concatenate(pieces, axis=-1).astype(BF)

    uq = w_uq.reshape(ne, MLA_Q_RANK, MLA_HEADS, MLA_NOPE + MLA_ROPE)
    q_nope, q1, q2 = uq[..., :MLA_NOPE], uq[..., MLA_NOPE:MLA_NOPE + half], uq[..., MLA_NOPE + half:]
    zq = lambda n: jnp.zeros((ne, MLA_Q_RANK, MLA_HEADS, n), w_uq.dtype)
    hq = MLA_HEADS * LANES
    wqa = jnp.concatenate([q_nope, q1, q2, zq(pad)], axis=-1).reshape(ne, MLA_Q_RANK, hq).astype(BF)
    wqb = jnp.concatenate([zq(MLA_NOPE), -q2, q1, zq(pad)], axis=-1).reshape(ne, MLA_Q_RANK, hq).astype(BF)

    ukv = w_ukv.reshape(ne, MLA_KV_RANK, MLA_HEADS, MLA_NOPE + MLA_V)
    zkv = lambda *s: jnp.zeros((ne, MLA_KV_RANK) + s, w_ukv.dtype)
    wkn = jnp.concatenate([ukv[..., :MLA_NOPE], zkv(MLA_HEADS, LANES - MLA_NOPE)], axis=-1)
    wkn = wkn.reshape(ne, MLA_KV_RANK, hq).astype(BF)
    vv = ukv[..., MLA_NOPE:].reshape(ne, MLA_KV_RANK, MLA_HEADS // 2, 2, MLA_V)
    zv = zkv(MLA_HEADS // 2, MLA_V)
    wv = jnp.stack([jnp.concatenate([vv[:, :, :, 0], zv], axis=-1),
                    jnp.concatenate([zv, vv[:, :, :, 1]], axis=-1)], axis=3)
    wvt = jnp.swapaxes(wv.reshape(ne, MLA_KV_RANK, hq), 1, 2).astype(BF)

    padl = lambda a: jnp.concatenate([a, jnp.zeros((ne, LANES - a.shape[-1]), a.dtype)], axis=-1)[:, None, :]
    bias = padl(dt_bias.reshape(ne, 2 * SSD_HEADS))
    a_neg = padl(-jnp.exp(a_log.reshape(ne, 2 * SSD_HEADS)))
    d_exp = jnp.repeat(ssd_d, SSD_HEAD_DIM, axis=-1)[:, None, :]
    return w_all, tuple(cols), wqa, wqb, wkn, wvt, bias, a_neg, d_exp


def _rope_tables(positions):
    inv = 1.0 / (ROPE_THETA ** (jnp.arange(0, MLA_ROPE, 2, dtype=F32) / MLA_ROPE))
    ang = inv[:, None] * positions.astype(F32).reshape(1, -1)
    cos, sin = jnp.cos(ang), jnp.sin(ang)
    t = cos.shape[1]
    pad = LANES - MLA_NOPE - MLA_ROPE
    cos_t = jnp.concatenate([jnp.ones((MLA_NOPE, t), F32), cos, cos, jnp.zeros((pad, t), F32)], axis=0).T
    sin_t = jnp.concatenate([jnp.zeros((MLA_NOPE, t), F32), sin, sin, jnp.zeros((pad, t), F32)], axis=0).T
    return cos_t, sin_t


def kernel(x, mem, positions, mem_norm, final_norm, ffn1_norm, ffn1_w_gu, ffn1_w_down, mix_norm, xa_norm,
           xa_wq, xa_wkv, xa_wo, ffn2_norm, ffn2_w_gu, ffn2_w_down, w_in, conv_w, conv_b, dt_bias, a_log,
           ssd_d, ssd_norm, q_norm, w_uq, kv_norm, w_ukv, w_out, fnet_w_out):
    batch, seq, d = x.shape
    depth = ffn1_norm.shape[0]
    t = batch * seq
    bf = lambda a: a.astype(BF)
    row3 = lambda a: a[:, None, :]

    kv = _kvproj(mem.reshape(-1, d), mem_norm[None, :], bf(xa_wkv)).reshape(depth, batch, mem.shape[1], 2 * d)
    w_all, cols, wqa, wqb, wkn, wvt, bias, a_neg, d_exp = _mixer_weights(w_in, w_uq, w_ukv, dt_bias, a_log, ssd_d)
    cos_t, sin_t = _rope_tables(positions)
    gc = d // FNET_GROUPS
    tables = _dft_tables(gc, fold=False) + _dft_tables(seq, fold=True)
    f1n, f2n, mxn, xan = row3(ffn1_norm), row3(ffn2_norm), row3(mix_norm), row3(xa_norm)
    gu, dn = bf(ffn1_w_gu[0]), bf(ffn1_w_down[0])
    wq, wo, w_mix_out, w_fnet = bf(xa_wq), bf(xa_wo), bf(w_out), bf(fnet_w_out)
    ssd_nw, qn, kvn = row3(ssd_norm), row3(q_norm), row3(kv_norm)

    h = x.reshape(t, d)
    for layer in range(depth):
        h, gu, dn = _ffn(h, f1n, gu, dn, layer, cast_next=(ffn2_w_gu, ffn2_w_down, layer))
        if layer % 2 == 0:
            e = layer // 2
            z, xbc, dt_raw, q, k, vt = _inproj(h, mxn, w_all, cols, qn, kvn, wqa, wqb, wkn, wvt,
                                              cos_t, sin_t, layer, e)
            y_ssd = _ssd(xbc, dt_raw, z, conv_w, row3(conv_b), bias, a_neg, d_exp, ssd_nw, e, batch)
            o_mla = _mla(q, k, vt, batch)
            mix = (y_ssd, o_mla, w_mix_out, e)
        else:
            h = _fnet(h, mxn, w_fnet, layer, layer // 2, batch, tables)
            mix = None
        h = _xattn(h, xan, wq, kv, wo, layer, batch, mix=mix)
        if layer + 1 < depth:
            h, gu, dn = _ffn(h, f2n, gu, dn, layer, cast_next=(ffn1_w_gu, ffn1_w_down, layer + 1))
        else:
            h = _ffn(h, f2n, gu, dn, layer, final_w=final_norm[None, :])
    return h.reshape(batch, seq, d)
```

```python
import functools
import math

import numpy as np
import jax
import jax.numpy as jnp
from jax import lax
from jax.experimental import pallas as pl
from jax.experimental.pallas import tpu as pltpu

EPS = 1e-6
BF = jnp.bfloat16
F32 = jnp.float32

V7X_VMEM_BYTES = 64 * 1024 * 1024
VMEM_LIMIT = V7X_VMEM_BYTES - 8 * 1024 * 1024
LANES = 128

SSD_HEADS = 16
SSD_HEAD_DIM = 64
SSD_GROUPS = 2
SSD_STATE = 128
SSD_CONV = 5
SSD_CHUNK = 128
MLA_HEADS = 8
MLA_Q_RANK = 512
MLA_KV_RANK = 256
MLA_NOPE = 64
MLA_ROPE = 32
MLA_V = 64
ROPE_THETA = 10000.0
FNET_GROUPS = 4
XA_HEADS = 4

NT_DIMS = (((1,), (1,)), ((), ()))
TN_DIMS = (((0,), (0,)), ((), ()))


def _params(*sem):
    return pltpu.CompilerParams(dimension_semantics=sem, vmem_limit_bytes=VMEM_LIMIT)


def _resident(shape, index_map):
    return pl.BlockSpec(shape, index_map, pipeline_mode=pl.Buffered(1))


def _rms(x, w):
    return x * lax.rsqrt(jnp.mean(x * x, axis=-1, keepdims=True) + EPS) * w


def _dot(a, b):
    return jnp.dot(a, b, preferred_element_type=F32)


def _tile(n, pref):
    t = min(n, pref)
    assert n % t == 0, (n, t)
    return t


def _ffn_body(*refs, chunks, final, cast):
    refs = list(refs)
    h_ref, nw_ref, wg_ref, wu_ref, wd_ref = refs[:5]
    del refs[:5]
    fw_ref = refs.pop(0) if final else None
    if cast:
        gu_src, dn_src = refs.pop(0), refs.pop(0)
        o_ref, gu_dst, dn_dst = refs
        gu_dst[...] = gu_src[...].astype(BF)
        dn_dst[...] = dn_src[...].astype(BF)
    else:
        (o_ref,) = refs
    h = h_ref[...]
    xn = _rms(h, nw_ref[...]).astype(BF)
    acc = jnp.zeros(h.shape, F32)
    for a, b in chunks:
        g = _dot(xn, wg_ref[:, a:b])
        u = _dot(xn, wu_ref[:, a:b])
        act = (jax.nn.silu(g) * u).astype(BF)
        acc = acc + _dot(act, wd_ref[a:b, :])
    out = h + 0.5 * acc
    if final:
        out = _rms(out, fw_ref[...])
    o_ref[...] = out


def _ffn(h, norm_w, w_gu, w_down, layer, final_w=None, cast_next=None):
    t, d = h.shape
    f = w_down.shape[0]
    tm = _tile(t, 1024)
    steps = t // tm
    step = 768
    chunks = tuple((a, min(a + step, f)) for a in range(0, f, step))
    row = lambda i: (i, 0)
    in_specs = [
        pl.BlockSpec((tm, d), row),
        pl.BlockSpec((None, 1, d), lambda i: (layer, 0, 0)),
        _resident((d, f), lambda i: (0, 0)),
        _resident((d, f), lambda i: (0, 1)),
        _resident((f, d), lambda i: (0, 0)),
    ]
    args = [h, norm_w, w_gu, w_gu, w_down]
    out_specs = [pl.BlockSpec((tm, d), row)]
    out_shape = [jax.ShapeDtypeStruct((t, d), F32)]
    if final_w is not None:
        in_specs.append(pl.BlockSpec((1, d), lambda i: (0, 0)))
        args.append(final_w)
    if cast_next is not None:
        src_gu, src_dn, nxt = cast_next
        nblk = next(n for n in (16, 8, 4, 2, 1) if steps % n == 0)
        rep = steps // nblk
        assert d % (16 * nblk) == 0 and f % (16 * nblk) == 0
        in_specs += [pl.BlockSpec((None, d // nblk, 2 * f), lambda i: (nxt, i // rep, 0)),
                     pl.BlockSpec((None, f // nblk, d), lambda i: (nxt, i // rep, 0))]
        args += [src_gu, src_dn]
        out_specs += [pl.BlockSpec((d // nblk, 2 * f), lambda i: (i // rep, 0)),
                      pl.BlockSpec((f // nblk, d), lambda i: (i // rep, 0))]
        out_shape += [jax.ShapeDtypeStruct((d, 2 * f), BF), jax.ShapeDtypeStruct((f, d), BF)]
    outs = pl.pallas_call(
        functools.partial(_ffn_body, chunks=chunks, final=final_w is not None, cast=cast_next is not None),
        grid=(steps,),
        in_specs=in_specs,
        out_specs=out_specs,
        out_shape=out_shape,
        compiler_params=_params("arbitrary" if cast_next is not None else "parallel"),
        name="ffn",
    )(*args)
    return outs if cast_next is not None else outs[0]


def _cast_body(src_ref, dst_ref):
    dst_ref[...] = src_ref[...].astype(BF)


def _cast_layer(w, layer):
    _, r, c = w.shape
    nblk = next(n for n in (8, 4, 2, 1) if r % (16 * n) == 0)
    return pl.pallas_call(
        _cast_body,
        grid=(nblk,),
        in_specs=[pl.BlockSpec((None, r // nblk, c), lambda i: (layer, i, 0))],
        out_specs=pl.BlockSpec((r // nblk, c), lambda i: (i, 0)),
        out_shape=jax.ShapeDtypeStruct((r, c), BF),
        compiler_params=_params("parallel"),
        name="cast_weights",
    )(w)


def _kvproj_body(m_ref, nw_ref, w_ref, o_ref):
    mn = _rms(m_ref[...], nw_ref[...]).astype(BF)
    o_ref[...] = _dot(mn, w_ref[...]).astype(BF)


def _kvproj(mem2d, mem_norm, wkv):
    n, d = mem2d.shape
    nl, _, d2 = wkv.shape
    tm = _tile(n, 512)
    return pl.pallas_call(
        _kvproj_body,
        grid=(nl, n // tm),
        in_specs=[
            pl.BlockSpec((tm, d), lambda l, i: (i, 0)),
            pl.BlockSpec((1, d), lambda l, i: (0, 0)),
            pl.BlockSpec((None, d, d2), lambda l, i: (l, 0, 0)),
        ],
        out_specs=pl.BlockSpec((None, tm, d2), lambda l, i: (l, i, 0)),
        out_shape=jax.ShapeDtypeStruct((nl, n, d2), BF),
        compiler_params=_params("parallel", "parallel"),
        name="xa_kvproj",
    )(mem2d, mem_norm, wkv)


def _xa_body(*refs, heads, mixed):
    if mixed:
        h_ref, nw_ref, wq_ref, k_ref, v_ref, wo_ref, y_ref, a_ref, wy_ref, wa_ref, o_ref = refs
        h = h_ref[...] + _dot(y_ref[...], wy_ref[...]) + _dot(a_ref[...], wa_ref[...])
    else:
        h_ref, nw_ref, wq_ref, k_ref, v_ref, wo_ref, o_ref = refs
        h = h_ref[...]
    hn = _rms(h, nw_ref[...]).astype(BF)
    dh = h.shape[-1] // heads
    q = (_dot(hn, wq_ref[...]) * (math.log2(math.e) * dh ** -0.5)).astype(BF)
    outs = []
    for i in range(heads):
        sl = slice(i * dh, (i + 1) * dh)
        s = lax.dot_general(q[:, sl], k_ref[:, sl], NT_DIMS, preferred_element_type=F32)
        p = jnp.exp2(s - jnp.max(s, axis=-1, keepdims=True))
        l = jnp.sum(p, axis=-1, keepdims=True)
        outs.append((_dot(p.astype(BF), v_ref[:, sl]) / l).astype(BF))
    o = jnp.concatenate(outs, axis=-1)
    o_ref[...] = h + _dot(o, wo_ref[...])


def _xattn(h, norm_w, wq, kv, wo, layer, batch, mix=None):
    t, d = h.shape
    s = t // batch
    nm = kv.shape[2]
    tm = _tile(s, 1024)
    ns = s // tm
    row = lambda b, i: (b * ns + i, 0)
    in_specs = [
        pl.BlockSpec((tm, d), row),
        pl.BlockSpec((None, 1, d), lambda b, i: (layer, 0, 0)),
        _resident((None, d, d), lambda b, i: (layer, 0, 0)),
        pl.BlockSpec((None, None, nm, d), lambda b, i: (layer, b, 0, 0)),
        pl.BlockSpec((None, None, nm, d), lambda b, i: (layer, b, 0, 1)),
        _resident((None, d, d), lambda b, i: (layer, 0, 0)),
    ]
    args = [h, norm_w, wq, kv, kv, wo]
    if mix is not None:
        y_ssd, o_mla, w_out, e = mix
        ny, na = y_ssd.shape[1], o_mla.shape[1]
        assert ny % na == 0
        in_specs += [
            pl.BlockSpec((tm, ny), row),
            pl.BlockSpec((tm, na), row),
            _resident((None, ny, d), lambda b, i: (e, 0, 0)),
            _resident((None, na, d), lambda b, i: (e, ny // na, 0)),
        ]
        args += [y_ssd, o_mla, w_out, w_out]
    return pl.pallas_call(
        functools.partial(_xa_body, heads=XA_HEADS, mixed=mix is not None),
        grid=(batch, ns),
        in_specs=in_specs,
        out_specs=pl.BlockSpec((tm, d), row),
        out_shape=jax.ShapeDtypeStruct((t, d), F32),
        compiler_params=_params("parallel", "parallel"),
        name="xattn",
    )(*args)


def _inproj_body(h_ref, nw_ref, w_ref, qn_ref, kvn_ref, wqa_ref, wqb_ref, wkn_ref, wvt_ref,
                 cos_ref, sin_ref, z_ref, xbc_ref, dt_ref, q_ref, k_ref, vt_ref, *, cols, scale):
    c_z, c_xbc, c_dt, c_cq, c_ckv, c_kr = cols
    u = _rms(h_ref[...], nw_ref[...]).astype(BF)
    z_ref[...] = _dot(u, w_ref[:, c_z[0]:c_z[1]]).astype(BF)
    xbc_ref[...] = _dot(u, w_ref[:, c_xbc[0]:c_xbc[1]])
    dt_ref[...] = _dot(u, w_ref[:, c_dt[0]:c_dt[1]])
    cqn = _rms(_dot(u, w_ref[:, c_cq[0]:c_cq[1]]), qn_ref[...]).astype(BF)
    ckvn = _rms(_dot(u, w_ref[:, c_ckv[0]:c_ckv[1]]), kvn_ref[...]).astype(BF)
    kr = _dot(u, w_ref[:, c_kr[0]:c_kr[1]])
    cos_t = cos_ref[...]
    sin_t = sin_ref[...]
    kp = kr[:, :LANES] * cos_t + kr[:, LANES:] * sin_t
    qa = _dot(cqn, wqa_ref[...])
    qb = _dot(cqn, wqb_ref[...])
    kn = _dot(ckvn, wkn_ref[...])
    for i in range(qa.shape[-1] // LANES):
        sl = slice(i * LANES, (i + 1) * LANES)
        q_ref[:, sl] = ((qa[:, sl] * cos_t + qb[:, sl] * sin_t) * scale).astype(BF)
        k_ref[:, sl] = (kn[:, sl] + kp).astype(BF)
    vrow = lax.broadcasted_iota(jnp.int32, (vt_ref.shape[0], 1), 0) % (2 * LANES)
    ones = jnp.where((vrow == MLA_V) | (vrow == LANES), 1.0, 0.0)
    vt = lax.dot_general(wvt_ref[...], ckvn, NT_DIMS, preferred_element_type=F32)
    vt_ref[...] = (vt + ones).astype(BF)


def _inproj(h, norm_w, w_all, cols, q_norm, kv_norm, wqa, wqb, wkn, wvt, cos_t, sin_t, layer, e):
    t, d = h.shape
    tm = _tile(t, 1024)
    wc = w_all.shape[-1]
    n_z = cols[0][1] - cols[0][0]
    n_xbc = cols[1][1] - cols[1][0]
    hq = wqa.shape[-1]
    row = lambda i: (i, 0)
    return pl.pallas_call(
        functools.partial(_inproj_body, cols=cols, scale=math.log2(math.e) * (MLA_NOPE + MLA_ROPE) ** -0.5),
        grid=(t // tm,),
        in_specs=[
            pl.BlockSpec((tm, d), row),
            pl.BlockSpec((None, 1, d), lambda i: (layer, 0, 0)),
            _resident((None, d, wc), lambda i: (e, 0, 0)),
            pl.BlockSpec((None, 1, MLA_Q_RANK), lambda i: (e, 0, 0)),
            pl.BlockSpec((None, 1, MLA_KV_RANK), lambda i: (e, 0, 0)),
            _resident((None, MLA_Q_RANK, hq), lambda i: (e, 0, 0)),
            _resident((None, MLA_Q_RANK, hq), lambda i: (e, 0, 0)),
            _resident((None, MLA_KV_RANK, hq), lambda i: (e, 0, 0)),
            _resident((None, hq, MLA_KV_RANK), lambda i: (e, 0, 0)),
            pl.BlockSpec((tm, LANES), row),
            pl.BlockSpec((tm, LANES), row),
        ],
        out_specs=[
            pl.BlockSpec((tm, n_z), row),
            pl.BlockSpec((tm, n_xbc), row),
            pl.BlockSpec((tm, LANES), row),
            pl.BlockSpec((tm, hq), row),
            pl.BlockSpec((tm, hq), row),
            pl.BlockSpec((hq, tm), lambda i: (0, i)),
        ],
        out_shape=[
            jax.ShapeDtypeStruct((t, n_z), BF),
            jax.ShapeDtypeStruct((t, n_xbc), F32),
            jax.ShapeDtypeStruct((t, LANES), F32),
            jax.ShapeDtypeStruct((t, hq), BF),
            jax.ShapeDtypeStruct((t, hq), BF),
            jax.ShapeDtypeStruct((hq, t), BF),
        ],
        compiler_params=_params("parallel"),
        name="mix_inproj",
    )(h, norm_w, w_all, q_norm, kv_norm, wqa, wqb, wkn, wvt, cos_t, sin_t)


def _split3_dot(a_bf, x):
    hi = x.astype(BF)
    r1 = x - hi.astype(F32)
    mid = r1.astype(BF)
    low = (r1 - mid.astype(F32)).astype(BF)
    return _dot(a_bf, hi) + _dot(a_bf, mid) + _dot(a_bf, low)


def _ssd_decay_stage(z, r0, dt_ref, bias_ref, a_ref, sel_ref, rows_ref, dec_ref, wall_ref):
    q = SSD_CHUNK
    nh = SSD_HEADS
    dt = jax.nn.softplus(dt_ref[pl.ds(r0, q), :] + bias_ref[...])
    la = dt * (a_ref[...] * math.log2(math.e))
    row = lax.broadcasted_iota(jnp.int32, (q, q), 0)
    col = lax.broadcasted_iota(jnp.int32, (q, q), 1)
    tril = jnp.where(row >= col, 1.0, 0.0).astype(BF)
    cum = _split3_dot(tril, la)
    tot = cum[q - 1:q, :]
    rev = tot - cum + la
    fwd_lane = lax.broadcasted_iota(jnp.int32, (q, LANES), 1) < nh
    sel = jnp.where(fwd_lane, cum, rev)
    sel_ref[z] = sel
    dt_t = dt.T
    rows_ref[z, 0:2 * nh, :] = (sel - jnp.log2(dt)).T[0:2 * nh]
    rows_ref[z, 2 * nh:3 * nh, :] = jnp.log2(dt_t[0:nh] + dt_t[nh:2 * nh])
    wall_ref[z] = (jnp.exp2(tot - sel) * dt).astype(BF)
    etot = jnp.exp2(tot)
    lo1 = lax.broadcasted_iota(jnp.int32, (1, LANES), 1) < SSD_HEAD_DIM
    for d in range(2):
        for p in range(nh // 2):
            h0 = d * nh + 2 * p
            dec = jnp.where(lo1, etot[:, h0:h0 + 1], etot[:, h0 + 1:h0 + 2])
            dec_ref[z, d, p] = jnp.broadcast_to(dec, dec_ref.shape[3:])


def _ssd_conv_stage(z, j, first, last, xc_ref, xp_ref, xn_ref, cw_ref, cb_ref, sm_ref, x_ref, b_ref, c_ref):
    q = SSD_CHUNK
    inner = SSD_HEADS * SSD_HEAD_DIM
    gn = SSD_GROUPS * SSD_STATE
    half = SSD_CONV // 2
    edge = xp_ref.shape[0]
    nsub = xc_ref.shape[0] // q
    r0 = pl.multiple_of(j * q, q)
    cur = xc_ref[pl.ds(r0, q), :]
    before = xc_ref[pl.ds(pl.multiple_of(jnp.maximum(r0 - edge, 0), edge), edge), :]
    after = xc_ref[pl.ds(pl.multiple_of(jnp.minimum(r0 + q, (nsub - 1) * q + q - edge), edge), edge), :]
    prev = jnp.where(first, 0.0, jnp.where(j == 0, xp_ref[...], before))
    nxt = jnp.where(last, 0.0, jnp.where(j == nsub - 1, xn_ref[...], after))
    fill = jnp.zeros((sm_ref.shape[1] - q - 2 * edge, cur.shape[1]), F32)
    window = jnp.concatenate([prev, cur, nxt, fill], axis=0).astype(BF)
    shifted = _dot(sm_ref[...], window)
    acc = cur * cw_ref[half:half + 1, :] + cb_ref[...]
    blk = 0
    for k in range(SSD_CONV):
        if k != half:
            acc = acc + shifted[blk * q:(blk + 1) * q] * cw_ref[k:k + 1, :]
            blk += 1
    xbc = jax.nn.silu(acc)
    x_ref[z] = xbc[:, :inner].astype(BF)
    for g in range(SSD_GROUPS):
        b_ref[z, g] = xbc[:, inner + g * SSD_STATE:inner + (g + 1) * SSD_STATE].T.astype(BF)
    c_ref[z] = xbc[:, inner + gn:].astype(BF)


def _ssd_state_stage(z, e_ref, st_ref, x_ref, b_ref, wall_ref, dec_ref, carry_ref):
    nh, n = SSD_HEADS, SSD_STATE
    inner = nh * SSD_HEAD_DIM
    gw = inner // SSD_GROUPS
    ppg = gw // LANES
    x = x_ref[z].astype(F32)
    wexp = _dot(wall_ref[z], e_ref[...])
    for d in range(2):
        xw = (x * wexp[:, d * inner:(d + 1) * inner]).astype(BF)
        for g in range(SSD_GROUPS):
            upd = _dot(b_ref[z, g], xw[:, g * gw:(g + 1) * gw])
            for i in range(ppg):
                p = g * ppg + i
                contrib = upd[:, i * LANES:(i + 1) * LANES]
                if d == 0:
                    state = carry_ref[0, p]
                    st_ref[z, 0, p] = state
                    carry_ref[0, p] = state * dec_ref[z, 0, p][0:1] + contrib
                else:
                    st_ref[z, 1, p] = contrib


def _ssd_output_phase(z, j, z_ref, d_ref, nw_ref, o_ref, st_ref, x_ref, b_ref, c_ref, sel_ref, rows_ref,
                      dec_ref, carry_ref):
    q = SSD_CHUNK
    r0 = pl.multiple_of(j * q, q)
    nh, n, hpg = SSD_HEADS, SSD_STATE, SSD_HEADS // SSD_GROUPS
    sel = sel_ref[z]
    rows = rows_ref[z]
    row = lax.broadcasted_iota(jnp.int32, (q, q), 0)
    col = lax.broadcasted_iota(jnp.int32, (q, q), 1)
    lower = row > col
    diag = row == col
    lo = lax.broadcasted_iota(jnp.int32, (q, LANES), 1) < SSD_HEAD_DIM
    ys = []
    for g in range(SSD_GROUPS):
        cg = c_ref[z, :, g * n:(g + 1) * n]
        cb = _dot(cg, b_ref[z, g])
        cg32 = cg.astype(F32)
        for j in range(hpg // 2):
            p = g * (hpg // 2) + j
            lhs = []
            for h in (2 * p, 2 * p + 1):
                hb = nh + h
                a_f = jnp.broadcast_to(sel[:, h:h + 1], (q, q))
                a_b = jnp.broadcast_to(sel[:, hb:hb + 1], (q, q))
                seg = jnp.where(lower, a_f - rows[h:h + 1, :],
                                jnp.where(diag, rows[2 * nh + h:2 * nh + h + 1, :], a_b - rows[hb:hb + 1, :]))
                m = (cb * jnp.exp2(seg)).astype(BF)
                cef = (cg32 * jnp.exp2(a_f)).astype(BF)
                ceb = (cg32 * jnp.exp2(a_b)).astype(BF)
                lhs.append(jnp.concatenate([m, cef, ceb], axis=1))
            xp = x_ref[z, :, p * LANES:(p + 1) * LANES]
            back = carry_ref[1, p]
            carry_ref[1, p] = back * dec_ref[z, 1, p][0:1] + st_ref[z, 1, p]
            rhs = jnp.concatenate([xp, st_ref[z, 0, p].astype(BF), back.astype(BF)], axis=0)
            out = _dot(jnp.concatenate(lhs, axis=0), rhs)
            ys.append(jnp.where(lo, out[:q], out[q:]) + xp.astype(F32) * d_ref[:, p * LANES:(p + 1) * LANES])
    y = jnp.concatenate(ys, axis=-1)
    gated = y * jax.nn.silu(z_ref[pl.ds(r0, q), :].astype(F32))
    o_ref[pl.ds(r0, q), :] = _rms(gated, nw_ref[...]).astype(BF)


def _ssd_body(xc_ref, xp_ref, xn_ref, dt_ref, z_ref, cw_ref, cb_ref, bias_ref, a_ref, d_ref, nw_ref, e_ref, sm_ref,
              o_ref, st_ref, x_ref, b_ref, c_ref, sel_ref, rows_ref, dec_ref, wall_ref, carry_ref):
    t = pl.program_id(1)
    nc = st_ref.shape[0]
    nsub = xc_ref.shape[0] // SSD_CHUNK
    nb = nc // nsub

    @pl.when(t == 0)
    def _():
        carry_ref[0] = jnp.zeros(carry_ref.shape[1:], F32)

    @pl.when(t == nb)
    def _():
        carry_ref[1] = jnp.zeros(carry_ref.shape[1:], F32)

    @pl.when(t < nb)
    def _():
        base = t * nsub
        for j in range(nsub):
            _ssd_decay_stage(base + j, j * SSD_CHUNK, dt_ref, bias_ref, a_ref, sel_ref, rows_ref, dec_ref, wall_ref)

        def conv(j):
            z = base + j
            _ssd_conv_stage(z, j, z == 0, z == nc - 1, xc_ref, xp_ref, xn_ref, cw_ref, cb_ref, sm_ref,
                            x_ref, b_ref, c_ref)

        def state(j):
            _ssd_state_stage(base + j, e_ref, st_ref, x_ref, b_ref, wall_ref, dec_ref, carry_ref)

        def sub(j, carry):
            conv(j)
            state(j - 1)
            return carry

        conv(0)
        lax.fori_loop(1, nsub, sub, 0)
        state(nsub - 1)

    @pl.when(t >= nb)
    def _():
        def sub(i, carry):
            j = nsub - 1 - i
            z = (2 * nb - 1 - t) * nsub + j
            _ssd_output_phase(z, j, z_ref, d_ref, nw_ref, o_ref, st_ref, x_ref, b_ref, c_ref, sel_ref,
                              rows_ref, dec_ref, carry_ref)
            return carry
        lax.fori_loop(0, nsub, sub, 0, unroll=2 if nsub % 2 == 0 else 1)


def _conv_shift_table(q, edge, rows):
    half = SSD_CONV // 2
    m = np.zeros(((SSD_CONV - 1) * q, rows), np.float32)
    blk = 0
    for k in range(SSD_CONV):
        if k != half:
            m[blk * q + np.arange(q), edge + np.arange(q) + k - half] = 1.0
            blk += 1
    return jnp.asarray(m, BF)


def _head_expand_table():
    inner = SSD_HEADS * SSD_HEAD_DIM
    e = np.zeros((LANES, 2 * inner), np.float32)
    for h in range(2 * SSD_HEADS):
        e[h, h * SSD_HEAD_DIM:(h + 1) * SSD_HEAD_DIM] = 1.0
    return jnp.asarray(e, BF)


def _ssd(xbc, dt_raw, z, conv_w, conv_b, dt_bias, a_neg, d_exp, ssd_norm, e, batch):
    t, c = xbc.shape
    s = t // batch
    inner = SSD_HEADS * SSD_HEAD_DIM
    gn = SSD_GROUPS * SSD_STATE
    q = SSD_CHUNK
    nc = s // q
    nsub = next(n for n in (8, 4, 2, 1) if nc % n == 0)
    nb = nc // nsub
    rows = nsub * q
    edge = 8
    epb = rows // edge
    blk1 = lambda i: jnp.minimum(i, nb - 1)
    blk3 = lambda i: nb - 1 - jnp.maximum(i - nb, 0)
    par = lambda b, i: (e, 0, 0)
    window_rows = 2 * q
    return pl.pallas_call(
        _ssd_body,
        grid=(batch, 2 * nb),
        in_specs=[
            pl.BlockSpec((rows, c), lambda b, i: (b * nb + blk1(i), 0)),
            pl.BlockSpec((edge, c), lambda b, i: (b * nb * epb + jnp.maximum(blk1(i) * epb - 1, 0), 0)),
            pl.BlockSpec((edge, c), lambda b, i: (b * nb * epb + jnp.minimum((blk1(i) + 1) * epb, nb * epb - 1), 0)),
            pl.BlockSpec((rows, LANES), lambda b, i: (b * nb + blk1(i), 0)),
            pl.BlockSpec((rows, inner), lambda b, i: (b * nb + blk3(i), 0)),
            pl.BlockSpec((None, SSD_CONV, c), par),
            pl.BlockSpec((None, 1, c), par),
            pl.BlockSpec((None, 1, LANES), par),
            pl.BlockSpec((None, 1, LANES), par),
            pl.BlockSpec((None, 1, inner), par),
            pl.BlockSpec((None, 1, inner), par),
            _resident((LANES, 2 * inner), lambda b, i: (0, 0)),
            _resident(((SSD_CONV - 1) * q, window_rows), lambda b, i: (0, 0)),
        ],
        out_specs=pl.BlockSpec((rows, inner), lambda b, i: (b * nb + blk3(i), 0)),
        out_shape=jax.ShapeDtypeStruct((t, inner), BF),
        scratch_shapes=[
            pltpu.VMEM((nc, 2, SSD_HEADS // 2, SSD_STATE, LANES), F32),
            pltpu.VMEM((nc, q, inner), BF),
            pltpu.VMEM((nc, SSD_GROUPS, SSD_STATE, q), BF),
            pltpu.VMEM((nc, q, gn), BF),
            pltpu.VMEM((nc, q, LANES), F32),
            pltpu.VMEM((nc, 3 * SSD_HEADS, q), F32),
            pltpu.VMEM((nc, 2, SSD_HEADS // 2, edge, LANES), F32),
            pltpu.VMEM((nc, q, LANES), BF),
            pltpu.VMEM((2, SSD_HEADS // 2, SSD_STATE, LANES), F32),
        ],
        compiler_params=_params("parallel", "arbitrary"),
        name="ssd",
    )(xbc, xbc, xbc, dt_raw, z, conv_w, conv_b, dt_bias, a_neg, d_exp, ssd_norm, _head_expand_table(),
      _conv_shift_table(q, edge, window_rows))


def _lane_fold(x, op):
    out = x[:, :LANES]
    for j in range(1, x.shape[-1] // LANES):
        out = op(out, x[:, j * LANES:(j + 1) * LANES])
    return out


def _mla_body(q_ref, k_ref, vt_ref, o_ref, s_ref, m_ref, *, kt):
    nk = k_ref.shape[0]
    rb = s_ref.shape[2]
    nrb = q_ref.shape[0] // rb
    nheads = q_ref.shape[1] // LANES

    def score_pass(r, head):
        rows = pl.ds(pl.multiple_of(r * rb, rb), rb)
        sl = slice(head * LANES, (head + 1) * LANES)
        slot = head % 2
        s = lax.dot_general(k_ref[:, sl], q_ref[rows, sl], NT_DIMS, preferred_element_type=F32)
        s_ref[slot] = s
        m_ref[slot] = jnp.max(s, axis=0, keepdims=True)

    def value_pass(head):
        hs = slice(head * LANES, (head + 1) * LANES)
        slot = head % 2
        m = m_ref[slot]
        o = jnp.zeros((LANES, rb), F32)
        for c in range(0, nk, kt):
            p = jnp.exp2(s_ref[slot, c:c + kt, :] - m)
            o = o + _dot(vt_ref[hs, c:c + kt], p.astype(BF))
        return o

    def row_block(r, carry):
        top = lax.broadcasted_iota(jnp.int32, (LANES, rb), 0) < MLA_V
        outs = []
        for head in range(nheads):
            if head + 1 < nheads:
                score_pass(r, head + 1)
            else:
                score_pass(jnp.minimum(r + 1, nrb - 1), 0)
            outs.append(value_pass(head))
        pairs = [jnp.where(top, o0 / o0[MLA_V:MLA_V + 1, :], o1 / o1[0:1, :]).T
                 for o0, o1 in zip(outs[0::2], outs[1::2])]
        o_ref[pl.ds(pl.multiple_of(r * rb, rb), rb), :] = jnp.concatenate(pairs, axis=-1).astype(BF)
        return carry

    score_pass(0, 0)
    lax.fori_loop(0, nrb, row_block, 0)


def _mla(q, k, vt, batch):
    t, hq = q.shape
    s = t // batch
    rb = _tile(s, 512)
    nv = MLA_HEADS * MLA_V
    q3, k3 = (a.reshape(batch, s, hq) for a in (q, k))
    blk = pl.BlockSpec((None, s, hq), lambda b: (b, 0, 0))
    o = pl.pallas_call(
        functools.partial(_mla_body, kt=_tile(s, 256)),
        grid=(batch,),
        in_specs=[blk, blk, pl.BlockSpec((hq, s), lambda b: (0, b))],
        out_specs=pl.BlockSpec((None, s, nv), lambda b: (b, 0, 0)),
        out_shape=jax.ShapeDtypeStruct((batch, s, nv), BF),
        scratch_shapes=[pltpu.VMEM((2, s, rb), F32), pltpu.VMEM((2, 1, rb), F32)],
        compiler_params=_params("parallel"),
        name="mla_attn",
    )(q3, k3, vt)
    return o.reshape(t, nv)


def _fnet_fold_body(ha_ref, hm_ref, hx_ref, nw_ref, cc_ref, sc_ref, ec_ref, es_ref):
    tm = ha_ref.shape[0]
    nw = nw_ref[...]
    u_a = _rms(ha_ref[...], nw)
    u_m = _rms(hm_ref[...], nw)
    u_x = _rms(hx_ref[...], nw)[0:1]
    r = lax.broadcasted_iota(jnp.int32, (tm, tm), 0)
    c = lax.broadcasted_iota(jnp.int32, (tm, tm), 1)
    perm = jnp.where(r + c == tm, 1.0, 0.0).astype(BF)
    hi = u_m.astype(BF)
    low = (u_m - hi.astype(F32)).astype(BF)
    mirror = _dot(perm, hi) + _dot(perm, low)
    first = lax.broadcasted_iota(jnp.int32, u_a.shape, 0) == 0
    mirror = jnp.where(first, u_x, mirror)
    ue = (u_a + mirror).astype(BF)
    uo = (u_a - mirror).astype(BF)
    gc = cc_ref.shape[0]
    for g in range(ue.shape[-1] // gc):
        sl = slice(g * gc, (g + 1) * gc)
        ec_ref[:, sl] = _dot(ue[:, sl], cc_ref[...]).astype(BF)
        es_ref[:, sl] = _dot(uo[:, sl], sc_ref[...]).astype(BF)


def _fnet_seq_body(h_ref, hh_ref, nw_ref, cc_ref, cs_ref, ss_ref, ec_ref, es_ref, w_ref, o_ref, *, scale):
    ts = h_ref.shape[0]
    u_h = _rms(hh_ref[...], nw_ref[...]).astype(BF)
    gc = cc_ref.shape[0]
    x_h = jnp.concatenate([_dot(u_h[:, g * gc:(g + 1) * gc], cc_ref[...])
                           for g in range(u_h.shape[-1] // gc)], axis=-1)[0:1]
    hr = ts // 2 if ts % 16 == 0 else ts
    odd = lax.broadcasted_iota(jnp.int32, (hr, 1), 0) % 2 == 1
    for r0 in range(0, ts, hr):
        rows = pl.ds(pl.multiple_of(pl.program_id(1) * ts + r0, hr), hr)
        y = _dot(cs_ref[rows, :], ec_ref[...]) - _dot(ss_ref[rows, :], es_ref[...])
        y = y + jnp.where(odd, -1.0, 1.0) * x_h
        o_ref[r0:r0 + hr, :] = h_ref[r0:r0 + hr, :] + _dot((y * scale).astype(BF), w_ref[...])


def _dft_tables(n, fold):
    j = np.arange(n)[:, None]
    k = np.arange(n // 2 if fold else n)[None, :]
    ang = ((j * k) % n) * (2.0 * np.pi / n)
    cos, sin = np.cos(ang), np.sin(ang)
    if fold:
        cos[:, 0] = 0.5
    return jnp.asarray(cos, BF), jnp.asarray(sin, BF)


def _fnet(h, norm_w, w_out, layer, o, batch, tables):
    t, d = h.shape
    s = t // batch
    cc, sc, cs, ss = tables
    gc = cc.shape[0]
    sub = 8
    tm = _tile(s // 2, 256)
    nt, nf = s // tm, s // 2 // tm
    nwspec = pl.BlockSpec((None, 1, d), lambda b, i: (layer, 0, 0))
    table = _resident((gc, gc), lambda b, i: (0, 0))
    ec, es = pl.pallas_call(
        _fnet_fold_body,
        grid=(batch, nf),
        in_specs=[
            pl.BlockSpec((tm, d), lambda b, i: (b * nt + i, 0)),
            pl.BlockSpec((tm, d), lambda b, i: (b * nt + nt - 1 - i, 0)),
            pl.BlockSpec((sub, d), lambda b, i: (b * (s // sub) + ((nt - i) % nt) * (tm // sub), 0)),
            nwspec, table, table,
        ],
        out_specs=[pl.BlockSpec((tm, d), lambda b, i: (b * nf + i, 0))] * 2,
        out_shape=[jax.ShapeDtypeStruct((t // 2, d), BF)] * 2,
        compiler_params=_params("parallel", "parallel"),
        name="fnet_fold_channel_dft",
    )(h, h, h, norm_w, cc, sc)
    ts = _tile(s, 512)
    ns = s // ts
    return pl.pallas_call(
        functools.partial(_fnet_seq_body, scale=(s * gc) ** -0.5),
        grid=(batch, ns),
        in_specs=[
            pl.BlockSpec((ts, d), lambda b, i: (b * ns + i, 0)),
            pl.BlockSpec((sub, d), lambda b, i: (b * (s // sub) + s // 2 // sub, 0)),
            nwspec, table,
            _resident((s, s // 2), lambda b, i: (0, 0)),
            _resident((s, s // 2), lambda b, i: (0, 0)),
            pl.BlockSpec((s // 2, d), lambda b, i: (b, 0)),
            pl.BlockSpec((s // 2, d), lambda b, i: (b, 0)),
            _resident((None, d, d), lambda b, i: (o, 0, 0)),
        ],
        out_specs=pl.BlockSpec((ts, d), lambda b, i: (b * ns + i, 0)),
        out_shape=jax.ShapeDtypeStruct((t, d), F32),
        compiler_params=_params("parallel", "arbitrary"),
        name="fnet_seq_dft",
    )(h, h, norm_w, cc, cs, ss, ec, es, w_out)


def _mixer_weights(w_in, w_uq, w_ukv, dt_bias, a_log, ssd_d):
    w_in, w_uq, w_ukv = w_in.astype(BF), w_uq.astype(BF), w_ukv.astype(BF)
    ne, d, _ = w_in.shape
    inner = SSD_HEADS * SSD_HEAD_DIM
    conv_ch = inner + 2 * SSD_GROUPS * SSD_STATE
    o_z, o_xbc = 0, inner
    o_dt = o_xbc + conv_ch
    o_cq = o_dt + 2 * SSD_HEADS
    o_ckv = o_cq + MLA_Q_RANK
    o_kr = o_ckv + MLA_KV_RANK
    half = MLA_ROPE // 2
    pad = LANES - MLA_NOPE - MLA_ROPE
    zeros = lambda *s: jnp.zeros(s, w_in.dtype)
    w_dt = jnp.concatenate([w_in[:, :, o_dt:o_cq], zeros(ne, d, LANES - 2 * SSD_HEADS)], axis=-1)
    kr1 = w_in[:, :, o_kr:o_kr + half]
    kr2 = w_in[:, :, o_kr + half:o_kr + MLA_ROPE]
    kr_a = jnp.concatenate([zeros(ne, d, MLA_NOPE), kr1, kr2, zeros(ne, d, pad)], axis=-1)
    kr_b = jnp.concatenate([zeros(ne, d, MLA_NOPE), -kr2, kr1, zeros(ne, d, pad)], axis=-1)
    pieces = [w_in[:, :, o_z:o_xbc], w_in[:, :, o_xbc:o_dt], w_dt, w_in[:, :, o_cq:o_ckv],
              w_in[:, :, o_ckv:o_kr], jnp.concatenate([kr_a, kr_b], axis=-1)]
    cols, c = [], 0
    for p in pieces:
        cols.append((c, c + p.shape[-1]))
        c += p.shape[-1]
    w_all = jnp.concatenate(pieces, axis=-1).astype(BF)

    uq = w_uq.reshape(ne, MLA_Q_RANK, MLA_HEADS, MLA_NOPE + MLA_ROPE)
    q_nope, q1, q2 = uq[..., :MLA_NOPE], uq[..., MLA_NOPE:MLA_NOPE + half], uq[..., MLA_NOPE + half:]
    zq = lambda n: jnp.zeros((ne, MLA_Q_RANK, MLA_HEADS, n), w_uq.dtype)
    hq = MLA_HEADS * LANES
    wqa = jnp.concatenate([q_nope, q1, q2, zq(pad)], axis=-1).reshape(ne, MLA_Q_RANK, hq).astype(BF)
    wqb = jnp.concatenate([zq(MLA_NOPE), -q2, q1, zq(pad)], axis=-1).reshape(ne, MLA_Q_RANK, hq).astype(BF)

    ukv = w_ukv.reshape(ne, MLA_KV_RANK, MLA_HEADS, MLA_NOPE + MLA_V)
    zkv = lambda *s: jnp.zeros((ne, MLA_KV_RANK) + s, w_ukv.dtype)
    wkn = jnp.concatenate([ukv[..., :MLA_NOPE], zkv(MLA_HEADS, LANES - MLA_NOPE)], axis=-1)
    wkn = wkn.reshape(ne, MLA_KV_RANK, hq).astype(BF)
    vv = ukv[..., MLA_NOPE:].reshape(ne, MLA_KV_RANK, MLA_HEADS // 2, 2, MLA_V)
    zv = zkv(MLA_HEADS // 2, MLA_V)
    wv = jnp.stack([jnp.concatenate([vv[:, :, :, 0], zv], axis=-1),
                    jnp.concatenate([zv, vv[:, :, :, 1]], axis=-1)], axis=3)
    wvt = jnp.swapaxes(wv.reshape(ne, MLA_KV_RANK, hq), 1, 2).astype(BF)

    padl = lambda a: jnp.concatenate([a, jnp.zeros((ne, LANES - a.shape[-1]), a.dtype)], axis=-1)[:, None, :]
    bias = padl(dt_bias.reshape(ne, 2 * SSD_HEADS))
    a_neg = padl(-jnp.exp(a_log.reshape(ne, 2 * SSD_HEADS)))
    d_exp = jnp.repeat(ssd_d, SSD_HEAD_DIM, axis=-1)[:, None, :]
    return w_all, tuple(cols), wqa, wqb, wkn, wvt, bias, a_neg, d_exp


def _rope_tables(positions):
    inv = 1.0 / (ROPE_THETA ** (jnp.arange(0, MLA_ROPE, 2, dtype=F32) / MLA_ROPE))
    ang = inv[:, None] * positions.astype(F32).reshape(1, -1)
    cos, sin = jnp.cos(ang), jnp.sin(ang)
    t = cos.shape[1]
    pad = LANES - MLA_NOPE - MLA_ROPE
    cos_t = jnp.concatenate([jnp.ones((MLA_NOPE, t), F32), cos, cos, jnp.zeros((pad, t), F32)], axis=0).T
    sin_t = jnp.concatenate([jnp.zeros((MLA_NOPE, t), F32), sin, sin, jnp.zeros((pad, t), F32)], axis=0).T
    return cos_t, sin_t


def kernel(x, mem, positions, mem_norm, final_norm, ffn1_norm, ffn1_w_gu, ffn1_w_down, mix_norm, xa_norm,
           xa_wq, xa_wkv, xa_wo, ffn2_norm, ffn2_w_gu, ffn2_w_down, w_in, conv_w, conv_b, dt_bias, a_log,
           ssd_d, ssd_norm, q_norm, w_uq, kv_norm, w_ukv, w_out, fnet_w_out):
    batch, seq, d = x.shape
    depth = ffn1_norm.shape[0]
    t = batch * seq
    bf = lambda a: a.astype(BF)
    row3 = lambda a: a[:, None, :]

    kv = _kvproj(mem.reshape(-1, d), mem_norm[None, :], bf(xa_wkv)).reshape(depth, batch, mem.shape[1], 2 * d)
    w_all, cols, wqa, wqb, wkn, wvt, bias, a_neg, d_exp = _mixer_weights(w_in, w_uq, w_ukv, dt_bias, a_log, ssd_d)
    cos_t, sin_t = _rope_tables(positions)
    gc = d // FNET_GROUPS
    tables = _dft_tables(gc, fold=False) + _dft_tables(seq, fold=True)
    f1n, f2n, mxn, xan = row3(ffn1_norm), row3(ffn2_norm), row3(mix_norm), row3(xa_norm)
    gu, dn = _cast_layer(ffn1_w_gu, 0), _cast_layer(ffn1_w_down, 0)
    wq, wo, w_mix_out, w_fnet = bf(xa_wq), bf(xa_wo), bf(w_out), bf(fnet_w_out)
    ssd_nw, qn, kvn = row3(ssd_norm), row3(q_norm), row3(kv_norm)

    h = x.reshape(t, d)
    for layer in range(depth):
        h, gu, dn = _ffn(h, f1n, gu, dn, layer, cast_next=(ffn2_w_gu, ffn2_w_down, layer))
        if layer % 2 == 0:
            e = layer // 2
            z, xbc, dt_raw, q, k, vt = _inproj(h, mxn, w_all, cols, qn, kvn, wqa, wqb, wkn, wvt,
                                              cos_t, sin_t, layer, e)
            y_ssd = _ssd(xbc, dt_raw, z, conv_w, row3(conv_b), bias, a_neg, d_exp, ssd_nw, e, batch)
            o_mla = _mla(q, k, vt, batch)
            mix = (y_ssd, o_mla, w_mix_out, e)
        else:
            h = _fnet(h, mxn, w_fnet, layer, layer // 2, batch, tables)
            mix = None
        h = _xattn(h, xan, wq, kv, wo, layer, batch, mix=mix)
        if layer + 1 < depth:
            h, gu, dn = _ffn(h, f2n, gu, dn, layer, cast_next=(ffn1_w_gu, ffn1_w_down, layer + 1))
        else:
            h = _ffn(h, f2n, gu, dn, layer, final_w=final_norm[None, :])
    return h.reshape(batch, seq, d)
```

```python
import functools
import math

import numpy as np
import jax
import jax.numpy as jnp
from jax import lax
from jax.experimental import pallas as pl
from jax.experimental.pallas import tpu as pltpu

EPS = 1e-6
BF = jnp.bfloat16
F32 = jnp.float32

V7X_VMEM_BYTES = 64 * 1024 * 1024
VMEM_LIMIT = V7X_VMEM_BYTES - 8 * 1024 * 1024
LANES = 128

SSD_HEADS = 16
SSD_HEAD_DIM = 64
SSD_GROUPS = 2
SSD_STATE = 128
SSD_CONV = 5
SSD_CHUNK = 128
MLA_HEADS = 8
MLA_Q_RANK = 512
MLA_KV_RANK = 256
MLA_NOPE = 64
MLA_ROPE = 32
MLA_V = 64
ROPE_THETA = 10000.0
FNET_GROUPS = 4
XA_HEADS = 4

NT_DIMS = (((1,), (1,)), ((), ()))
TN_DIMS = (((0,), (0,)), ((), ()))


def _params(*sem):
    return pltpu.CompilerParams(dimension_semantics=sem, vmem_limit_bytes=VMEM_LIMIT)


def _resident(shape, index_map):
    return pl.BlockSpec(shape, index_map, pipeline_mode=pl.Buffered(1))


def _rms(x, w):
    return x * lax.rsqrt(jnp.mean(x * x, axis=-1, keepdims=True) + EPS) * w


def _dot(a, b):
    return jnp.dot(a, b, preferred_element_type=F32)


def _tile(n, pref):
    t = min(n, pref)
    assert n % t == 0, (n, t)
    return t


def _ffn_body(*refs, chunks, final, cast):
    refs = list(refs)
    h_ref, nw_ref, wg_ref, wu_ref, wd_ref = refs[:5]
    del refs[:5]
    fw_ref = refs.pop(0) if final else None
    if cast:
        gu_src, dn_src = refs.pop(0), refs.pop(0)
        o_ref, gu_dst, dn_dst = refs
        gu_dst[...] = gu_src[...].astype(BF)
        dn_dst[...] = dn_src[...].astype(BF)
    else:
        (o_ref,) = refs
    h = h_ref[...]
    xn = _rms(h, nw_ref[...]).astype(BF)
    acc = jnp.zeros(h.shape, F32)
    for a, b in chunks:
        g = _dot(xn, wg_ref[:, a:b])
        u = _dot(xn, wu_ref[:, a:b])
        act = (jax.nn.silu(g) * u).astype(BF)
        acc = acc + _dot(act, wd_ref[a:b, :])
    out = h + 0.5 * acc
    if final:
        out = _rms(out, fw_ref[...])
    o_ref[...] = out


def _ffn(h, norm_w, w_gu, w_down, layer, final_w=None, cast_next=None):
    t, d = h.shape
    f = w_down.shape[0]
    tm = _tile(t, 1024)
    steps = t // tm
    step = 768
    chunks = tuple((a, min(a + step, f)) for a in range(0, f, step))
    row = lambda i: (i, 0)
    in_specs = [
        pl.BlockSpec((tm, d), row),
        pl.BlockSpec((None, 1, d), lambda i: (layer, 0, 0)),
        _resident((d, f), lambda i: (0, 0)),
        _resident((d, f), lambda i: (0, 1)),
        _resident((f, d), lambda i: (0, 0)),
    ]
    args = [h, norm_w, w_gu, w_gu, w_down]
    out_specs = [pl.BlockSpec((tm, d), row)]
    out_shape = [jax.ShapeDtypeStruct((t, d), F32)]
    if final_w is not None:
        in_specs.append(pl.BlockSpec((1, d), lambda i: (0, 0)))
        args.append(final_w)
    if cast_next is not None:
        src_gu, src_dn, nxt = cast_next
        nblk = next(n for n in (16, 8, 4, 2, 1) if steps % n == 0)
        rep = steps // nblk
        assert d % (16 * nblk) == 0 and f % (16 * nblk) == 0
        in_specs += [pl.BlockSpec((None, d // nblk, 2 * f), lambda i: (nxt, i // rep, 0)),
                     pl.BlockSpec((None, f // nblk, d), lambda i: (nxt, i // rep, 0))]
        args += [src_gu, src_dn]
        out_specs += [pl.BlockSpec((d // nblk, 2 * f), lambda i: (i // rep, 0)),
                      pl.BlockSpec((f // nblk, d), lambda i: (i // rep, 0))]
        out_shape += [jax.ShapeDtypeStruct((d, 2 * f), BF), jax.ShapeDtypeStruct((f, d), BF)]
    outs = pl.pallas_call(
        functools.partial(_ffn_body, chunks=chunks, final=final_w is not None, cast=cast_next is not None),
        grid=(steps,),
        in_specs=in_specs,
        out_specs=out_specs,
        out_shape=out_shape,
        compiler_params=_params("arbitrary" if cast_next is not None else "parallel"),
        name="ffn",
    )(*args)
    return outs if cast_next is not None else outs[0]


def _cast_body(src_ref, dst_ref):
    dst_ref[...] = src_ref[...].astype(BF)


def _cast_layer(w, layer):
    _, r, c = w.shape
    nblk = next(n for n in (8, 4, 2, 1) if r % (16 * n) == 0)
    return pl.pallas_call(
        _cast_body,
        grid=(nblk,),
        in_specs=[pl.BlockSpec((None, r // nblk, c), lambda i: (layer, i, 0))],
        out_specs=pl.BlockSpec((r // nblk, c), lambda i: (i, 0)),
        out_shape=jax.ShapeDtypeStruct((r, c), BF),
        compiler_params=_params("parallel"),
        name="cast_weights",
    )(w)


def _kvproj_body(m_ref, nw_ref, w_ref, o_ref):
    mn = _rms(m_ref[...], nw_ref[...]).astype(BF)
    o_ref[...] = _dot(mn, w_ref[...]).astype(BF)


def _kvproj(mem2d, mem_norm, wkv):
    n, d = mem2d.shape
    nl, _, d2 = wkv.shape
    tm = _tile(n, 512)
    return pl.pallas_call(
        _kvproj_body,
        grid=(nl, n // tm),
        in_specs=[
            pl.BlockSpec((tm, d), lambda l, i: (i, 0)),
            pl.BlockSpec((1, d), lambda l, i: (0, 0)),
            pl.BlockSpec((None, d, d2), lambda l, i: (l, 0, 0)),
        ],
        out_specs=pl.BlockSpec((None, tm, d2), lambda l, i: (l, i, 0)),
        out_shape=jax.ShapeDtypeStruct((nl, n, d2), BF),
        compiler_params=_params("parallel", "parallel"),
        name="xa_kvproj",
    )(mem2d, mem_norm, wkv)


def _xa_body(*refs, heads, mixed):
    if mixed:
        h_ref, nw_ref, wq_ref, k_ref, v_ref, wo_ref, y_ref, a_ref, wy_ref, wa_ref, o_ref = refs
        h = h_ref[...] + _dot(y_ref[...], wy_ref[...]) + _dot(a_ref[...], wa_ref[...])
    else:
        h_ref, nw_ref, wq_ref, k_ref, v_ref, wo_ref, o_ref = refs
        h = h_ref[...]
    hn = _rms(h, nw_ref[...]).astype(BF)
    dh = h.shape[-1] // heads
    q = (_dot(hn, wq_ref[...]) * (math.log2(math.e) * dh ** -0.5)).astype(BF)
    outs = []
    for i in range(heads):
        sl = slice(i * dh, (i + 1) * dh)
        s = lax.dot_general(q[:, sl], k_ref[:, sl], NT_DIMS, preferred_element_type=F32)
        p = jnp.exp2(s - jnp.max(s, axis=-1, keepdims=True))
        l = jnp.sum(p, axis=-1, keepdims=True)
        outs.append((_dot(p.astype(BF), v_ref[:, sl]) / l).astype(BF))
    o = jnp.concatenate(outs, axis=-1)
    o_ref[...] = h + _dot(o, wo_ref[...])


def _xattn(h, norm_w, wq, kv, wo, layer, batch, mix=None):
    t, d = h.shape
    s = t // batch
    nm = kv.shape[2]
    tm = _tile(s, 1024)
    ns = s // tm
    row = lambda b, i: (b * ns + i, 0)
    in_specs = [
        pl.BlockSpec((tm, d), row),
        pl.BlockSpec((None, 1, d), lambda b, i: (layer, 0, 0)),
        _resident((None, d, d), lambda b, i: (layer, 0, 0)),
        pl.BlockSpec((None, None, nm, d), lambda b, i: (layer, b, 0, 0)),
        pl.BlockSpec((None, None, nm, d), lambda b, i: (layer, b, 0, 1)),
        _resident((None, d, d), lambda b, i: (layer, 0, 0)),
    ]
    args = [h, norm_w, wq, kv, kv, wo]
    if mix is not None:
        y_ssd, o_mla, w_out, e = mix
        ny, na = y_ssd.shape[1], o_mla.shape[1]
        assert ny % na == 0
        in_specs += [
            pl.BlockSpec((tm, ny), row),
            pl.BlockSpec((tm, na), row),
            _resident((None, ny, d), lambda b, i: (e, 0, 0)),
            _resident((None, na, d), lambda b, i: (e, ny // na, 0)),
        ]
        args += [y_ssd, o_mla, w_out, w_out]
    return pl.pallas_call(
        functools.partial(_xa_body, heads=XA_HEADS, mixed=mix is not None),
        grid=(batch, ns),
        in_specs=in_specs,
        out_specs=pl.BlockSpec((tm, d), row),
        out_shape=jax.ShapeDtypeStruct((t, d), F32),
        compiler_params=_params("parallel", "parallel"),
        name="xattn",
    )(*args)


def _inproj_body(h_ref, nw_ref, w_ref, qn_ref, kvn_ref, wqa_ref, wqb_ref, wkn_ref, wvt_ref,
                 cos_ref, sin_ref, z_ref, xbc_ref, dt_ref, q_ref, k_ref, vt_ref, *, cols, scale):
    c_z, c_xbc, c_dt, c_cq, c_ckv, c_kr = cols
    u = _rms(h_ref[...], nw_ref[...]).astype(BF)
    z_ref[...] = _dot(u, w_ref[:, c_z[0]:c_z[1]]).astype(BF)
    xbc_ref[...] = _dot(u, w_ref[:, c_xbc[0]:c_xbc[1]])
    dt_ref[...] = _dot(u, w_ref[:, c_dt[0]:c_dt[1]])
    cqn = _rms(_dot(u, w_ref[:, c_cq[0]:c_cq[1]]), qn_ref[...]).astype(BF)
    ckvn = _rms(_dot(u, w_ref[:, c_ckv[0]:c_ckv[1]]), kvn_ref[...]).astype(BF)
    kr = _dot(u, w_ref[:, c_kr[0]:c_kr[1]])
    cos_t = cos_ref[...]
    sin_t = sin_ref[...]
    kp = kr[:, :LANES] * cos_t + kr[:, LANES:] * sin_t
    qa = _dot(cqn, wqa_ref[...])
    qb = _dot(cqn, wqb_ref[...])
    kn = _dot(ckvn, wkn_ref[...])
    for i in range(qa.shape[-1] // LANES):
        sl = slice(i * LANES, (i + 1) * LANES)
        q_ref[:, sl] = ((qa[:, sl] * cos_t + qb[:, sl] * sin_t) * scale).astype(BF)
        k_ref[:, sl] = (kn[:, sl] + kp).astype(BF)
    vrow = lax.broadcasted_iota(jnp.int32, (vt_ref.shape[0], 1), 0) % (2 * LANES)
    ones = jnp.where((vrow == MLA_V) | (vrow == LANES), 1.0, 0.0)
    vt = lax.dot_general(wvt_ref[...], ckvn, NT_DIMS, preferred_element_type=F32)
    vt_ref[...] = (vt + ones).astype(BF)


def _inproj(h, norm_w, w_all, cols, q_norm, kv_norm, wqa, wqb, wkn, wvt, cos_t, sin_t, layer, e):
    t, d = h.shape
    tm = _tile(t, 1024)
    wc = w_all.shape[-1]
    n_z = cols[0][1] - cols[0][0]
    n_xbc = cols[1][1] - cols[1][0]
    hq = wqa.shape[-1]
    row = lambda i: (i, 0)
    return pl.pallas_call(
        functools.partial(_inproj_body, cols=cols, scale=math.log2(math.e) * (MLA_NOPE + MLA_ROPE) ** -0.5),
        grid=(t // tm,),
        in_specs=[
            pl.BlockSpec((tm, d), row),
            pl.BlockSpec((None, 1, d), lambda i: (layer, 0, 0)),
            _resident((None, d, wc), lambda i: (e, 0, 0)),
            pl.BlockSpec((None, 1, MLA_Q_RANK), lambda i: (e, 0, 0)),
            pl.BlockSpec((None, 1, MLA_KV_RANK), lambda i: (e, 0, 0)),
            _resident((None, MLA_Q_RANK, hq), lambda i: (e, 0, 0)),
            _resident((None, MLA_Q_RANK, hq), lambda i: (e, 0, 0)),
            _resident((None, MLA_KV_RANK, hq), lambda i: (e, 0, 0)),
            _resident((None, hq, MLA_KV_RANK), lambda i: (e, 0, 0)),
            pl.BlockSpec((tm, LANES), row),
            pl.BlockSpec((tm, LANES), row),
        ],
        out_specs=[
            pl.BlockSpec((tm, n_z), row),
            pl.BlockSpec((tm, n_xbc), row),
            pl.BlockSpec((tm, LANES), row),
            pl.BlockSpec((tm, hq), row),
            pl.BlockSpec((tm, hq), row),
            pl.BlockSpec((hq, tm), lambda i: (0, i)),
        ],
        out_shape=[
            jax.ShapeDtypeStruct((t, n_z), BF),
            jax.ShapeDtypeStruct((t, n_xbc), F32),
            jax.ShapeDtypeStruct((t, LANES), F32),
            jax.ShapeDtypeStruct((t, hq), BF),
            jax.ShapeDtypeStruct((t, hq), BF),
            jax.ShapeDtypeStruct((hq, t), BF),
        ],
        compiler_params=_params("parallel"),
        name="mix_inproj",
    )(h, norm_w, w_all, q_norm, kv_norm, wqa, wqb, wkn, wvt, cos_t, sin_t)


def _split3_dot(a_bf, x):
    hi = x.astype(BF)
    r1 = x - hi.astype(F32)
    mid = r1.astype(BF)
    low = (r1 - mid.astype(F32)).astype(BF)
    return _dot(a_bf, hi) + _dot(a_bf, mid) + _dot(a_bf, low)


def _ssd_decay_stage(z, r0, dt_ref, bias_ref, a_ref, sel_ref, rows_ref, dec_ref, wall_ref):
    q = SSD_CHUNK
    nh = SSD_HEADS
    dt = jax.nn.softplus(dt_ref[pl.ds(r0, q), :] + bias_ref[...])
    la = dt * (a_ref[...] * math.log2(math.e))
    row = lax.broadcasted_iota(jnp.int32, (q, q), 0)
    col = lax.broadcasted_iota(jnp.int32, (q, q), 1)
    tril = jnp.where(row >= col, 1.0, 0.0).astype(BF)
    cum = _split3_dot(tril, la)
    tot = cum[q - 1:q, :]
    rev = tot - cum + la
    fwd_lane = lax.broadcasted_iota(jnp.int32, (q, LANES), 1) < nh
    sel = jnp.where(fwd_lane, cum, rev)
    sel_ref[z] = sel
    dt_t = dt.T
    rows_ref[z, 0:2 * nh, :] = (sel - jnp.log2(dt)).T[0:2 * nh]
    rows_ref[z, 2 * nh:3 * nh, :] = jnp.log2(dt_t[0:nh] + dt_t[nh:2 * nh])
    wall_ref[z] = (jnp.exp2(tot - sel) * dt).astype(BF)
    etot = jnp.exp2(tot)
    lo1 = lax.broadcasted_iota(jnp.int32, (1, LANES), 1) < SSD_HEAD_DIM
    for d in range(2):
        for p in range(nh // 2):
            h0 = d * nh + 2 * p
            dec = jnp.where(lo1, etot[:, h0:h0 + 1], etot[:, h0 + 1:h0 + 2])
            dec_ref[z, d, p] = jnp.broadcast_to(dec, dec_ref.shape[3:])


def _ssd_conv_stage(z, j, first, last, xc_ref, xp_ref, xn_ref, cw_ref, cb_ref, sm_ref, x_ref, b_ref, c_ref):
    q = SSD_CHUNK
    inner = SSD_HEADS * SSD_HEAD_DIM
    gn = SSD_GROUPS * SSD_STATE
    half = SSD_CONV // 2
    edge = xp_ref.shape[0]
    nsub = xc_ref.shape[0] // q
    r0 = pl.multiple_of(j * q, q)
    cur = xc_ref[pl.ds(r0, q), :]
    before = xc_ref[pl.ds(pl.multiple_of(jnp.maximum(r0 - edge, 0), edge), edge), :]
    after = xc_ref[pl.ds(pl.multiple_of(jnp.minimum(r0 + q, (nsub - 1) * q + q - edge), edge), edge), :]
    prev = jnp.where(first, 0.0, jnp.where(j == 0, xp_ref[...], before))
    nxt = jnp.where(last, 0.0, jnp.where(j == nsub - 1, xn_ref[...], after))
    fill = jnp.zeros((sm_ref.shape[1] - q - 2 * edge, cur.shape[1]), F32)
    window = jnp.concatenate([prev, cur, nxt, fill], axis=0).astype(BF)
    shifted = _dot(sm_ref[...], window)
    acc = cur * cw_ref[half:half + 1, :] + cb_ref[...]
    blk = 0
    for k in range(SSD_CONV):
        if k != half:
            acc = acc + shifted[blk * q:(blk + 1) * q] * cw_ref[k:k + 1, :]
            blk += 1
    xbc = jax.nn.silu(acc)
    x_ref[z] = xbc[:, :inner].astype(BF)
    for g in range(SSD_GROUPS):
        b_ref[z, g] = xbc[:, inner + g * SSD_STATE:inner + (g + 1) * SSD_STATE].T.astype(BF)
    c_ref[z] = xbc[:, inner + gn:].astype(BF)


def _ssd_state_stage(z, e_ref, st_ref, x_ref, b_ref, wall_ref, dec_ref, carry_ref):
    nh, n = SSD_HEADS, SSD_STATE
    inner = nh * SSD_HEAD_DIM
    gw = inner // SSD_GROUPS
    ppg = gw // LANES
    x = x_ref[z].astype(F32)
    wexp = _dot(wall_ref[z], e_ref[...])
    for d in range(2):
        xw = (x * wexp[:, d * inner:(d + 1) * inner]).astype(BF)
        for g in range(SSD_GROUPS):
            upd = _dot(b_ref[z, g], xw[:, g * gw:(g + 1) * gw])
            for i in range(ppg):
                p = g * ppg + i
                contrib = upd[:, i * LANES:(i + 1) * LANES]
                if d == 0:
                    state = carry_ref[0, p]
                    st_ref[z, 0, p] = state
                    carry_ref[0, p] = state * dec_ref[z, 0, p][0:1] + contrib
                else:
                    st_ref[z, 1, p] = contrib


def _ssd_output_phase(z, j, z_ref, d_ref, nw_ref, o_ref, st_ref, x_ref, b_ref, c_ref, sel_ref, rows_ref,
                      dec_ref, carry_ref):
    q = SSD_CHUNK
    r0 = pl.multiple_of(j * q, q)
    nh, n, hpg = SSD_HEADS, SSD_STATE, SSD_HEADS // SSD_GROUPS
    sel = sel_ref[z]
    rows = rows_ref[z]
    row = lax.broadcasted_iota(jnp.int32, (q, q), 0)
    col = lax.broadcasted_iota(jnp.int32, (q, q), 1)
    lower = row > col
    diag = row == col
    lo = lax.broadcasted_iota(jnp.int32, (q, LANES), 1) < SSD_HEAD_DIM
    ys = []
    for g in range(SSD_GROUPS):
        cg = c_ref[z, :, g * n:(g + 1) * n]
        cb = _dot(cg, b_ref[z, g])
        cg32 = cg.astype(F32)
        for j in range(hpg // 2):
            p = g * (hpg // 2) + j
            lhs = []
            for h in (2 * p, 2 * p + 1):
                hb = nh + h
                a_f = jnp.broadcast_to(sel[:, h:h + 1], (q, q))
                a_b = jnp.broadcast_to(sel[:, hb:hb + 1], (q, q))
                seg = jnp.where(lower, a_f - rows[h:h + 1, :],
                                jnp.where(diag, rows[2 * nh + h:2 * nh + h + 1, :], a_b - rows[hb:hb + 1, :]))
                m = (cb * jnp.exp2(seg)).astype(BF)
                cef = (cg32 * jnp.exp2(a_f)).astype(BF)
                ceb = (cg32 * jnp.exp2(a_b)).astype(BF)
                lhs.append(jnp.concatenate([m, cef, ceb], axis=1))
            xp = x_ref[z, :, p * LANES:(p + 1) * LANES]
            back = carry_ref[1, p]
            carry_ref[1, p] = back * dec_ref[z, 1, p][0:1] + st_ref[z, 1, p]
            rhs = jnp.concatenate([xp, st_ref[z, 0, p].astype(BF), back.astype(BF)], axis=0)
            out = _dot(jnp.concatenate(lhs, axis=0), rhs)
            ys.append(jnp.where(lo, out[:q], out[q:]) + xp.astype(F32) * d_ref[:, p * LANES:(p + 1) * LANES])
    y = jnp.concatenate(ys, axis=-1)
    gated = y * jax.nn.silu(z_ref[pl.ds(r0, q), :].astype(F32))
    o_ref[pl.ds(r0, q), :] = _rms(gated, nw_ref[...]).astype(BF)


def _ssd_body(xc_ref, xp_ref, xn_ref, dt_ref, z_ref, cw_ref, cb_ref, bias_ref, a_ref, d_ref, nw_ref, e_ref, sm_ref,
              o_ref, st_ref, x_ref, b_ref, c_ref, sel_ref, rows_ref, dec_ref, wall_ref, carry_ref):
    t = pl.program_id(1)
    nc = st_ref.shape[0]
    nsub = xc_ref.shape[0] // SSD_CHUNK
    nb = nc // nsub

    @pl.when(t == 0)
    def _():
        carry_ref[0] = jnp.zeros(carry_ref.shape[1:], F32)

    @pl.when(t == nb)
    def _():
        carry_ref[1] = jnp.zeros(carry_ref.shape[1:], F32)

    @pl.when(t < nb)
    def _():
        base = t * nsub
        for j in range(nsub):
            _ssd_decay_stage(base + j, j * SSD_CHUNK, dt_ref, bias_ref, a_ref, sel_ref, rows_ref, dec_ref, wall_ref)

        def conv(j):
            z = base + j
            _ssd_conv_stage(z, j, z == 0, z == nc - 1, xc_ref, xp_ref, xn_ref, cw_ref, cb_ref, sm_ref,
                            x_ref, b_ref, c_ref)

        def state(j):
            _ssd_state_stage(base + j, e_ref, st_ref, x_ref, b_ref, wall_ref, dec_ref, carry_ref)

        def sub(j, carry):
            conv(j)
            state(j - 1)
            return carry

        conv(0)
        lax.fori_loop(1, nsub, sub, 0, unroll=2)
        state(nsub - 1)

    @pl.when(t >= nb)
    def _():
        def sub(i, carry):
            j = nsub - 1 - i
            z = (2 * nb - 1 - t) * nsub + j
            _ssd_output_phase(z, j, z_ref, d_ref, nw_ref, o_ref, st_ref, x_ref, b_ref, c_ref, sel_ref,
                              rows_ref, dec_ref, carry_ref)
            return carry
        lax.fori_loop(0, nsub, sub, 0, unroll=2 if nsub % 2 == 0 else 1)


def _conv_shift_table(q, edge, rows):
    half = SSD_CONV // 2
    m = np.zeros(((SSD_CONV - 1) * q, rows), np.float32)
    blk = 0
    for k in range(SSD_CONV):
        if k != half:
            m[blk * q + np.arange(q), edge + np.arange(q) + k - half] = 1.0
            blk += 1
    return jnp.asarray(m, BF)


def _head_expand_table():
    inner = SSD_HEADS * SSD_HEAD_DIM
    e = np.zeros((LANES, 2 * inner), np.float32)
    for h in range(2 * SSD_HEADS):
        e[h, h * SSD_HEAD_DIM:(h + 1) * SSD_HEAD_DIM] = 1.0
    return jnp.asarray(e, BF)


def _ssd(xbc, dt_raw, z, conv_w, conv_b, dt_bias, a_neg, d_exp, ssd_norm, e, batch):
    t, c = xbc.shape
    s = t // batch
    inner = SSD_HEADS * SSD_HEAD_DIM
    gn = SSD_GROUPS * SSD_STATE
    q = SSD_CHUNK
    nc = s // q
    nsub = next(n for n in (8, 4, 2, 1) if nc % n == 0)
    nb = nc // nsub
    rows = nsub * q
    edge = 8
    epb = rows // edge
    blk1 = lambda i: jnp.minimum(i, nb - 1)
    blk3 = lambda i: nb - 1 - jnp.maximum(i - nb, 0)
    par = lambda b, i: (e, 0, 0)
    window_rows = 2 * q
    return pl.pallas_call(
        _ssd_body,
        grid=(batch, 2 * nb),
        in_specs=[
            pl.BlockSpec((rows, c), lambda b, i: (b * nb + blk1(i), 0)),
            pl.BlockSpec((edge, c), lambda b, i: (b * nb * epb + jnp.maximum(blk1(i) * epb - 1, 0), 0)),
            pl.BlockSpec((edge, c), lambda b, i: (b * nb * epb + jnp.minimum((blk1(i) + 1) * epb, nb * epb - 1), 0)),
            pl.BlockSpec((rows, LANES), lambda b, i: (b * nb + blk1(i), 0)),
            pl.BlockSpec((rows, inner), lambda b, i: (b * nb + blk3(i), 0)),
            pl.BlockSpec((None, SSD_CONV, c), par),
            pl.BlockSpec((None, 1, c), par),
            pl.BlockSpec((None, 1, LANES), par),
            pl.BlockSpec((None, 1, LANES), par),
            pl.BlockSpec((None, 1, inner), par),
            pl.BlockSpec((None, 1, inner), par),
            _resident((LANES, 2 * inner), lambda b, i: (0, 0)),
            _resident(((SSD_CONV - 1) * q, window_rows), lambda b, i: (0, 0)),
        ],
        out_specs=pl.BlockSpec((rows, inner), lambda b, i: (b * nb + blk3(i), 0)),
        out_shape=jax.ShapeDtypeStruct((t, inner), BF),
        scratch_shapes=[
            pltpu.VMEM((nc, 2, SSD_HEADS // 2, SSD_STATE, LANES), F32),
            pltpu.VMEM((nc, q, inner), BF),
            pltpu.VMEM((nc, SSD_GROUPS, SSD_STATE, q), BF),
            pltpu.VMEM((nc, q, gn), BF),
            pltpu.VMEM((nc, q, LANES), F32),
            pltpu.VMEM((nc, 3 * SSD_HEADS, q), F32),
            pltpu.VMEM((nc, 2, SSD_HEADS // 2, edge, LANES), F32),
            pltpu.VMEM((nc, q, LANES), BF),
            pltpu.VMEM((2, SSD_HEADS // 2, SSD_STATE, LANES), F32),
        ],
        compiler_params=_params("parallel", "arbitrary"),
        name="ssd",
    )(xbc, xbc, xbc, dt_raw, z, conv_w, conv_b, dt_bias, a_neg, d_exp, ssd_norm, _head_expand_table(),
      _conv_shift_table(q, edge, window_rows))


def _lane_fold(x, op):
    out = x[:, :LANES]
    for j in range(1, x.shape[-1] // LANES):
        out = op(out, x[:, j * LANES:(j + 1) * LANES])
    return out


def _mla_body(q_ref, k_ref, vt_ref, o_ref, s_ref, m_ref, *, kt):
    nk = k_ref.shape[0]
    rb = s_ref.shape[2]
    nrb = q_ref.shape[0] // rb
    nheads = q_ref.shape[1] // LANES

    def score_pass(r, head):
        rows = pl.ds(pl.multiple_of(r * rb, rb), rb)
        sl = slice(head * LANES, (head + 1) * LANES)
        slot = head % 2
        s = lax.dot_general(k_ref[:, sl], q_ref[rows, sl], NT_DIMS, preferred_element_type=F32)
        s_ref[slot] = s
        m_ref[slot] = jnp.max(s, axis=0, keepdims=True)

    def value_pass(head):
        hs = slice(head * LANES, (head + 1) * LANES)
        slot = head % 2
        m = m_ref[slot]
        o = jnp.zeros((LANES, rb), F32)
        for c in range(0, nk, kt):
            p = jnp.exp2(s_ref[slot, c:c + kt, :] - m)
            o = o + _dot(vt_ref[hs, c:c + kt], p.astype(BF))
        return o

    def row_block(r, carry):
        top = lax.broadcasted_iota(jnp.int32, (LANES, rb), 0) < MLA_V
        outs = []
        for head in range(nheads):
            if head + 1 < nheads:
                score_pass(r, head + 1)
            else:
                score_pass(jnp.minimum(r + 1, nrb - 1), 0)
            outs.append(value_pass(head))
        pairs = [jnp.where(top, o0 / o0[MLA_V:MLA_V + 1, :], o1 / o1[0:1, :]).T
                 for o0, o1 in zip(outs[0::2], outs[1::2])]
        o_ref[pl.ds(pl.multiple_of(r * rb, rb), rb), :] = jnp.concatenate(pairs, axis=-1).astype(BF)
        return carry

    score_pass(0, 0)
    lax.fori_loop(0, nrb, row_block, 0)


def _mla(q, k, vt, batch):
    t, hq = q.shape
    s = t // batch
    rb = _tile(s, 512)
    nv = MLA_HEADS * MLA_V
    q3, k3 = (a.reshape(batch, s, hq) for a in (q, k))
    blk = pl.BlockSpec((None, s, hq), lambda b: (b, 0, 0))
    o = pl.pallas_call(
        functools.partial(_mla_body, kt=_tile(s, 256)),
        grid=(batch,),
        in_specs=[blk, blk, pl.BlockSpec((hq, s), lambda b: (0, b))],
        out_specs=pl.BlockSpec((None, s, nv), lambda b: (b, 0, 0)),
        out_shape=jax.ShapeDtypeStruct((batch, s, nv), BF),
        scratch_shapes=[pltpu.VMEM((2, s, rb), F32), pltpu.VMEM((2, 1, rb), F32)],
        compiler_params=_params("parallel"),
        name="mla_attn",
    )(q3, k3, vt)
    return o.reshape(t, nv)


def _fnet_fold_body(ha_ref, hm_ref, hx_ref, nw_ref, cc_ref, sc_ref, ec_ref, es_ref):
    tm = ha_ref.shape[0]
    nw = nw_ref[...]
    u_a = _rms(ha_ref[...], nw)
    u_m = _rms(hm_ref[...], nw)
    u_x = _rms(hx_ref[...], nw)[0:1]
    r = lax.broadcasted_iota(jnp.int32, (tm, tm), 0)
    c = lax.broadcasted_iota(jnp.int32, (tm, tm), 1)
    perm = jnp.where(r + c == tm, 1.0, 0.0).astype(BF)
    hi = u_m.astype(BF)
    low = (u_m - hi.astype(F32)).astype(BF)
    mirror = _dot(perm, hi) + _dot(perm, low)
    first = lax.broadcasted_iota(jnp.int32, u_a.shape, 0) == 0
    mirror = jnp.where(first, u_x, mirror)
    ue = (u_a + mirror).astype(BF)
    uo = (u_a - mirror).astype(BF)
    gc = cc_ref.shape[0]
    for g in range(ue.shape[-1] // gc):
        sl = slice(g * gc, (g + 1) * gc)
        ec_ref[:, sl] = _dot(ue[:, sl], cc_ref[...]).astype(BF)
        es_ref[:, sl] = _dot(uo[:, sl], sc_ref[...]).astype(BF)


def _fnet_seq_body(h_ref, hh_ref, nw_ref, cc_ref, cs_ref, ss_ref, ec_ref, es_ref, w_ref, o_ref, *, scale):
    ts = h_ref.shape[0]
    u_h = _rms(hh_ref[...], nw_ref[...]).astype(BF)
    gc = cc_ref.shape[0]
    x_h = jnp.concatenate([_dot(u_h[:, g * gc:(g + 1) * gc], cc_ref[...])
                           for g in range(u_h.shape[-1] // gc)], axis=-1)[0:1]
    hr = ts // 2 if ts % 16 == 0 else ts
    odd = lax.broadcasted_iota(jnp.int32, (hr, 1), 0) % 2 == 1
    for r0 in range(0, ts, hr):
        rows = pl.ds(pl.multiple_of(pl.program_id(1) * ts + r0, hr), hr)
        y = _dot(cs_ref[rows, :], ec_ref[...]) - _dot(ss_ref[rows, :], es_ref[...])
        y = y + jnp.where(odd, -1.0, 1.0) * x_h
        o_ref[r0:r0 + hr, :] = h_ref[r0:r0 + hr, :] + _dot((y * scale).astype(BF), w_ref[...])


def _dft_tables(n, fold):
    j = np.arange(n)[:, None]
    k = np.arange(n // 2 if fold else n)[None, :]
    ang = ((j * k) % n) * (2.0 * np.pi / n)
    cos, sin = np.cos(ang), np.sin(ang)
    if fold:
        cos[:, 0] = 0.5
    return jnp.asarray(cos, BF), jnp.asarray(sin, BF)


def _fnet(h, norm_w, w_out, layer, o, batch, tables):
    t, d = h.shape
    s = t // batch
    cc, sc, cs, ss = tables
    gc = cc.shape[0]
    sub = 8
    tm = _tile(s // 2, 256)
    nt, nf = s // tm, s // 2 // tm
    nwspec = pl.BlockSpec((None, 1, d), lambda b, i: (layer, 0, 0))
    table = _resident((gc, gc), lambda b, i: (0, 0))
    ec, es = pl.pallas_call(
        _fnet_fold_body,
        grid=(batch, nf),
        in_specs=[
            pl.BlockSpec((tm, d), lambda b, i: (b * nt + i, 0)),
            pl.BlockSpec((tm, d), lambda b, i: (b * nt + nt - 1 - i, 0)),
            pl.BlockSpec((sub, d), lambda b, i: (b * (s // sub) + ((nt - i) % nt) * (tm // sub), 0)),
            nwspec, table, table,
        ],
        out_specs=[pl.BlockSpec((tm, d), lambda b, i: (b * nf + i, 0))] * 2,
        out_shape=[jax.ShapeDtypeStruct((t // 2, d), BF)] * 2,
        compiler_params=_params("parallel", "parallel"),
        name="fnet_fold_channel_dft",
    )(h, h, h, norm_w, cc, sc)
    ts = _tile(s, 512)
    ns = s // ts
    return pl.pallas_call(
        functools.partial(_fnet_seq_body, scale=(s * gc) ** -0.5),
        grid=(batch, ns),
        in_specs=[
            pl.BlockSpec((ts, d), lambda b, i: (b * ns + i, 0)),
            pl.BlockSpec((sub, d), lambda b, i: (b * (s // sub) + s // 2 // sub, 0)),
            nwspec, table,
            _resident((s, s // 2), lambda b, i: (0, 0)),
            _resident((s, s // 2), lambda b, i: (0, 0)),
            pl.BlockSpec((s // 2, d), lambda b, i: (b, 0)),
            pl.BlockSpec((s // 2, d), lambda b, i: (b, 0)),
            _resident((None, d, d), lambda b, i: (o, 0, 0)),
        ],
        out_specs=pl.BlockSpec((ts, d), lambda b, i: (b * ns + i, 0)),
        out_shape=jax.ShapeDtypeStruct((t, d), F32),
        compiler_params=_params("parallel", "arbitrary"),
        name="fnet_seq_dft",
    )(h, h, norm_w, cc, cs, ss, ec, es, w_out)


def _mixer_weights(w_in, w_uq, w_ukv, dt_bias, a_log, ssd_d):
    w_in, w_uq, w_ukv = w_in.astype(BF), w_uq.astype(BF), w_ukv.astype(BF)
    ne, d, _ = w_in.shape
    inner = SSD_HEADS * SSD_HEAD_DIM
    conv_ch = inner + 2 * SSD_GROUPS * SSD_STATE
    o_z, o_xbc = 0, inner
    o_dt = o_xbc + conv_ch
    o_cq = o_dt + 2 * SSD_HEADS
    o_ckv = o_cq + MLA_Q_RANK
    o_kr = o_ckv + MLA_KV_RANK
    half = MLA_ROPE // 2
    pad = LANES - MLA_NOPE - MLA_ROPE
    zeros = lambda *s: jnp.zeros(s, w_in.dtype)
    w_dt = jnp.concatenate([w_in[:, :, o_dt:o_cq], zeros(ne, d, LANES - 2 * SSD_HEADS)], axis=-1)
    kr1 = w_in[:, :, o_kr:o_kr + half]
    kr2 = w_in[:, :, o_kr + half:o_kr + MLA_ROPE]
    kr_a = jnp.concatenate([zeros(ne, d, MLA_NOPE), kr1, kr2, zeros(ne, d, pad)], axis=-1)
    kr_b = jnp.concatenate([zeros(ne, d, MLA_NOPE), -kr2, kr1, zeros(ne, d, pad)], axis=-1)
    pieces = [w_in[:, :, o_z:o_xbc], w_in[:, :, o_xbc:o_dt], w_dt, w_in[:, :, o_cq:o_ckv],
              w_in[:, :, o_ckv:o_kr], jnp.concatenate([kr_a, kr_b], axis=-1)]
    cols, c = [], 0
    for p in pieces:
        cols.append((c, c + p.shape[-1]))
        c += p.shape[-1]
    w_all = jnp.concatenate(pieces, axis=-1).astype(BF)

    uq = w_uq.reshape(ne, MLA_Q_RANK, MLA_HEADS, MLA_NOPE + MLA_ROPE)
    q_nope, q1, q2 = uq[..., :MLA_NOPE], uq[..., MLA_NOPE:MLA_NOPE + half], uq[..., MLA_NOPE + half:]
    zq = lambda n: jnp.zeros((ne, MLA_Q_RANK, MLA_HEADS, n), w_uq.dtype)
    hq = MLA_HEADS * LANES
    wqa = jnp.concatenate([q_nope, q1, q2, zq(pad)], axis=-1).reshape(ne, MLA_Q_RANK, hq).astype(BF)
    wqb = jnp.concatenate([zq(MLA_NOPE), -q2, q1, zq(pad)], axis=-1).reshape(ne, MLA_Q_RANK, hq).astype(BF)

    ukv = w_ukv.reshape(ne, MLA_KV_RANK, MLA_HEADS, MLA_NOPE + MLA_V)
    zkv = lambda *s: jnp.zeros((ne, MLA_KV_RANK) + s, w_ukv.dtype)
    wkn = jnp.concatenate([ukv[..., :MLA_NOPE], zkv(MLA_HEADS, LANES - MLA_NOPE)], axis=-1)
    wkn = wkn.reshape(ne, MLA_KV_RANK, hq).astype(BF)
    vv = ukv[..., MLA_NOPE:].reshape(ne, MLA_KV_RANK, MLA_HEADS // 2, 2, MLA_V)
    zv = zkv(MLA_HEADS // 2, MLA_V)
    wv = jnp.stack([jnp.concatenate([vv[:, :, :, 0], zv], axis=-1),
                    jnp.concatenate([zv, vv[:, :, :, 1]], axis=-1)], axis=3)
    wvt = jnp.swapaxes(wv.reshape(ne, MLA_KV_RANK, hq), 1, 2).astype(BF)

    padl = lambda a: jnp.concatenate([a, jnp.zeros((ne, LANES - a.shape[-1]), a.dtype)], axis=-1)[:, None, :]
    bias = padl(dt_bias.reshape(ne, 2 * SSD_HEADS))
    a_neg = padl(-jnp.exp(a_log.reshape(ne, 2 * SSD_HEADS)))
    d_exp = jnp.repeat(ssd_d, SSD_HEAD_DIM, axis=-1)[:, None, :]
    return w_all, tuple(cols), wqa, wqb, wkn, wvt, bias, a_neg, d_exp


def _rope_tables(positions):
    inv = 1.0 / (ROPE_THETA ** (jnp.arange(0, MLA_ROPE, 2, dtype=F32) / MLA_ROPE))
    ang = inv[:, None] * positions.astype(F32).reshape(1, -1)
    cos, sin = jnp.cos(ang), jnp.sin(ang)
    t = cos.shape[1]
    pad = LANES - MLA_NOPE - MLA_ROPE
    cos_t = jnp.concatenate([jnp.ones((MLA_NOPE, t), F32), cos, cos, jnp.zeros((pad, t), F32)], axis=0).T
    sin_t = jnp.concatenate([jnp.zeros((MLA_NOPE, t), F32), sin, sin, jnp.zeros((pad, t), F32)], axis=0).T
    return cos_t, sin_t


def kernel(x, mem, positions, mem_norm, final_norm, ffn1_norm, ffn1_w_gu, ffn1_w_down, mix_norm, xa_norm,
           xa_wq, xa_wkv, xa_wo, ffn2_norm, ffn2_w_gu, ffn2_w_down, w_in, conv_w, conv_b, dt_bias, a_log,
           ssd_d, ssd_norm, q_norm, w_uq, kv_norm, w_ukv, w_out, fnet_w_out):
    batch, seq, d = x.shape
    depth = ffn1_norm.shape[0]
    t = batch * seq
    bf = lambda a: a.astype(BF)
    row3 = lambda a: a[:, None, :]

    kv = _kvproj(mem.reshape(-1, d), mem_norm[None, :], bf(xa_wkv)).reshape(depth, batch, mem.shape[1], 2 * d)
    w_all, cols, wqa, wqb, wkn, wvt, bias, a_neg, d_exp = _mixer_weights(w_in, w_uq, w_ukv, dt_bias, a_log, ssd_d)
    cos_t, sin_t = _rope_tables(positions)
    gc = d // FNET_GROUPS
    tables = _dft_tables(gc, fold=False) + _dft_tables(seq, fold=True)
    f1n, f2n, mxn, xan = row3(ffn1_norm), row3(ffn2_norm), row3(mix_norm), row3(xa_norm)
    gu, dn = _cast_layer(ffn1_w_gu, 0), _cast_layer(ffn1_w_down, 0)
    wq, wo, w_mix_out, w_fnet = bf(xa_wq), bf(xa_wo), bf(w_out), bf(fnet_w_out)
    ssd_nw, qn, kvn = row3(ssd_norm), row3(q_norm), row3(kv_norm)

    h = x.reshape(t, d)
    for layer in range(depth):
        h, gu, dn = _ffn(h, f1n, gu, dn, layer, cast_next=(ffn2_w_gu, ffn2_w_down, layer))
        if layer % 2 == 0:
            e = layer // 2
            z, xbc, dt_raw, q, k, vt = _inproj(h, mxn, w_all, cols, qn, kvn, wqa, wqb, wkn, wvt,
                                              cos_t, sin_t, layer, e)
            y_ssd = _ssd(xbc, dt_raw, z, conv_w, row3(conv_b), bias, a_neg, d_exp, ssd_nw, e, batch)
            o_mla = _mla(q, k, vt, batch)
            mix = (y_ssd, o_mla, w_mix_out, e)
        else:
            h = _fnet(h, mxn, w_fnet, layer, layer // 2, batch, tables)
            mix = None
        h = _xattn(h, xan, wq, kv, wo, layer, batch, mix=mix)
        if layer + 1 < depth:
            h, gu, dn = _ffn(h, f2n, gu, dn, layer, cast_next=(ffn1_w_gu, ffn1_w_down, layer + 1))
        else:
            h = _ffn(h, f2n, gu, dn, layer, final_w=final_norm[None, :])
    return h.reshape(batch, seq, d)
```
